```python
import jax, jax.numpy as jnp
from jax import lax
import numpy as np

D_MODEL = 1024
BATCH = 32
SEQ = 2048
DEPTH = 1

PLE_DIM = 256
ATT_HEADS = 8
ATT_KV_HEADS = 2
HEAD_DIM = 64
ATT_WIDTH = ATT_HEADS * HEAD_DIM
KV_WIDTH = ATT_KV_HEADS * HEAD_DIM
WINDOW = 128
BLOCK_Q = 128
ROPE_THETA = 10000.0
RWKV_HEADS = 8
RWKV_HEAD = 64
RWKV_WIDTH = RWKV_HEADS * RWKV_HEAD
DECAY_LORA = 64
AAA_LORA = 64
GATE_LORA = 128
RWKV_GN_EPS = 64e-5
ATT_COLS = ATT_WIDTH + 2 * KV_WIDTH
SHIFT_COLS = 3 * RWKV_WIDTH + DECAY_LORA + AAA_LORA + GATE_LORA
GATE_COLS = 2 * D_MODEL
IN_WIDTH = ATT_COLS + SHIFT_COLS + GATE_COLS
N_GROUPS = 4
EXPERTS_PER_GROUP = 8
N_EXPERTS = N_GROUPS * EXPERTS_PER_GROUP
TOP_K = 2
D_EXPERT = 512
MOE_BLOCK = 256
NORM_EPS = 1e-6
NEG_INF = -1e30

kernel_name = "hybrid_swa_rwkv7_hiermoe_block"


def rmsnorm(x, g):
    xf = x.astype(jnp.float32)
    y = xf * lax.rsqrt(jnp.mean(xf * xf, axis=-1, keepdims=True) + NORM_EPS)
    return (y * g.astype(jnp.float32)).astype(x.dtype)


def apply_rope(t, pos):
    half = t.shape[-1] // 2
    inv_freq = ROPE_THETA ** (-jnp.arange(half, dtype=jnp.float32) / half)
    ang = pos.astype(jnp.float32)[..., None] * inv_freq
    cos = jnp.cos(ang)[:, :, None, :]
    sin = jnp.sin(ang)[:, :, None, :]
    tf = t.astype(jnp.float32)
    t1, t2 = tf[..., :half], tf[..., half:]
    return jnp.concatenate([t1 * cos - t2 * sin, t2 * cos + t1 * sin], axis=-1).astype(t.dtype)


def sliding_window_attention(q, k, v, sinks):
    B, S, H, Dh = q.shape
    nb = S // BLOCK_Q
    grp = H // ATT_KV_HEADS
    qb = q.reshape(B, nb, BLOCK_Q, ATT_KV_HEADS, grp, Dh)
    kb = k.reshape(B, nb, BLOCK_Q, ATT_KV_HEADS, Dh)
    vb = v.reshape(B, nb, BLOCK_Q, ATT_KV_HEADS, Dh)
    prev = lambda t: jnp.concatenate([jnp.zeros_like(t[:, :1]), t[:, :-1]], axis=1)
    kk = jnp.concatenate([prev(kb), kb], axis=2)
    vv = jnp.concatenate([prev(vb), vb], axis=2)
    scores = jnp.einsum('bnqkgd,bnskd->bnkgqs', qb, kk,
                        preferred_element_type=jnp.float32) * (Dh ** -0.5)
    qi = jnp.arange(BLOCK_Q)[:, None]
    si = jnp.arange(2 * BLOCK_Q)[None, :]
    diff = qi + BLOCK_Q - si
    band = (diff >= 0) & (diff < WINDOW)
    exists = (jnp.arange(nb)[:, None, None] > 0) | (si >= BLOCK_Q)[None]
    valid = band[None] & exists
    scores = jnp.where(valid[None, :, None, None], scores, NEG_INF)
    sink = sinks.astype(jnp.float32).reshape(ATT_KV_HEADS, grp)[None, None, :, :, None, None]
    m = jnp.maximum(jnp.max(scores, axis=-1, keepdims=True), sink)
    e = jnp.exp(scores - m)
    probs = e / (jnp.sum(e, axis=-1, keepdims=True) + jnp.exp(sink - m))
    out = jnp.einsum('bnkgqs,bnskd->bnqkgd', probs.astype(v.dtype), vv)
    return out.reshape(B, S, H * Dh)


def token_shift(y, mu):
    prev = jnp.pad(y, ((0, 0), (1, 0), (0, 0)))[:, :-1]
    return y + (prev - y) * mu


def rwkv7_time_mix(r, k, v, xw, xa, xg, w0, w_decay_up, a0, w_aaa_up, w_gate_up,
                   k_k, k_a, r_k, ln_x_w, ln_x_b):
    B, S, C = r.shape
    H, N = RWKV_HEADS, RWKV_HEAD
    f32 = jnp.float32
    r, k, v = r.astype(f32), k.astype(f32), v.astype(f32)
    w = -jax.nn.softplus(-(w0.astype(f32) + jnp.tanh(xw.astype(f32)) @ w_decay_up.astype(f32))) - 0.5
    decay = jnp.exp(-jnp.exp(w))
    a = jax.nn.sigmoid(a0.astype(f32) + xa.astype(f32) @ w_aaa_up.astype(f32))
    g = jax.nn.sigmoid(xg.astype(f32)) @ w_gate_up.astype(f32)
    kk = (k * k_k.astype(f32)).reshape(B, S, H, N)
    kk = kk / jnp.maximum(jnp.sqrt(jnp.sum(kk * kk, axis=-1, keepdims=True)), 1e-12)
    k = k * (1.0 + (a - 1.0) * k_a.astype(f32))

    def step(state, inp):
        r_t, w_t, k_t, v_t, kk_t, a_t = inp
        sa = jnp.einsum('bhvk,bhk->bhv', state, -kk_t)
        state = (state * w_t[:, :, None, :]
                 + sa[..., None] * (kk_t * a_t)[:, :, None, :]
                 + v_t[..., None] * k_t[:, :, None, :])
        y_t = jnp.einsum('bhvk,bhk->bhv', state, r_t)
        return state, y_t

    to_seq = lambda t: t.reshape(B, S, H, N).transpose(1, 0, 2, 3)
    xs = (to_seq(r), to_seq(decay), to_seq(k), to_seq(v), kk.transpose(1, 0, 2, 3), to_seq(a))
    _, ys = lax.scan(step, jnp.zeros((B, H, N, N), f32), xs)
    y = ys.transpose(1, 0, 2, 3)
    mu = jnp.mean(y, axis=-1, keepdims=True)
    var = jnp.mean(jnp.square(y - mu), axis=-1, keepdims=True)
    y = ((y - mu) * lax.rsqrt(var + RWKV_GN_EPS)).reshape(B, S, C)
    y = y * ln_x_w.astype(f32) + ln_x_b.astype(f32)
    bonus = jnp.sum((r * k * r_k.astype(f32)).reshape(B, S, H, N), axis=-1, keepdims=True) * v.reshape(B, S, H, N)
    y = y + bonus.reshape(B, S, C)
    return y * g


def hierarchical_moe(h, w_group, b_group, w_expert, b_expert, w_gate_e, w_up_e, w_down_e):
    B, S, D = h.shape
    T = B * S
    xt = h.reshape(T, D)
    g_prob = jax.nn.softmax((xt @ w_group + b_group).astype(jnp.float32), axis=-1)
    g_w, g_idx = lax.top_k(g_prob, 1)
    e_logits = (xt @ w_expert + b_expert).astype(jnp.float32).reshape(T, N_GROUPS, EXPERTS_PER_GROUP)
    e_sel = jnp.take_along_axis(e_logits, g_idx[:, :, None], axis=1)[:, 0]
    e_w, e_idx = lax.top_k(jax.nn.softmax(e_sel, axis=-1), TOP_K)
    e_w = e_w / jnp.sum(e_w, axis=-1, keepdims=True)
    weights = (g_w * e_w).reshape(-1)
    experts = (g_idx * EXPERTS_PER_GROUP + e_idx).reshape(-1)
    tokens = jnp.repeat(jnp.arange(T, dtype=jnp.int32), TOP_K)
    A = T * TOP_K
    order = jnp.argsort(experts)
    se, stok, sw = experts[order], tokens[order], weights[order]
    counts = jnp.bincount(experts, length=N_EXPERTS)
    starts = jnp.cumsum(counts) - counts
    padded = (counts + MOE_BLOCK - 1) // MOE_BLOCK * MOE_BLOCK
    pad_ends = jnp.cumsum(padded)
    pad_starts = pad_ends - padded
    dest = pad_starts[se] + jnp.arange(A) - starts[se]
    n_blocks = -(-A // MOE_BLOCK) + N_EXPERTS
    P = n_blocks * MOE_BLOCK
    row_tok = jnp.zeros((P,), jnp.int32).at[dest].set(stok)
    row_w = jnp.zeros((P,), jnp.float32).at[dest].set(sw.astype(jnp.float32))
    block_e = jnp.minimum(jnp.searchsorted(pad_ends, jnp.arange(n_blocks) * MOE_BLOCK, side='right'),
                          N_EXPERTS - 1)
    xrows = xt[row_tok].reshape(n_blocks, MOE_BLOCK, D)

    def expert_block(args):
        xb, e = args
        hid = jax.nn.silu(xb @ w_gate_e[e]) * (xb @ w_up_e[e])
        return hid @ w_down_e[e]

    yrows = lax.map(expert_block, (xrows, block_e)).reshape(P, D)
    out = jax.ops.segment_sum(yrows * row_w[:, None].astype(yrows.dtype), row_tok, num_segments=T)
    return out.reshape(B, S, D)


def setup_inputs(seed: int = 0) -> dict:
    key = jax.random.key(seed)
    ks = jax.random.split(key, 32)
    nrm = lambda k, shape, s: jax.random.normal(k, shape, jnp.float32) * s
    gain = lambda k, shape: 1.0 + 0.02 * jax.random.normal(k, shape, jnp.float32)
    L, D = DEPTH, D_MODEL
    offsets = jax.random.randint(ks[2], (BATCH, 1), 0, 4096, dtype=jnp.int32)
    positions = jnp.arange(SEQ, dtype=jnp.int32)[None, :] + offsets
    return {
        "x": nrm(ks[0], (BATCH, SEQ, D), 1.0),
        "p": nrm(ks[1], (DEPTH, BATCH, SEQ, PLE_DIM), 1.0),
        "positions": positions,
        "ln_mix": gain(ks[3], (L, D)),
        "w_in": nrm(ks[4], (L, D, IN_WIDTH), D ** -0.5),
        "mu_shift": jax.random.uniform(ks[5], (L, SHIFT_COLS), jnp.float32),
        "w0": jax.random.uniform(ks[6], (L, RWKV_WIDTH), jnp.float32, -6.0, 1.0),
        "w_decay_up": nrm(ks[7], (L, DECAY_LORA, RWKV_WIDTH), 0.1),
        "a0": nrm(ks[8], (L, RWKV_WIDTH), 0.1),
        "w_aaa_up": nrm(ks[9], (L, AAA_LORA, RWKV_WIDTH), 0.1),
        "w_gate_up": nrm(ks[10], (L, GATE_LORA, RWKV_WIDTH), GATE_LORA ** -0.5),
        "k_k": 0.85 + 0.05 * jax.random.normal(ks[11], (L, RWKV_WIDTH), jnp.float32),
        "k_a": 1.0 + 0.05 * jax.random.normal(ks[12], (L, RWKV_WIDTH), jnp.float32),
        "r_k": nrm(ks[13], (L, RWKV_WIDTH), 0.1),
        "ln_x_w": gain(ks[14], (L, RWKV_WIDTH)),
        "ln_x_b": nrm(ks[15], (L, RWKV_WIDTH), 0.02),
        "sinks": nrm(ks[16], (L, ATT_HEADS), 1.0),
        "w_branch_att": nrm(ks[17], (L, ATT_WIDTH, D), ATT_WIDTH ** -0.5),
        "w_branch_rwkv": nrm(ks[18], (L, RWKV_WIDTH, D), RWKV_WIDTH ** -0.5),
        "w_out": nrm(ks[19], (L, D, D), D ** -0.5),
        "ln_moe": gain(ks[20], (L, D)),
        "w_group": nrm(ks[21], (L, D, N_GROUPS), D ** -0.5),
        "b_group": nrm(ks[22], (L, N_GROUPS), 0.01),
        "w_expert": nrm(ks[23], (L, D, N_EXPERTS), D ** -0.5),
        "b_expert": nrm(ks[24], (L, N_EXPERTS), 0.01),
        "w_gate_e": nrm(ks[25], (L, N_EXPERTS, D, D_EXPERT), D ** -0.5),
        "w_up_e": nrm(ks[26], (L, N_EXPERTS, D, D_EXPERT), D ** -0.5),
        "w_down_e": nrm(ks[27], (L, N_EXPERTS, D_EXPERT, D), D_EXPERT ** -0.5),
        "ln_ple": gain(ks[28], (L, D)),
        "w_ple_gate": nrm(ks[29], (L, D, D), D ** -0.5),
        "w_ple_proj": nrm(ks[30], (L, PLE_DIM, D), PLE_DIM ** -0.5),
        "ln_final": gain(ks[31], (D,)),
    }


def reference(x, p, positions, ln_mix, w_in, mu_shift, w0, w_decay_up, a0, w_aaa_up, w_gate_up,
              k_k, k_a, r_k, ln_x_w, ln_x_b, sinks, w_branch_att, w_branch_rwkv, w_out,
              ln_moe, w_group, b_group, w_expert, b_expert, w_gate_e, w_up_e, w_down_e,
              ln_ple, w_ple_gate, w_ple_proj, ln_final):
    B, S, D = x.shape
    for i in range(DEPTH):
        h = rmsnorm(x, ln_mix[i])
        z = h @ w_in[i]
        z_att = z[..., :ATT_COLS]
        z_rwkv = token_shift(z[..., ATT_COLS:ATT_COLS + SHIFT_COLS], mu_shift[i].astype(z.dtype))
        z_gate = z[..., ATT_COLS + SHIFT_COLS:]
        q = z_att[..., :ATT_WIDTH].reshape(B, S, ATT_HEADS, HEAD_DIM)
        k = z_att[..., ATT_WIDTH:ATT_WIDTH + KV_WIDTH].reshape(B, S, ATT_KV_HEADS, HEAD_DIM)
        v = z_att[..., ATT_WIDTH + KV_WIDTH:].reshape(B, S, ATT_KV_HEADS, HEAD_DIM)
        y_att = sliding_window_attention(apply_rope(q, positions), apply_rope(k, positions), v, sinks[i])
        c0 = RWKV_WIDTH
        r_b = z_rwkv[..., :c0]
        k_b = z_rwkv[..., c0:2 * c0]
        v_b = z_rwkv[..., 2 * c0:3 * c0]
        xw = z_rwkv[..., 3 * c0:3 * c0 + DECAY_LORA]
        xa = z_rwkv[..., 3 * c0 + DECAY_LORA:3 * c0 + DECAY_LORA + AAA_LORA]
        xg = z_rwkv[..., 3 * c0 + DECAY_LORA + AAA_LORA:]
        y_rwkv = rwkv7_time_mix(r_b, k_b, v_b, xw, xa, xg, w0[i], w_decay_up[i], a0[i], w_aaa_up[i],
                                w_gate_up[i], k_k[i], k_a[i], r_k[i], ln_x_w[i], ln_x_b[i]).astype(x.dtype)
        gates = jax.nn.sigmoid(z_gate.astype(jnp.float32)).astype(x.dtype)
        merged = gates[..., :D] * (y_att @ w_branch_att[i]) + gates[..., D:] * (y_rwkv @ w_branch_rwkv[i])
        x = x + merged @ w_out[i]
        x = x + hierarchical_moe(rmsnorm(x, ln_moe[i]), w_group[i], b_group[i], w_expert[i], b_expert[i],
                                 w_gate_e[i], w_up_e[i], w_down_e[i])
        ple_gate = jax.nn.sigmoid((rmsnorm(x, ln_ple[i]) @ w_ple_gate[i]).astype(jnp.float32)).astype(x.dtype)
        x = x + ple_gate * (p[i] @ w_ple_proj[i])
    return rmsnorm(x, ln_final)
```

```python
import functools

import jax
import jax.numpy as jnp
from jax import lax
from jax.experimental import pallas as pl
from jax.experimental.pallas import tpu as pltpu

F32 = jnp.float32
BF16 = jnp.bfloat16
HIGHEST = lax.Precision.HIGHEST

D_MODEL = 1024
PLE_DIM = 256
ATT_HEADS = 8
ATT_KV_HEADS = 2
HEAD_DIM = 64
ATT_WIDTH = ATT_HEADS * HEAD_DIM
KV_WIDTH = ATT_KV_HEADS * HEAD_DIM
WINDOW = 128
ROPE_THETA = 10000.0
RWKV_HEADS = 8
RWKV_HEAD = 64
RWKV_WIDTH = RWKV_HEADS * RWKV_HEAD
DECAY_LORA = 64
AAA_LORA = 64
GATE_LORA = 128
RWKV_GN_EPS = 64e-5
ATT_COLS = ATT_WIDTH + 2 * KV_WIDTH
SHIFT_COLS = 3 * RWKV_WIDTH + DECAY_LORA + AAA_LORA + GATE_LORA
GATE_COLS = 2 * D_MODEL
N_GROUPS = 4
EXPERTS_PER_GROUP = 8
N_EXPERTS = N_GROUPS * EXPERTS_PER_GROUP
D_EXPERT = 512
NORM_EPS = 1e-6
NEG_INF = -1e30

LANES = 128
VMEM_LIMIT = 56 * 1024 * 1024

ROW_TILE = 256
RWKV_CHUNK = 64
EXPERT_BLOCK = 256
GATHER_TILE = 256


def _params(*sem):
    return pltpu.CompilerParams(dimension_semantics=sem, vmem_limit_bytes=VMEM_LIMIT)


def _bdot(a, b):
    return jnp.dot(a.astype(BF16), b.astype(BF16), preferred_element_type=F32)


def _bdot_nt(a, b):
    return lax.dot_general(a.astype(BF16), b.astype(BF16), (((1,), (1,)), ((), ())),
                           preferred_element_type=F32)


def _fdot(a, b):
    return jnp.dot(a, b, preferred_element_type=F32, precision=HIGHEST)


def _rms(x, g):
    return x * lax.rsqrt(jnp.mean(x * x, axis=-1, keepdims=True) + NORM_EPS) * g


def _sigmoid(x):
    return 1.0 / (1.0 + jnp.exp(-x))


def _rope_table_kernel(pos_ref, invf_ref, cos_ref, sin_ref):
    ang = pos_ref[...].astype(F32) * invf_ref[...]
    cos_ref[...] = jnp.cos(ang)
    sin_ref[...] = jnp.sin(ang)


def _rope_tables(positions):
    T = positions.size
    half = HEAD_DIM // 2
    per_row = LANES // half
    inv_freq = ROPE_THETA ** (-jnp.arange(half, dtype=F32) / half)
    invf = jnp.tile(inv_freq, per_row).reshape(1, LANES)
    pos = jnp.repeat(positions.reshape(T // per_row, per_row), half, axis=1)
    rows = T // per_row
    tr = min(rows, 2048)
    cos, sin = pl.pallas_call(
        _rope_table_kernel,
        out_shape=(jax.ShapeDtypeStruct((rows, LANES), F32),) * 2,
        grid=(rows // tr,),
        in_specs=[pl.BlockSpec((tr, LANES), lambda i: (i, 0)),
                  pl.BlockSpec((1, LANES), lambda i: (0, 0))],
        out_specs=(pl.BlockSpec((tr, LANES), lambda i: (i, 0)),) * 2,
        compiler_params=_params("arbitrary"),
    )(pos, invf)
    cos = cos.reshape(T, half)
    sin = sin.reshape(T, half)
    return jnp.tile(cos, (1, 4)), jnp.tile(jnp.concatenate([-sin, sin], axis=1), (1, 2))


def _rope(t, cos, sin):
    n = t.shape[1]
    reps = n // LANES
    c = jnp.tile(cos, (1, reps)) if reps > 1 else cos
    s = jnp.tile(sin, (1, reps)) if reps > 1 else sin
    lane = lax.broadcasted_iota(jnp.int32, t.shape, 1)
    first_half = (lane % HEAD_DIM) < (HEAD_DIM // 2)
    partner = jnp.where(first_half, pltpu.roll(t, n - HEAD_DIM // 2, 1), pltpu.roll(t, HEAD_DIM // 2, 1))
    return t * c + partner * s


def _inproj_kernel(x_ref, g_ref, w_ref, mu_ref, cos_ref, sin_ref,
                   q_ref, k_ref, v_ref, zr_ref, gate_ref, carry_ref, *, tiles_per_seq):
    i = pl.program_id(0)
    tm = x_ref.shape[0]
    h = _rms(x_ref[...], g_ref[...]).astype(BF16)
    cos = cos_ref[...]
    sin = sin_ref[...]

    za = jnp.dot(h, w_ref[:, :ATT_COLS], preferred_element_type=F32)
    q_ref[...] = (_rope(za[:, :ATT_WIDTH], cos, sin) * (HEAD_DIM ** -0.5)).astype(BF16)
    k_ref[...] = _rope(za[:, ATT_WIDTH:ATT_WIDTH + KV_WIDTH], cos, sin).astype(BF16)
    v_ref[...] = za[:, ATT_WIDTH + KV_WIDTH:].astype(BF16)

    zs = jnp.dot(h, w_ref[:, ATT_COLS:ATT_COLS + SHIFT_COLS], preferred_element_type=F32)
    row = lax.broadcasted_iota(jnp.int32, zs.shape, 0)
    seq_start = (i % tiles_per_seq) == 0
    before = jnp.where(seq_start, 0.0, carry_ref[0:1, :])
    prev = jnp.where(row == 0, before, pltpu.roll(zs, 1, 0))
    carry_ref[0:1, :] = zs[tm - 1:tm, :]
    zr_ref[...] = zs + (prev - zs) * mu_ref[...]

    zg = jnp.dot(h, w_ref[:, ATT_COLS + SHIFT_COLS:], preferred_element_type=F32)
    gate_ref[...] = _sigmoid(zg).astype(BF16)


def _inproj(x2, ln, w_in, mu, cos, sin, seq):
    T, D = x2.shape
    tm = ROW_TILE
    in_width = w_in.shape[1]
    row = lambda n: pl.BlockSpec((tm, n), lambda i: (i, 0))
    full = lambda a: pl.BlockSpec(a.shape, lambda i: (0,) * a.ndim)
    ln = ln.reshape(1, D)
    mu = mu.reshape(1, SHIFT_COLS)
    return pl.pallas_call(
        functools.partial(_inproj_kernel, tiles_per_seq=seq // tm),
        out_shape=(jax.ShapeDtypeStruct((T, ATT_WIDTH), BF16),
                   jax.ShapeDtypeStruct((T, KV_WIDTH), BF16),
                   jax.ShapeDtypeStruct((T, KV_WIDTH), BF16),
                   jax.ShapeDtypeStruct((T, SHIFT_COLS), F32),
                   jax.ShapeDtypeStruct((T, GATE_COLS), BF16)),
        grid=(T // tm,),
        in_specs=[row(D), full(ln), full(w_in), full(mu), row(LANES), row(LANES)],
        out_specs=(row(ATT_WIDTH), row(KV_WIDTH), row(KV_WIDTH), row(SHIFT_COLS), row(GATE_COLS)),
        scratch_shapes=[pltpu.VMEM((8, SHIFT_COLS), F32)],
        compiler_params=_params("arbitrary"),
    )(x2, ln, w_in, mu, cos, sin)


def _attn_kernel(sink_ref, q_ref, kp_ref, kc_ref, vp_ref, vc_ref, o_ref):
    n = pl.program_id(1)
    bq = q_ref.shape[0]
    q = q_ref[...]
    kcat = jnp.concatenate([kp_ref[...], kc_ref[...]], axis=0)
    vcat = jnp.concatenate([vp_ref[...], vc_ref[...]], axis=0)
    qi = lax.broadcasted_iota(jnp.int32, (bq, 2 * bq), 0)
    si = lax.broadcasted_iota(jnp.int32, (bq, 2 * bq), 1)
    diff = qi + bq - si
    valid = (diff >= 0) & (diff < WINDOW) & ((si >= bq) | (n > 0))
    grp = ATT_HEADS // ATT_KV_HEADS
    outs = []
    for hd in range(ATT_HEADS):
        j = hd // grp
        qh = q[:, hd * HEAD_DIM:(hd + 1) * HEAD_DIM]
        kj = kcat[:, j * HEAD_DIM:(j + 1) * HEAD_DIM]
        vj = vcat[:, j * HEAD_DIM:(j + 1) * HEAD_DIM]
        s = lax.dot_general(qh, kj, (((1,), (1,)), ((), ())), preferred_element_type=F32)
        s = jnp.where(valid, s, NEG_INF)
        sink = sink_ref[hd]
        m = jnp.maximum(jnp.max(s, axis=-1, keepdims=True), sink)
        e = jnp.exp(s - m)
        denom = jnp.sum(e, axis=-1, keepdims=True) + jnp.exp(sink - m)
        pv = jnp.dot(e.astype(BF16), vj, preferred_element_type=F32)
        outs.append(pv / denom)
    o_ref[...] = jnp.concatenate(outs, axis=1).astype(o_ref.dtype)


def _attention(q, k, v, sinks, batch, seq):
    T = q.shape[0]
    bq = WINDOW
    nb = seq // bq
    cur = lambda b, n: (b * nb + n, 0)
    prev = lambda b, n: (jnp.maximum(b * nb + n - 1, 0), 0)
    return pl.pallas_call(
        _attn_kernel,
        out_shape=jax.ShapeDtypeStruct((T, ATT_WIDTH), BF16),
        grid=(batch, nb),
        in_specs=[pl.BlockSpec(memory_space=pltpu.SMEM),
                  pl.BlockSpec((bq, ATT_WIDTH), cur),
                  pl.BlockSpec((bq, KV_WIDTH), prev), pl.BlockSpec((bq, KV_WIDTH), cur),
                  pl.BlockSpec((bq, KV_WIDTH), prev), pl.BlockSpec((bq, KV_WIDTH), cur)],
        out_specs=pl.BlockSpec((bq, ATT_WIDTH), cur),
        compiler_params=_params("arbitrary", "arbitrary"),
    )(sinks, q, k, k, v, v)


def _rwkv_kernel(z_ref, w0_ref, wd_ref, a0_ref, wa_ref, wg_ref, kk_ref, ka_ref, rk_ref, lnw_ref, lnb_ref,
                 y_ref, s_ref):
    c = pl.program_id(1)

    @pl.when(c == 0)
    def _():
        s_ref[...] = jnp.zeros_like(s_ref)

    L = z_ref.shape[0]
    C = RWKV_WIDTH
    N = RWKV_HEAD
    r = z_ref[:, 0:C]
    k = z_ref[:, C:2 * C]
    v = z_ref[:, 2 * C:3 * C]
    xw = z_ref[:, 3 * C:3 * C + DECAY_LORA]
    xa = z_ref[:, 3 * C + DECAY_LORA:3 * C + DECAY_LORA + AAA_LORA]
    xg = z_ref[:, 3 * C + DECAY_LORA + AAA_LORA:]

    wlin = w0_ref[...] + _bdot(jnp.tanh(xw), wd_ref[...])
    softplus = jnp.maximum(-wlin, 0.0) + jnp.log(1.0 + jnp.exp(-jnp.abs(wlin)))
    logdecay = -jnp.exp(-softplus - 0.5)
    a = _sigmoid(a0_ref[...] + _bdot(xa, wa_ref[...]))
    g = _bdot(_sigmoid(xg), wg_ref[...])
    kk = k * kk_ref[...]
    k2 = k * (1.0 + (a - 1.0) * ka_ref[...])

    row = lax.broadcasted_iota(jnp.int32, (L, L), 0)
    col = lax.broadcasted_iota(jnp.int32, (L, L), 1)
    lower = row > col
    lower_eq = row >= col
    cum = _fdot(lower_eq.astype(F32), logdecay)
    p_in = jnp.exp(cum)
    p_ex = jnp.exp(cum - logdecay)
    p_inv = jnp.exp(-cum)
    p_rem = jnp.exp(cum[L - 1:L, :] - cum)
    p_tot = p_in[L - 1:L, :]
    eye = (row == col).astype(F32)
    rk_bonus = r * k2 * rk_ref[...]

    outs = []
    for hd in range(RWKV_HEADS):
        sl = slice(hd * N, (hd + 1) * N)
        kk_h = kk[:, sl]
        norm = jnp.sqrt(jnp.sum(kk_h * kk_h, axis=-1, keepdims=True))
        kk_h = kk_h / jnp.maximum(norm, 1e-12)
        b_h = kk_h * a[:, sl]
        v_h = v[:, sl]
        ar = jnp.concatenate([-kk_h * p_ex[:, sl], r[:, sl] * p_in[:, sl]], axis=0)
        bk = jnp.concatenate([b_h * p_inv[:, sl], k2[:, sl] * p_inv[:, sl]], axis=0)
        bk_rem = jnp.concatenate([b_h * p_rem[:, sl], k2[:, sl] * p_rem[:, sl]], axis=0)
        m = _bdot_nt(ar, bk)
        m_ab = jnp.where(lower, m[:L, :L], 0.0)
        m_ak = jnp.where(lower, m[:L, L:], 0.0)
        m_rb = jnp.where(lower_eq, m[L:, :L], 0.0)
        m_rk = jnp.where(lower_eq, m[L:, L:], 0.0)
        inv = eye + m_ab
        pw = m_ab
        steps = max(1, (L - 1).bit_length()) - 1
        for _ in range(steps):
            pw = _fdot(pw, pw)
            inv = inv + _fdot(inv, pw)
        state = s_ref[hd]
        a_s = _bdot_nt(ar, state)
        u = _fdot(inv, a_s[:L] + _bdot(m_ak, v_h))
        y = a_s[L:] + _bdot(m_rb, u) + _bdot(m_rk, v_h)
        uv = jnp.concatenate([u, v_h], axis=0)
        upd = lax.dot_general(uv.astype(BF16), bk_rem.astype(BF16), (((0,), (0,)), ((), ())),
                              preferred_element_type=F32)
        s_ref[hd] = state * p_tot[:, sl] + upd

        mu = jnp.mean(y, axis=-1, keepdims=True)
        var = jnp.mean(jnp.square(y - mu), axis=-1, keepdims=True)
        yn = (y - mu) * lax.rsqrt(var + RWKV_GN_EPS)
        bonus = jnp.sum(rk_bonus[:, sl], axis=-1, keepdims=True) * v_h
        outs.append((yn, bonus))
    yn = jnp.concatenate([o[0] for o in outs], axis=1)
    bonus = jnp.concatenate([o[1] for o in outs], axis=1)
    y_ref[...] = ((yn * lnw_ref[...] + lnb_ref[...] + bonus) * g).astype(y_ref.dtype)


def _rwkv(zr, w0, wd, a0, wa, wg, k_k, k_a, r_k, ln_w, ln_b, batch, seq):
    T = zr.shape[0]
    L = RWKV_CHUNK
    nc = seq // L
    vec = lambda a: a.reshape(1, RWKV_WIDTH)
    full = lambda a: pl.BlockSpec(a.shape, lambda b, c: (0,) * a.ndim)
    args = (vec(w0), wd, vec(a0), wa, wg, vec(k_k), vec(k_a), vec(r_k), vec(ln_w), vec(ln_b))
    return pl.pallas_call(
        _rwkv_kernel,
        out_shape=jax.ShapeDtypeStruct((T, RWKV_WIDTH), BF16),
        grid=(batch, nc),
        in_specs=[pl.BlockSpec((L, SHIFT_COLS), lambda b, c: (b * nc + c, 0))] + [full(a) for a in args],
        out_specs=pl.BlockSpec((L, RWKV_WIDTH), lambda b, c: (b * nc + c, 0)),
        scratch_shapes=[pltpu.VMEM((RWKV_HEADS, RWKV_HEAD, RWKV_HEAD), F32)],
        compiler_params=_params("arbitrary", "arbitrary"),
    )(zr, *args)


def _merge_kernel(x_ref, ya_ref, yr_ref, gate_ref, wba_ref, wbr_ref, wo_ref, lnm_ref, wr_ref, br_ref,
                  x1_ref, h2_ref, route_ref, cnt_ref, cnt_scr):
    i = pl.program_id(0)

    @pl.when(i == 0)
    def _():
        cnt_scr[...] = jnp.zeros_like(cnt_scr)

    tm = x_ref.shape[0]
    D = D_MODEL
    gates = gate_ref[...].astype(F32)
    ya = jnp.dot(ya_ref[...], wba_ref[...], preferred_element_type=F32)
    yr = jnp.dot(yr_ref[...], wbr_ref[...], preferred_element_type=F32)
    merged = gates[:, :D] * ya + gates[:, D:] * yr
    x1 = x_ref[...] + jnp.dot(merged.astype(BF16), wo_ref[...], preferred_element_type=F32)
    x1_ref[...] = x1
    h2 = _rms(x1, lnm_ref[...])
    h2_ref[...] = h2

    logits = _fdot(h2, wr_ref[...]) + br_ref[...]
    lane = lax.broadcasted_iota(jnp.int32, logits.shape, 1)
    big = jnp.int32(1 << 20)
    gl = jnp.where(lane < N_GROUPS, logits, NEG_INF)
    gmax = jnp.max(gl, axis=-1, keepdims=True)
    gidx = jnp.min(jnp.where(gl == gmax, lane, big), axis=-1, keepdims=True)
    gsum = jnp.sum(jnp.where(lane < N_GROUPS, jnp.exp(logits - gmax), 0.0), axis=-1, keepdims=True)
    g_w = 1.0 / gsum
    lo = N_GROUPS + EXPERTS_PER_GROUP * gidx
    el = jnp.where((lane >= lo) & (lane < lo + EXPERTS_PER_GROUP), logits, NEG_INF)
    m1 = jnp.max(el, axis=-1, keepdims=True)
    i1 = jnp.min(jnp.where(el == m1, lane, big), axis=-1, keepdims=True)
    el2 = jnp.where(lane == i1, NEG_INF, el)
    m2 = jnp.max(el2, axis=-1, keepdims=True)
    i2 = jnp.min(jnp.where(el2 == m2, lane, big), axis=-1, keepdims=True)
    d = jnp.exp(m2 - m1)
    w1 = g_w / (1.0 + d)
    w2 = g_w * d / (1.0 + d)
    e1 = i1 - N_GROUPS
    e2 = i2 - N_GROUPS

    hit1 = lane == e1
    hit2 = lane == e2
    onehot = jnp.where(hit1 | hit2, 1.0, 0.0).astype(BF16)
    r_i = lax.broadcasted_iota(jnp.int32, (tm, tm), 0)
    c_i = lax.broadcasted_iota(jnp.int32, (tm, tm), 1)
    before = jnp.dot(jnp.where(r_i > c_i, 1.0, 0.0).astype(BF16), onehot, preferred_element_type=F32)
    before = before + cnt_scr[...]
    rank1 = jnp.sum(jnp.where(hit1, before, 0.0), axis=-1, keepdims=True)
    rank2 = jnp.sum(jnp.where(hit2, before, 0.0), axis=-1, keepdims=True)
    cnt_scr[...] = cnt_scr[...] + jnp.sum(onehot.astype(F32), axis=0, keepdims=True)
    cnt_ref[...] = cnt_scr[...]

    route = jnp.where(lane == 0, e1.astype(F32), 0.0)
    route = jnp.where(lane == 1, e2.astype(F32), route)
    route = jnp.where(lane == 2, rank1, route)
    route = jnp.where(lane == 3, rank2, route)
    route = jnp.where(lane == 4, w1, route)
    route = jnp.where(lane == 5, w2, route)
    route_ref[...] = route


def _merge(x2, y_att, y_rwkv, gates, wba, wbr, wo, ln_moe, w_router, b_router):
    T, D = x2.shape
    tm = ROW_TILE
    row = lambda n: pl.BlockSpec((tm, n), lambda i: (i, 0))
    full = lambda a: pl.BlockSpec(a.shape, lambda i: (0,) * a.ndim)
    ln_moe = ln_moe.reshape(1, D)
    return pl.pallas_call(
        _merge_kernel,
        out_shape=(jax.ShapeDtypeStruct((T, D), F32), jax.ShapeDtypeStruct((T, D), F32),
                   jax.ShapeDtypeStruct((T, LANES), F32), jax.ShapeDtypeStruct((1, LANES), F32)),
        grid=(T // tm,),
        in_specs=[row(D), row(ATT_WIDTH), row(RWKV_WIDTH), row(GATE_COLS),
                  full(wba), full(wbr), full(wo), full(ln_moe), full(w_router), full(b_router)],
        out_specs=(row(D), row(D), row(LANES), pl.BlockSpec((1, LANES), lambda i: (0, 0))),
        scratch_shapes=[pltpu.VMEM((1, LANES), F32)],
        compiler_params=_params("arbitrary"),
    )(x2, y_att, y_rwkv, gates, wba, wbr, wo, ln_moe, w_router, b_router)


def _dispatch_kernel(dest_hbm, h_ref, xrows_hbm, idx_smem, idx_sem, row_sem):
    i = pl.program_id(0)
    tm = h_ref.shape[0]
    load = pltpu.make_async_copy(dest_hbm.at[i], idx_smem, idx_sem)
    load.start()
    load.wait()

    def row_copy(t, slot):
        return pltpu.make_async_copy(h_ref.at[pl.ds(t, 1)], xrows_hbm.at[pl.ds(idx_smem[2 * t + slot], 1)], row_sem)

    def issue(t, carry):
        row_copy(t, 0).start()
        row_copy(t, 1).start()
        return carry

    lax.fori_loop(0, tm, issue, 0)

    def drain(t, carry):
        row_copy(t, 0).wait()
        row_copy(t, 1).wait()
        return carry

    lax.fori_loop(0, tm, drain, 0)


def _dispatch(h2, dest):
    T, D = h2.shape
    tm = GATHER_TILE
    dest_tiles = dest.reshape(T // tm, 2 * tm)
    return pl.pallas_call(
        _dispatch_kernel,
        out_shape=jax.ShapeDtypeStruct((2 * T, D), F32),
        grid=(T // tm,),
        in_specs=[pl.BlockSpec(memory_space=pl.ANY), pl.BlockSpec((tm, D), lambda i: (i, 0))],
        out_specs=pl.BlockSpec(memory_space=pl.ANY),
        scratch_shapes=[pltpu.SMEM((2 * tm,), jnp.int32), pltpu.SemaphoreType.DMA, pltpu.SemaphoreType.DMA],
        compiler_params=_params("arbitrary"),
    )(dest_tiles, h2)


def _expert_kernel(item_e, item_b, item_lo, item_hi, x_ref, wg_ref, wu_ref, wd_ref, y_ref):
    w = pl.program_id(0)
    lo = item_lo[w]
    hi = item_hi[w]

    @pl.when(hi > lo)
    def _():
        xb = x_ref[...].astype(BF16)
        hg = jnp.dot(xb, wg_ref[0], preferred_element_type=F32)
        hu = jnp.dot(xb, wu_ref[0], preferred_element_type=F32)
        hid = hg * _sigmoid(hg) * hu
        y = jnp.dot(hid.astype(BF16), wd_ref[0], preferred_element_type=F32)

        @pl.when(lo == 0)
        def _():
            y_ref[...] = y

        @pl.when(lo > 0)
        def _():
            rows = lax.broadcasted_iota(jnp.int32, y.shape, 0)
            y_ref[...] = jnp.where((rows >= lo) & (rows < hi), y, y_ref[...])


def _experts(xrows, items, wg, wu, wd):
    A, D = xrows.shape
    bm = EXPERT_BLOCK
    n_items = items[0].shape[0]
    return pl.pallas_call(
        _expert_kernel,
        out_shape=jax.ShapeDtypeStruct((A, D), F32),
        grid_spec=pltpu.PrefetchScalarGridSpec(
            num_scalar_prefetch=4,
            grid=(n_items,),
            in_specs=[pl.BlockSpec((bm, D), lambda w, ie, ib, lo, hi: (ib[w], 0)),
                      pl.BlockSpec((1, D, D_EXPERT), lambda w, ie, ib, lo, hi: (ie[w], 0, 0)),
                      pl.BlockSpec((1, D, D_EXPERT), lambda w, ie, ib, lo, hi: (ie[w], 0, 0)),
                      pl.BlockSpec((1, D_EXPERT, D), lambda w, ie, ib, lo, hi: (ie[w], 0, 0))],
            out_specs=pl.BlockSpec((bm, D), lambda w, ie, ib, lo, hi: (ib[w], 0)),
        ),
        compiler_params=_params("arbitrary"),
    )(*items, xrows, wg, wu, wd)


def _work_items(counts, n_rows):
    bm = EXPERT_BLOCK
    nb = n_rows // bm
    n_items = nb + N_EXPERTS
    ends = jnp.cumsum(counts)
    starts = ends - counts
    first_blk = starts // bm
    last_blk = (ends - 1) // bm
    per_e = jnp.where(counts > 0, last_blk - first_blk + 1, 0)
    item_end = jnp.cumsum(per_e)
    total = item_end[-1]
    w = jnp.arange(n_items, dtype=jnp.int32)
    wc = jnp.minimum(w, total - 1)
    e = jnp.minimum(jnp.searchsorted(item_end, wc, side='right'), N_EXPERTS - 1).astype(jnp.int32)
    b = first_blk[e] + (wc - (item_end[e] - per_e[e]))
    lo = jnp.maximum(starts[e], b * bm) - b * bm
    hi = jnp.minimum(ends[e], (b + 1) * bm) - b * bm
    live = w < total
    lo = jnp.where(live, lo, 0)
    hi = jnp.where(live, hi, 0)
    return (e.astype(jnp.int32), b.astype(jnp.int32), lo.astype(jnp.int32), hi.astype(jnp.int32)), starts


def _final_kernel(dest_hbm, yrows_hbm, x1_ref, route_ref, p_ref, lnp_ref, wpg_ref, wpp_ref, lnf_ref,
                  o_ref, idx_smem, rows_ref, idx_sem, row_sem):
    i = pl.program_id(0)
    tm = x1_ref.shape[0]
    load = pltpu.make_async_copy(dest_hbm.at[i], idx_smem, idx_sem)
    load.start()
    load.wait()

    def row_copy(t, slot):
        return pltpu.make_async_copy(yrows_hbm.at[pl.ds(idx_smem[2 * t + slot], 1)],
                                     rows_ref.at[slot, pl.ds(t, 1)], row_sem)

    def issue(t, carry):
        row_copy(t, 0).start()
        row_copy(t, 1).start()
        return carry

    lax.fori_loop(0, tm, issue, 0)

    pp = jnp.dot(p_ref[...].astype(BF16), wpp_ref[...], preferred_element_type=F32)

    def drain(t, carry):
        row_copy(t, 0).wait()
        row_copy(t, 1).wait()
        return carry

    lax.fori_loop(0, tm, drain, 0)

    route = route_ref[...]
    x2 = x1_ref[...] + route[:, 4:5] * rows_ref[0] + route[:, 5:6] * rows_ref[1]
    gate = _sigmoid(jnp.dot(_rms(x2, lnp_ref[...]).astype(BF16), wpg_ref[...], preferred_element_type=F32))
    x3 = x2 + gate * pp
    o_ref[...] = _rms(x3, lnf_ref[...])


def _final(dest, yrows, x1, route, p2, ln_ple, wpg, wpp, ln_final):
    T, D = x1.shape
    tm = GATHER_TILE
    dest_tiles = dest.reshape(T // tm, 2 * tm)
    row = lambda n: pl.BlockSpec((tm, n), lambda i: (i, 0))
    full = lambda a: pl.BlockSpec(a.shape, lambda i: (0,) * a.ndim)
    ln_ple = ln_ple.reshape(1, D)
    ln_final = ln_final.reshape(1, D)
    return pl.pallas_call(
        _final_kernel,
        out_shape=jax.ShapeDtypeStruct((T, D), F32),
        grid=(T // tm,),
        in_specs=[pl.BlockSpec(memory_space=pl.ANY), pl.BlockSpec(memory_space=pl.ANY),
                  row(D), row(LANES), row(PLE_DIM), full(ln_ple), full(wpg), full(wpp), full(ln_final)],
        out_specs=row(D),
        scratch_shapes=[pltpu.SMEM((2 * tm,), jnp.int32), pltpu.VMEM((2, tm, D), F32),
                        pltpu.SemaphoreType.DMA, pltpu.SemaphoreType.DMA],
        compiler_params=_params("arbitrary"),
    )(dest_tiles, yrows, x1, route, p2, ln_ple, wpg, wpp, ln_final)


def kernel(x, p, positions, ln_mix, w_in, mu_shift, w0, w_decay_up, a0, w_aaa_up, w_gate_up, k_k, k_a, r_k, ln_x_w, ln_x_b, sinks, w_branch_att, w_branch_rwkv, w_out, ln_moe, w_group, b_group, w_expert, b_expert, w_gate_e, w_up_e, w_down_e, ln_ple, w_ple_gate, w_ple_proj, ln_final):
    B, S, D = x.shape
    T = B * S
    depth = w_in.shape[0]
    assert D == D_MODEL and S % ROW_TILE == 0 and S % WINDOW == 0 and S % RWKV_CHUNK == 0
    assert T % GATHER_TILE == 0 and (2 * T) % EXPERT_BLOCK == 0
    cos, sin = _rope_tables(positions)
    x2 = x.reshape(T, D)
    out = None
    for i in range(depth):
        q, k, v, zr, gates = _inproj(x2, ln_mix[i], w_in[i].astype(BF16), mu_shift[i], cos, sin, S)
        y_att = _attention(q, k, v, sinks[i], B, S)
        y_rwkv = _rwkv(zr, w0[i], w_decay_up[i], a0[i], w_aaa_up[i], w_gate_up[i], k_k[i], k_a[i], r_k[i],
                       ln_x_w[i], ln_x_b[i], B, S)
        pad = LANES - N_GROUPS - N_EXPERTS
        w_router = jnp.concatenate([w_group[i], w_expert[i], jnp.zeros((D, pad), F32)], axis=1)
        b_router = jnp.concatenate([b_group[i], b_expert[i], jnp.zeros((pad,), F32)]).reshape(1, LANES)
        x1, h2, route, cnt = _merge(x2, y_att, y_rwkv, gates, w_branch_att[i].astype(BF16),
                                    w_branch_rwkv[i].astype(BF16), w_out[i].astype(BF16), ln_moe[i],
                                    w_router, b_router)
        counts = cnt[0, :N_EXPERTS].astype(jnp.int32)
        items, starts = _work_items(counts, 2 * T)
        dest = (starts[route[:, 0:2].astype(jnp.int32)] + route[:, 2:4].astype(jnp.int32)).reshape(-1)
        xrows = _dispatch(h2, dest)
        yrows = _experts(xrows, items, w_gate_e[i].astype(BF16), w_up_e[i].astype(BF16),
                         w_down_e[i].astype(BF16))
        last = i == depth - 1
        assert last, "the final-norm kernel closes the only layer"
        out = _final(dest, yrows, x1, route, p[i].reshape(T, PLE_DIM), ln_ple[i], w_ple_gate[i].astype(BF16),
                     w_ple_proj[i].astype(BF16), ln_final)
    return out.reshape(B, S, D)
```

```python
import functools

import jax
import jax.numpy as jnp
from jax import lax
from jax.experimental import pallas as pl
from jax.experimental.pallas import tpu as pltpu

F32 = jnp.float32
BF16 = jnp.bfloat16
HIGHEST = lax.Precision.HIGHEST

D_MODEL = 1024
PLE_DIM = 256
ATT_HEADS = 8
ATT_KV_HEADS = 2
HEAD_DIM = 64
ATT_WIDTH = ATT_HEADS * HEAD_DIM
KV_WIDTH = ATT_KV_HEADS * HEAD_DIM
WINDOW = 128
ROPE_THETA = 10000.0
RWKV_HEADS = 8
RWKV_HEAD = 64
RWKV_WIDTH = RWKV_HEADS * RWKV_HEAD
DECAY_LORA = 64
AAA_LORA = 64
GATE_LORA = 128
RWKV_GN_EPS = 64e-5
ATT_COLS = ATT_WIDTH + 2 * KV_WIDTH
SHIFT_COLS = 3 * RWKV_WIDTH + DECAY_LORA + AAA_LORA + GATE_LORA
GATE_COLS = 2 * D_MODEL
N_GROUPS = 4
EXPERTS_PER_GROUP = 8
N_EXPERTS = N_GROUPS * EXPERTS_PER_GROUP
D_EXPERT = 512
NORM_EPS = 1e-6
NEG_INF = -1e30

LANES = 128
VMEM_LIMIT = 56 * 1024 * 1024

ROW_TILE = 256
RWKV_CHUNK = 64
RWKV_CHUNKS_PER_STEP = 4
EXPERT_BLOCK = 256
GATHER_TILE = 256


def _params(*sem):
    return pltpu.CompilerParams(dimension_semantics=sem, vmem_limit_bytes=VMEM_LIMIT)


def _bdot(a, b):
    return jnp.dot(a.astype(BF16), b.astype(BF16), preferred_element_type=F32)


def _bdot_nt(a, b):
    return lax.dot_general(a.astype(BF16), b.astype(BF16), (((1,), (1,)), ((), ())),
                           preferred_element_type=F32)


def _fdot(a, b):
    return jnp.dot(a, b, preferred_element_type=F32, precision=HIGHEST)


def _rms(x, g):
    return x * lax.rsqrt(jnp.mean(x * x, axis=-1, keepdims=True) + NORM_EPS) * g


def _sigmoid(x):
    return 1.0 / (1.0 + jnp.exp(-x))


def _rope_table_kernel(pos_ref, invf_ref, cos_ref, sin_ref):
    ang = pos_ref[...].astype(F32) * invf_ref[...]
    cos_ref[...] = jnp.cos(ang)
    sin_ref[...] = jnp.sin(ang)


def _rope_tables(positions):
    T = positions.size
    half = HEAD_DIM // 2
    per_row = LANES // half
    inv_freq = ROPE_THETA ** (-jnp.arange(half, dtype=F32) / half)
    invf = jnp.tile(inv_freq, per_row).reshape(1, LANES)
    pos = jnp.repeat(positions.reshape(T // per_row, per_row), half, axis=1)
    rows = T // per_row
    tr = min(rows, 2048)
    cos, sin = pl.pallas_call(
        _rope_table_kernel,
        out_shape=(jax.ShapeDtypeStruct((rows, LANES), F32),) * 2,
        grid=(rows // tr,),
        in_specs=[pl.BlockSpec((tr, LANES), lambda i: (i, 0)),
                  pl.BlockSpec((1, LANES), lambda i: (0, 0))],
        out_specs=(pl.BlockSpec((tr, LANES), lambda i: (i, 0)),) * 2,
        compiler_params=_params("arbitrary"),
    )(pos, invf)
    cos = cos.reshape(T, half)
    sin = sin.reshape(T, half)
    return jnp.tile(cos, (1, 4)), jnp.tile(jnp.concatenate([-sin, sin], axis=1), (1, 2))


def _rope(t, cos, sin):
    n = t.shape[1]
    reps = n // LANES
    c = jnp.tile(cos, (1, reps)) if reps > 1 else cos
    s = jnp.tile(sin, (1, reps)) if reps > 1 else sin
    lane = lax.broadcasted_iota(jnp.int32, t.shape, 1)
    first_half = (lane % HEAD_DIM) < (HEAD_DIM // 2)
    partner = jnp.where(first_half, pltpu.roll(t, n - HEAD_DIM // 2, 1), pltpu.roll(t, HEAD_DIM // 2, 1))
    return t * c + partner * s


def _inproj_kernel(x_ref, g_ref, w_ref, mu_ref, cos_ref, sin_ref,
                   q_ref, k_ref, v_ref, zr_ref, gate_ref, carry_ref, *, tiles_per_seq):
    i = pl.program_id(0)
    tm = x_ref.shape[0]
    h = _rms(x_ref[...], g_ref[...]).astype(BF16)
    cos = cos_ref[...]
    sin = sin_ref[...]

    za = jnp.dot(h, w_ref[:, :ATT_COLS], preferred_element_type=F32)
    q_ref[...] = (_rope(za[:, :ATT_WIDTH], cos, sin) * (HEAD_DIM ** -0.5)).astype(BF16)
    k_ref[...] = _rope(za[:, ATT_WIDTH:ATT_WIDTH + KV_WIDTH], cos, sin).astype(BF16)
    v_ref[...] = za[:, ATT_WIDTH + KV_WIDTH:].astype(BF16)

    zs = jnp.dot(h, w_ref[:, ATT_COLS:ATT_COLS + SHIFT_COLS], preferred_element_type=F32)
    row = lax.broadcasted_iota(jnp.int32, zs.shape, 0)
    seq_start = (i % tiles_per_seq) == 0
    before = jnp.where(seq_start, 0.0, carry_ref[0:1, :])
    prev = jnp.where(row == 0, before, pltpu.roll(zs, 1, 0))
    carry_ref[0:1, :] = zs[tm - 1:tm, :]
    zr_ref[...] = zs + (prev - zs) * mu_ref[...]

    zg = jnp.dot(h, w_ref[:, ATT_COLS + SHIFT_COLS:], preferred_element_type=F32)
    gate_ref[...] = _sigmoid(zg).astype(BF16)


def _inproj(x2, ln, w_in, mu, cos, sin, seq):
    T, D = x2.shape
    tm = ROW_TILE
    in_width = w_in.shape[1]
    row = lambda n: pl.BlockSpec((tm, n), lambda i: (i, 0))
    full = lambda a: pl.BlockSpec(a.shape, lambda i: (0,) * a.ndim)
    ln = ln.reshape(1, D)
    mu = mu.reshape(1, SHIFT_COLS)
    return pl.pallas_call(
        functools.partial(_inproj_kernel, tiles_per_seq=seq // tm),
        out_shape=(jax.ShapeDtypeStruct((T, ATT_WIDTH), BF16),
                   jax.ShapeDtypeStruct((T, KV_WIDTH), BF16),
                   jax.ShapeDtypeStruct((T, KV_WIDTH), BF16),
                   jax.ShapeDtypeStruct((T, SHIFT_COLS), F32),
                   jax.ShapeDtypeStruct((T, GATE_COLS), BF16)),
        grid=(T // tm,),
        in_specs=[row(D), full(ln), full(w_in), full(mu), row(LANES), row(LANES)],
        out_specs=(row(ATT_WIDTH), row(KV_WIDTH), row(KV_WIDTH), row(SHIFT_COLS), row(GATE_COLS)),
        scratch_shapes=[pltpu.VMEM((8, SHIFT_COLS), F32)],
        compiler_params=_params("arbitrary"),
    )(x2, ln, w_in, mu, cos, sin)


def _attn_kernel(sink_ref, q_ref, kp_ref, kc_ref, vp_ref, vc_ref, o_ref):
    n = pl.program_id(1)
    bq = q_ref.shape[0]
    q = q_ref[...]
    kcat = jnp.concatenate([kp_ref[...], kc_ref[...]], axis=0)
    vcat = jnp.concatenate([vp_ref[...], vc_ref[...]], axis=0)
    qi = lax.broadcasted_iota(jnp.int32, (bq, 2 * bq), 0)
    si = lax.broadcasted_iota(jnp.int32, (bq, 2 * bq), 1)
    diff = qi + bq - si
    valid = (diff >= 0) & (diff < WINDOW) & ((si >= bq) | (n > 0))
    grp = ATT_HEADS // ATT_KV_HEADS
    outs = []
    for hd in range(ATT_HEADS):
        j = hd // grp
        qh = q[:, hd * HEAD_DIM:(hd + 1) * HEAD_DIM]
        kj = kcat[:, j * HEAD_DIM:(j + 1) * HEAD_DIM]
        vj = vcat[:, j * HEAD_DIM:(j + 1) * HEAD_DIM]
        s = lax.dot_general(qh, kj, (((1,), (1,)), ((), ())), preferred_element_type=F32)
        s = jnp.where(valid, s, NEG_INF)
        sink = sink_ref[hd]
        m = jnp.maximum(jnp.max(s, axis=-1, keepdims=True), sink)
        e = jnp.exp(s - m)
        denom = jnp.sum(e, axis=-1, keepdims=True) + jnp.exp(sink - m)
        pv = jnp.dot(e.astype(BF16), vj, preferred_element_type=F32)
        outs.append(pv / denom)
    o_ref[...] = jnp.concatenate(outs, axis=1).astype(o_ref.dtype)


def _attention(q, k, v, sinks, batch, seq):
    T = q.shape[0]
    bq = WINDOW
    nb = seq // bq
    cur = lambda b, n: (b * nb + n, 0)
    prev = lambda b, n: (jnp.maximum(b * nb + n - 1, 0), 0)
    return pl.pallas_call(
        _attn_kernel,
        out_shape=jax.ShapeDtypeStruct((T, ATT_WIDTH), BF16),
        grid=(batch, nb),
        in_specs=[pl.BlockSpec(memory_space=pltpu.SMEM),
                  pl.BlockSpec((bq, ATT_WIDTH), cur),
                  pl.BlockSpec((bq, KV_WIDTH), prev), pl.BlockSpec((bq, KV_WIDTH), cur),
                  pl.BlockSpec((bq, KV_WIDTH), prev), pl.BlockSpec((bq, KV_WIDTH), cur)],
        out_specs=pl.BlockSpec((bq, ATT_WIDTH), cur),
        compiler_params=_params("arbitrary", "arbitrary"),
    )(sinks, q, k, k, v, v)


def _rwkv_kernel(z_ref, w0_ref, wd_ref, a0_ref, wa_ref, wg_ref, kk_ref, ka_ref, rk_ref, lnw_ref, lnb_ref,
                 y_ref, s_ref):
    c = pl.program_id(1)

    @pl.when(c == 0)
    def _():
        s_ref[...] = jnp.zeros_like(s_ref)

    L = RWKV_CHUNK
    rows = z_ref.shape[0]
    n_chunks = rows // L
    C = RWKV_WIDTH
    N = RWKV_HEAD
    r = z_ref[:, 0:C]
    k = z_ref[:, C:2 * C]
    v = z_ref[:, 2 * C:3 * C]
    xw = z_ref[:, 3 * C:3 * C + DECAY_LORA]
    xa = z_ref[:, 3 * C + DECAY_LORA:3 * C + DECAY_LORA + AAA_LORA]
    xg = z_ref[:, 3 * C + DECAY_LORA + AAA_LORA:]

    wlin = w0_ref[...] + _bdot(jnp.tanh(xw), wd_ref[...])
    softplus = jnp.maximum(-wlin, 0.0) + jnp.log(1.0 + jnp.exp(-jnp.abs(wlin)))
    logdecay = -jnp.exp(-softplus - 0.5)
    a = _sigmoid(a0_ref[...] + _bdot(xa, wa_ref[...]))
    g = _bdot(_sigmoid(xg), wg_ref[...])
    kk = k * kk_ref[...]
    k2 = k * (1.0 + (a - 1.0) * ka_ref[...])

    hr = lax.broadcasted_iota(jnp.int32, (C, C), 0) // N
    hc = lax.broadcasted_iota(jnp.int32, (C, C), 1) // N
    head_ones = jnp.where(hr == hc, 1.0, 0.0).astype(BF16)
    head_sum = lambda t: jnp.dot(t.astype(BF16), head_ones, preferred_element_type=F32)

    kkn = kk / jnp.maximum(jnp.sqrt(head_sum(kk * kk)), 1e-12)

    row = lax.broadcasted_iota(jnp.int32, (L, L), 0)
    col = lax.broadcasted_iota(jnp.int32, (L, L), 1)
    lower = row > col
    lower_eq = row >= col
    eye = jnp.where(row == col, 1.0, 0.0)
    tri = jnp.where(lower_eq, 1.0, 0.0).astype(BF16)
    ld_1 = logdecay.astype(BF16)
    rest = logdecay - ld_1.astype(F32)
    ld_2 = rest.astype(BF16)
    ld_3 = (rest - ld_2.astype(F32)).astype(BF16)
    cums = []
    for ci in range(n_chunks):
        rs = slice(ci * L, (ci + 1) * L)
        cums.append(jnp.dot(tri, ld_1[rs], preferred_element_type=F32)
                    + jnp.dot(tri, ld_2[rs], preferred_element_type=F32)
                    + jnp.dot(tri, ld_3[rs], preferred_element_type=F32))
    cum = jnp.concatenate(cums, axis=0) if n_chunks > 1 else cums[0]
    last = [cums[ci][L - 1:L, :] for ci in range(n_chunks)]
    cum_last = jnp.concatenate([jnp.broadcast_to(t, (L, C)) for t in last], axis=0) if n_chunks > 1 \
        else jnp.broadcast_to(last[0], (L, C))
    p_in = jnp.exp(cum)
    p_inv = jnp.exp(-cum)
    p_rem = jnp.exp(cum_last - cum)
    b = kkn * a
    at_f = (-kkn * jnp.exp(cum - logdecay)).astype(BF16)
    rt_f = r * p_in
    rt_b = rt_f.astype(BF16)
    bt_b = (b * p_inv).astype(BF16)
    kt_b = (k2 * p_inv).astype(BF16)
    br_b = (b * p_rem).astype(BF16)
    kr_b = (k2 * p_rem).astype(BF16)
    v_b = v.astype(BF16)

    dot = lambda x, y: jnp.dot(x.astype(BF16), y.astype(BF16), preferred_element_type=F32)
    dot_nt = lambda x, y: lax.dot_general(x.astype(BF16), y.astype(BF16), (((1,), (1,)), ((), ())),
                                          preferred_element_type=F32)
    dot_tn = lambda x, y: lax.dot_general(x.astype(BF16), y.astype(BF16), (((0,), (0,)), ((), ())),
                                          preferred_element_type=F32)

    pieces = [(ci, hd) for ci in range(n_chunks) for hd in range(RWKV_HEADS)]
    cut = lambda t, ci, hd: t[ci * L:(ci + 1) * L, hd * N:(hd + 1) * N]
    at = {p: cut(at_f, *p) for p in pieces}
    vv = {p: cut(v_b, *p) for p in pieces}
    brm = {p: cut(br_b, *p) for p in pieces}
    m = {p: dot_nt(jnp.concatenate([at[p], cut(rt_b, *p)], axis=0),
                   jnp.concatenate([cut(bt_b, *p), cut(kt_b, *p)], axis=0)) for p in pieces}
    m_ab = {p: jnp.where(lower, m[p][:L, :L], 0.0) for p in pieces}
    m_ak = {p: jnp.where(lower, m[p][:L, L:], 0.0).astype(BF16) for p in pieces}
    m_rb = {p: jnp.where(lower_eq, m[p][L:, :L], 0.0).astype(BF16) for p in pieces}
    m_rk = {p: jnp.where(lower_eq, m[p][L:, L:], 0.0).astype(BF16) for p in pieces}
    inv = {p: eye + m_ab[p] for p in pieces}
    pw = {p: m_ab[p].astype(BF16) for p in pieces}
    for _ in range(max(1, (L - 1).bit_length()) - 1):
        pw_f = {p: dot(pw[p], pw[p]) for p in pieces}
        pw = {p: pw_f[p].astype(BF16) for p in pieces}
        inv = {p: inv[p] + dot(inv[p], pw[p]) for p in pieces}
    inv = {p: inv[p].astype(BF16) for p in pieces}
    x1 = {p: dot(m_ak[p], vv[p]) for p in pieces}
    w = {p: dot(inv[p], at[p]).astype(BF16) for p in pieces}
    u0 = {p: dot(inv[p], x1[p]).astype(BF16) for p in pieces}
    rq = {p: (cut(rt_f, *p) + dot(m_rb[p], w[p])).astype(BF16) for p in pieces}
    y0 = {p: dot(m_rb[p], u0[p]) + dot(m_rk[p], vv[p]) for p in pieces}
    gg = {p: dot_tn(w[p], brm[p]).astype(BF16) for p in pieces}
    hh = {p: dot_tn(u0[p], brm[p]) + dot_tn(vv[p], cut(kr_b, *p)) for p in pieces}

    ys = {}
    for hd in range(RWKV_HEADS):
        state = s_ref[hd]
        for ci in range(n_chunks):
            p = (ci, hd)
            sb = state.astype(BF16)
            ys[p] = y0[p] + dot_nt(rq[p], sb)
            p_tot = p_in[(ci + 1) * L - 1:(ci + 1) * L, hd * N:(hd + 1) * N]
            state = state * p_tot + dot(sb, gg[p]) + hh[p]
        s_ref[hd] = state
    y = jnp.concatenate([jnp.concatenate([ys[(ci, hd)] for hd in range(RWKV_HEADS)], axis=1)
                         for ci in range(n_chunks)], axis=0)

    mu = head_sum(y) * (1.0 / N)
    dev = y - mu
    var = head_sum(dev * dev) * (1.0 / N)
    yn = dev * lax.rsqrt(var + RWKV_GN_EPS)
    bonus = head_sum(r * k2 * rk_ref[...]) * v
    y_ref[...] = ((yn * lnw_ref[...] + lnb_ref[...] + bonus) * g).astype(y_ref.dtype)


def _rwkv(zr, w0, wd, a0, wa, wg, k_k, k_a, r_k, ln_w, ln_b, batch, seq):
    T = zr.shape[0]
    L = RWKV_CHUNK * RWKV_CHUNKS_PER_STEP
    nc = seq // L
    vec = lambda a: a.reshape(1, RWKV_WIDTH)
    full = lambda a: pl.BlockSpec(a.shape, lambda b, c: (0,) * a.ndim)
    args = (vec(w0), wd, vec(a0), wa, wg, vec(k_k), vec(k_a), vec(r_k), vec(ln_w), vec(ln_b))
    return pl.pallas_call(
        _rwkv_kernel,
        out_shape=jax.ShapeDtypeStruct((T, RWKV_WIDTH), BF16),
        grid=(batch, nc),
        in_specs=[pl.BlockSpec((L, SHIFT_COLS), lambda b, c: (b * nc + c, 0))] + [full(a) for a in args],
        out_specs=pl.BlockSpec((L, RWKV_WIDTH), lambda b, c: (b * nc + c, 0)),
        scratch_shapes=[pltpu.VMEM((RWKV_HEADS, RWKV_HEAD, RWKV_HEAD), F32)],
        compiler_params=_params("arbitrary", "arbitrary"),
    )(zr, *args)


def _merge_kernel(x_ref, ya_ref, yr_ref, gate_ref, wba_ref, wbr_ref, wo_ref, lnm_ref, wr_ref, br_ref,
                  x1_ref, h2_ref, route_ref, cnt_ref, cnt_scr):
    i = pl.program_id(0)

    @pl.when(i == 0)
    def _():
        cnt_scr[...] = jnp.zeros_like(cnt_scr)

    tm = x_ref.shape[0]
    D = D_MODEL
    gates = gate_ref[...].astype(F32)
    ya = jnp.dot(ya_ref[...], wba_ref[...], preferred_element_type=F32)
    yr = jnp.dot(yr_ref[...], wbr_ref[...], preferred_element_type=F32)
    merged = gates[:, :D] * ya + gates[:, D:] * yr
    x1 = x_ref[...] + jnp.dot(merged.astype(BF16), wo_ref[...], preferred_element_type=F32)
    x1_ref[...] = x1
    h2 = _rms(x1, lnm_ref[...])
    h2_ref[...] = h2

    logits = _fdot(h2, wr_ref[...]) + br_ref[...]
    lane = lax.broadcasted_iota(jnp.int32, logits.shape, 1)
    big = jnp.int32(1 << 20)
    gl = jnp.where(lane < N_GROUPS, logits, NEG_INF)
    gmax = jnp.max(gl, axis=-1, keepdims=True)
    gidx = jnp.min(jnp.where(gl == gmax, lane, big), axis=-1, keepdims=True)
    gsum = jnp.sum(jnp.where(lane < N_GROUPS, jnp.exp(logits - gmax), 0.0), axis=-1, keepdims=True)
    g_w = 1.0 / gsum
    lo = N_GROUPS + EXPERTS_PER_GROUP * gidx
    el = jnp.where((lane >= lo) & (lane < lo + EXPERTS_PER_GROUP), logits, NEG_INF)
    m1 = jnp.max(el, axis=-1, keepdims=True)
    i1 = jnp.min(jnp.where(el == m1, lane, big), axis=-1, keepdims=True)
    el2 = jnp.where(lane == i1, NEG_INF, el)
    m2 = jnp.max(el2, axis=-1, keepdims=True)
    i2 = jnp.min(jnp.where(el2 == m2, lane, big), axis=-1, keepdims=True)
    d = jnp.exp(m2 - m1)
    w1 = g_w / (1.0 + d)
    w2 = g_w * d / (1.0 + d)
    e1 = i1 - N_GROUPS
    e2 = i2 - N_GROUPS

    hit1 = lane == e1
    hit2 = lane == e2
    onehot = jnp.where(hit1 | hit2, 1.0, 0.0).astype(BF16)
    r_i = lax.broadcasted_iota(jnp.int32, (tm, tm), 0)
    c_i = lax.broadcasted_iota(jnp.int32, (tm, tm), 1)
    before = jnp.dot(jnp.where(r_i > c_i, 1.0, 0.0).astype(BF16), onehot, preferred_element_type=F32)
    before = before + cnt_scr[...]
    rank1 = jnp.sum(jnp.where(hit1, before, 0.0), axis=-1, keepdims=True)
    rank2 = jnp.sum(jnp.where(hit2, before, 0.0), axis=-1, keepdims=True)
    cnt_scr[...] = cnt_scr[...] + jnp.sum(onehot.astype(F32), axis=0, keepdims=True)
    cnt_ref[...] = cnt_scr[...]

    route = jnp.where(lane == 0, e1.astype(F32), 0.0)
    route = jnp.where(lane == 1, e2.astype(F32), route)
    route = jnp.where(lane == 2, rank1, route)
    route = jnp.where(lane == 3, rank2, route)
    route = jnp.where(lane == 4, w1, route)
    route = jnp.where(lane == 5, w2, route)
    route_ref[...] = route


def _merge(x2, y_att, y_rwkv, gates, wba, wbr, wo, ln_moe, w_router, b_router):
    T, D = x2.shape
    tm = ROW_TILE
    row = lambda n: pl.BlockSpec((tm, n), lambda i: (i, 0))
    full = lambda a: pl.BlockSpec(a.shape, lambda i: (0,) * a.ndim)
    ln_moe = ln_moe.reshape(1, D)
    return pl.pallas_call(
        _merge_kernel,
        out_shape=(jax.ShapeDtypeStruct((T, D), F32), jax.ShapeDtypeStruct((T, D), F32),
                   jax.ShapeDtypeStruct((T, LANES), F32), jax.ShapeDtypeStruct((1, LANES), F32)),
        grid=(T // tm,),
        in_specs=[row(D), row(ATT_WIDTH), row(RWKV_WIDTH), row(GATE_COLS),
                  full(wba), full(wbr), full(wo), full(ln_moe), full(w_router), full(b_router)],
        out_specs=(row(D), row(D), row(LANES), pl.BlockSpec((1, LANES), lambda i: (0, 0))),
        scratch_shapes=[pltpu.VMEM((1, LANES), F32)],
        compiler_params=_params("arbitrary"),
    )(x2, y_att, y_rwkv, gates, wba, wbr, wo, ln_moe, w_router, b_router)


def _dispatch_kernel(dest_hbm, h_ref, xrows_hbm, idx_smem, idx_sem, row_sem):
    i = pl.program_id(0)
    tm = h_ref.shape[0]
    load = pltpu.make_async_copy(dest_hbm.at[i], idx_smem, idx_sem)
    load.start()
    load.wait()

    def row_copy(t, slot):
        return pltpu.make_async_copy(h_ref.at[pl.ds(t, 1)], xrows_hbm.at[pl.ds(idx_smem[2 * t + slot], 1)], row_sem)

    def issue(t, carry):
        row_copy(t, 0).start()
        row_copy(t, 1).start()
        return carry

    lax.fori_loop(0, tm, issue, 0)

    def drain(t, carry):
        row_copy(t, 0).wait()
        row_copy(t, 1).wait()
        return carry

    lax.fori_loop(0, tm, drain, 0)


def _dispatch(h2, dest):
    T, D = h2.shape
    tm = GATHER_TILE
    dest_tiles = dest.reshape(T // tm, 2 * tm)
    return pl.pallas_call(
        _dispatch_kernel,
        out_shape=jax.ShapeDtypeStruct((2 * T, D), F32),
        grid=(T // tm,),
        in_specs=[pl.BlockSpec(memory_space=pl.ANY), pl.BlockSpec((tm, D), lambda i: (i, 0))],
        out_specs=pl.BlockSpec(memory_space=pl.ANY),
        scratch_shapes=[pltpu.SMEM((2 * tm,), jnp.int32), pltpu.SemaphoreType.DMA, pltpu.SemaphoreType.DMA],
        compiler_params=_params("arbitrary"),
    )(dest_tiles, h2)


def _expert_kernel(item_e, item_b, item_lo, item_hi, x_ref, wg_ref, wu_ref, wd_ref, y_ref):
    w = pl.program_id(0)
    lo = item_lo[w]
    hi = item_hi[w]

    @pl.when(hi > lo)
    def _():
        xb = x_ref[...].astype(BF16)
        hg = jnp.dot(xb, wg_ref[0], preferred_element_type=F32)
        hu = jnp.dot(xb, wu_ref[0], preferred_element_type=F32)
        hid = hg * _sigmoid(hg) * hu
        y = jnp.dot(hid.astype(BF16), wd_ref[0], preferred_element_type=F32)

        @pl.when(lo == 0)
        def _():
            y_ref[...] = y

        @pl.when(lo > 0)
        def _():
            rows = lax.broadcasted_iota(jnp.int32, y.shape, 0)
            y_ref[...] = jnp.where((rows >= lo) & (rows < hi), y, y_ref[...])


def _experts(xrows, items, wg, wu, wd):
    A, D = xrows.shape
    bm = EXPERT_BLOCK
    n_items = items[0].shape[0]
    return pl.pallas_call(
        _expert_kernel,
        out_shape=jax.ShapeDtypeStruct((A, D), F32),
        grid_spec=pltpu.PrefetchScalarGridSpec(
            num_scalar_prefetch=4,
            grid=(n_items,),
            in_specs=[pl.BlockSpec((bm, D), lambda w, ie, ib, lo, hi: (ib[w], 0)),
                      pl.BlockSpec((1, D, D_EXPERT), lambda w, ie, ib, lo, hi: (ie[w], 0, 0)),
                      pl.BlockSpec((1, D, D_EXPERT), lambda w, ie, ib, lo, hi: (ie[w], 0, 0)),
                      pl.BlockSpec((1, D_EXPERT, D), lambda w, ie, ib, lo, hi: (ie[w], 0, 0))],
            out_specs=pl.BlockSpec((bm, D), lambda w, ie, ib, lo, hi: (ib[w], 0)),
        ),
        compiler_params=_params("arbitrary"),
    )(*items, xrows, wg, wu, wd)


def _work_items(counts, n_rows):
    bm = EXPERT_BLOCK
    nb = n_rows // bm
    n_items = nb + N_EXPERTS
    ends = jnp.cumsum(counts)
    starts = ends - counts
    first_blk = starts // bm
    last_blk = (ends - 1) // bm
    per_e = jnp.where(counts > 0, last_blk - first_blk + 1, 0)
    item_end = jnp.cumsum(per_e)
    total = item_end[-1]
    w = jnp.arange(n_items, dtype=jnp.int32)
    wc = jnp.minimum(w, total - 1)
    e = jnp.minimum(jnp.sum((item_end[None, :] <= wc[:, None]).astype(jnp.int32), axis=1), N_EXPERTS - 1)
    b = first_blk[e] + (wc - (item_end[e] - per_e[e]))
    lo = jnp.maximum(starts[e], b * bm) - b * bm
    hi = jnp.minimum(ends[e], (b + 1) * bm) - b * bm
    live = w < total
    lo = jnp.where(live, lo, 0)
    hi = jnp.where(live, hi, 0)
    return (e.astype(jnp.int32), b.astype(jnp.int32), lo.astype(jnp.int32), hi.astype(jnp.int32)), starts


def _final_kernel(dest_hbm, yrows_hbm, x1_ref, route_ref, p_ref, lnp_ref, wpg_ref, wpp_ref, lnf_ref,
                  o_ref, idx_smem, rows_ref, idx_sem, row_sem):
    i = pl.program_id(0)
    tm = x1_ref.shape[0]
    load = pltpu.make_async_copy(dest_hbm.at[i], idx_smem, idx_sem)
    load.start()
    load.wait()

    def row_copy(t, slot):
        return pltpu.make_async_copy(yrows_hbm.at[pl.ds(idx_smem[2 * t + slot], 1)],
                                     rows_ref.at[slot, pl.ds(t, 1)], row_sem)

    def issue(t, carry):
        row_copy(t, 0).start()
        row_copy(t, 1).start()
        return carry

    lax.fori_loop(0, tm, issue, 0)

    pp = jnp.dot(p_ref[...].astype(BF16), wpp_ref[...], preferred_element_type=F32)

    def drain(t, carry):
        row_copy(t, 0).wait()
        row_copy(t, 1).wait()
        return carry

    lax.fori_loop(0, tm, drain, 0)

    route = route_ref[...]
    x2 = x1_ref[...] + route[:, 4:5] * rows_ref[0] + route[:, 5:6] * rows_ref[1]
    gate = _sigmoid(jnp.dot(_rms(x2, lnp_ref[...]).astype(BF16), wpg_ref[...], preferred_element_type=F32))
    x3 = x2 + gate * pp
    o_ref[...] = _rms(x3, lnf_ref[...])


def _final(dest, yrows, x1, route, p2, ln_ple, wpg, wpp, ln_final):
    T, D = x1.shape
    tm = GATHER_TILE
    dest_tiles = dest.reshape(T // tm, 2 * tm)
    row = lambda n: pl.BlockSpec((tm, n), lambda i: (i, 0))
    full = lambda a: pl.BlockSpec(a.shape, lambda i: (0,) * a.ndim)
    ln_ple = ln_ple.reshape(1, D)
    ln_final = ln_final.reshape(1, D)
    return pl.pallas_call(
        _final_kernel,
        out_shape=jax.ShapeDtypeStruct((T, D), F32),
        grid=(T // tm,),
        in_specs=[pl.BlockSpec(memory_space=pl.ANY), pl.BlockSpec(memory_space=pl.ANY),
                  row(D), row(LANES), row(PLE_DIM), full(ln_ple), full(wpg), full(wpp), full(ln_final)],
        out_specs=row(D),
        scratch_shapes=[pltpu.SMEM((2 * tm,), jnp.int32), pltpu.VMEM((2, tm, D), F32),
                        pltpu.SemaphoreType.DMA, pltpu.SemaphoreType.DMA],
        compiler_params=_params("arbitrary"),
    )(dest_tiles, yrows, x1, route, p2, ln_ple, wpg, wpp, ln_final)


def kernel(x, p, positions, ln_mix, w_in, mu_shift, w0, w_decay_up, a0, w_aaa_up, w_gate_up, k_k, k_a, r_k, ln_x_w, ln_x_b, sinks, w_branch_att, w_branch_rwkv, w_out, ln_moe, w_group, b_group, w_expert, b_expert, w_gate_e, w_up_e, w_down_e, ln_ple, w_ple_gate, w_ple_proj, ln_final):
    B, S, D = x.shape
    T = B * S
    depth = w_in.shape[0]
    assert D == D_MODEL and S % ROW_TILE == 0 and S % WINDOW == 0 and S % (RWKV_CHUNK * RWKV_CHUNKS_PER_STEP) == 0
    assert T % GATHER_TILE == 0 and (2 * T) % EXPERT_BLOCK == 0
    cos, sin = _rope_tables(positions)
    x2 = x.reshape(T, D)
    out = None
    for i in range(depth):
        q, k, v, zr, gates = _inproj(x2, ln_mix[i], w_in[i].astype(BF16), mu_shift[i], cos, sin, S)
        y_att = _attention(q, k, v, sinks[i], B, S)
        y_rwkv = _rwkv(zr, w0[i], w_decay_up[i], a0[i], w_aaa_up[i], w_gate_up[i], k_k[i], k_a[i], r_k[i],
                       ln_x_w[i], ln_x_b[i], B, S)
        pad = LANES - N_GROUPS - N_EXPERTS
        w_router = jnp.concatenate([w_group[i], w_expert[i], jnp.zeros((D, pad), F32)], axis=1)
        b_router = jnp.concatenate([b_group[i], b_expert[i], jnp.zeros((pad,), F32)]).reshape(1, LANES)
        x1, h2, route, cnt = _merge(x2, y_att, y_rwkv, gates, w_branch_att[i].astype(BF16),
                                    w_branch_rwkv[i].astype(BF16), w_out[i].astype(BF16), ln_moe[i],
                                    w_router, b_router)
        counts = cnt[0, :N_EXPERTS].astype(jnp.int32)
        items, starts = _work_items(counts, 2 * T)
        dest = (starts[route[:, 0:2].astype(jnp.int32)] + route[:, 2:4].astype(jnp.int32)).reshape(-1)
        xrows = _dispatch(h2, dest)
        yrows = _experts(xrows, items, w_gate_e[i].astype(BF16), w_up_e[i].astype(BF16),
                         w_down_e[i].astype(BF16))
        last = i == depth - 1
        assert last, "the final-norm kernel closes the only layer"
        out = _final(dest, yrows, x1, route, p[i].reshape(T, PLE_DIM), ln_ple[i], w_ple_gate[i].astype(BF16),
                     w_ple_proj[i].astype(BF16), ln_final)
    return out.reshape(B, S, D)
```

```python
import functools

import jax
import jax.numpy as jnp
from jax import lax
from jax.experimental import pallas as pl
from jax.experimental.pallas import tpu as pltpu

F32 = jnp.float32
BF16 = jnp.bfloat16
HIGHEST = lax.Precision.HIGHEST

D_MODEL = 1024
PLE_DIM = 256
ATT_HEADS = 8
ATT_KV_HEADS = 2
HEAD_DIM = 64
ATT_WIDTH = ATT_HEADS * HEAD_DIM
KV_WIDTH = ATT_KV_HEADS * HEAD_DIM
WINDOW = 128
ROPE_THETA = 10000.0
RWKV_HEADS = 8
RWKV_HEAD = 64
RWKV_WIDTH = RWKV_HEADS * RWKV_HEAD
DECAY_LORA = 64
AAA_LORA = 64
GATE_LORA = 128
RWKV_GN_EPS = 64e-5
ATT_COLS = ATT_WIDTH + 2 * KV_WIDTH
SHIFT_COLS = 3 * RWKV_WIDTH + DECAY_LORA + AAA_LORA + GATE_LORA
GATE_COLS = 2 * D_MODEL
N_GROUPS = 4
EXPERTS_PER_GROUP = 8
N_EXPERTS = N_GROUPS * EXPERTS_PER_GROUP
D_EXPERT = 512
NORM_EPS = 1e-6
NEG_INF = -1e30

LANES = 128
SLAB_ROWS = D_MODEL // LANES
VMEM_LIMIT = 56 * 1024 * 1024

ROW_TILE = 256
RWKV_CHUNK = 64
RWKV_CHUNKS_PER_STEP = 4
EXPERT_BLOCK = 256
GATHER_TILE = 256
DISPATCH_TILE = 512


def _params(*sem):
    return pltpu.CompilerParams(dimension_semantics=sem, vmem_limit_bytes=VMEM_LIMIT)


def _bdot(a, b):
    return jnp.dot(a.astype(BF16), b.astype(BF16), preferred_element_type=F32)


def _bdot_nt(a, b):
    return lax.dot_general(a.astype(BF16), b.astype(BF16), (((1,), (1,)), ((), ())),
                           preferred_element_type=F32)


def _fdot(a, b):
    return jnp.dot(a, b, preferred_element_type=F32, precision=HIGHEST)


def _rms(x, g):
    return x * lax.rsqrt(jnp.mean(x * x, axis=-1, keepdims=True) + NORM_EPS) * g


def _sigmoid(x):
    return 1.0 / (1.0 + jnp.exp(-x))


def _store_slabs(ref, val):
    m = val.shape[0]
    for j in range(SLAB_ROWS):
        ref[pl.ds(j, m, stride=SLAB_ROWS), :] = val[:, j * LANES:(j + 1) * LANES]


def _load_slabs(ref, m):
    return jnp.concatenate([ref[pl.ds(j, m, stride=SLAB_ROWS), :] for j in range(SLAB_ROWS)], axis=1)


def _slab(ref, index):
    return ref.at[pl.ds(pl.multiple_of(index * SLAB_ROWS, SLAB_ROWS), SLAB_ROWS)]


def _rope_table_kernel(pos_ref, invf_ref, cos_ref, sin_ref):
    ang = pos_ref[...].astype(F32) * invf_ref[...]
    cos_ref[...] = jnp.cos(ang)
    sin_ref[...] = jnp.sin(ang)


def _rope_tables(positions):
    T = positions.size
    half = HEAD_DIM // 2
    per_row = LANES // half
    inv_freq = ROPE_THETA ** (-jnp.arange(half, dtype=F32) / half)
    invf = jnp.tile(inv_freq, per_row).reshape(1, LANES)
    pos = jnp.repeat(positions.reshape(T // per_row, per_row), half, axis=1)
    rows = T // per_row
    tr = min(rows, 2048)
    cos, sin = pl.pallas_call(
        _rope_table_kernel,
        out_shape=(jax.ShapeDtypeStruct((rows, LANES), F32),) * 2,
        grid=(rows // tr,),
        in_specs=[pl.BlockSpec((tr, LANES), lambda i: (i, 0)),
                  pl.BlockSpec((1, LANES), lambda i: (0, 0))],
        out_specs=(pl.BlockSpec((tr, LANES), lambda i: (i, 0)),) * 2,
        compiler_params=_params("arbitrary"),
    )(pos, invf)
    cos = cos.reshape(T, half)
    sin = sin.reshape(T, half)
    return jnp.tile(cos, (1, 4)), jnp.tile(jnp.concatenate([-sin, sin], axis=1), (1, 2))


def _rope(t, cos, sin):
    n = t.shape[1]
    reps = n // LANES
    c = jnp.tile(cos, (1, reps)) if reps > 1 else cos
    s = jnp.tile(sin, (1, reps)) if reps > 1 else sin
    lane = lax.broadcasted_iota(jnp.int32, t.shape, 1)
    first_half = (lane % HEAD_DIM) < (HEAD_DIM // 2)
    partner = jnp.where(first_half, pltpu.roll(t, n - HEAD_DIM // 2, 1), pltpu.roll(t, HEAD_DIM // 2, 1))
    return t * c + partner * s


def _inproj_kernel(x_ref, g_ref, w_ref, mu_ref, cos_ref, sin_ref,
                   q_ref, k_ref, v_ref, zr_ref, gate_ref, carry_ref, *, tiles_per_seq):
    i = pl.program_id(0)
    tm = x_ref.shape[0]
    h = _rms(x_ref[...], g_ref[...]).astype(BF16)
    cos = cos_ref[...]
    sin = sin_ref[...]

    za = jnp.dot(h, w_ref[:, :ATT_COLS], preferred_element_type=F32)
    q_ref[...] = (_rope(za[:, :ATT_WIDTH], cos, sin) * (HEAD_DIM ** -0.5)).astype(BF16)
    k_ref[...] = _rope(za[:, ATT_WIDTH:ATT_WIDTH + KV_WIDTH], cos, sin).astype(BF16)
    v_ref[...] = za[:, ATT_WIDTH + KV_WIDTH:].astype(BF16)

    zs = jnp.dot(h, w_ref[:, ATT_COLS:ATT_COLS + SHIFT_COLS], preferred_element_type=F32)
    row = lax.broadcasted_iota(jnp.int32, zs.shape, 0)
    seq_start = (i % tiles_per_seq) == 0
    before = jnp.where(seq_start, 0.0, carry_ref[0:1, :])
    prev = jnp.where(row == 0, before, pltpu.roll(zs, 1, 0))
    carry_ref[0:1, :] = zs[tm - 1:tm, :]
    zr_ref[...] = zs + (prev - zs) * mu_ref[...]

    zg = jnp.dot(h, w_ref[:, ATT_COLS + SHIFT_COLS:], preferred_element_type=F32)
    gate_ref[...] = _sigmoid(zg).astype(BF16)


def _inproj(x2, ln, w_in, mu, cos, sin, seq):
    T, D = x2.shape
    tm = ROW_TILE
    in_width = w_in.shape[1]
    row = lambda n: pl.BlockSpec((tm, n), lambda i: (i, 0))
    full = lambda a: pl.BlockSpec(a.shape, lambda i: (0,) * a.ndim)
    ln = ln.reshape(1, D)
    mu = mu.reshape(1, SHIFT_COLS)
    return pl.pallas_call(
        functools.partial(_inproj_kernel, tiles_per_seq=seq // tm),
        out_shape=(jax.ShapeDtypeStruct((T, ATT_WIDTH), BF16),
                   jax.ShapeDtypeStruct((T, KV_WIDTH), BF16),
                   jax.ShapeDtypeStruct((T, KV_WIDTH), BF16),
                   jax.ShapeDtypeStruct((T, SHIFT_COLS), F32),
                   jax.ShapeDtypeStruct((T, GATE_COLS), BF16)),
        grid=(T // tm,),
        in_specs=[row(D), full(ln), full(w_in), full(mu), row(LANES), row(LANES)],
        out_specs=(row(ATT_WIDTH), row(KV_WIDTH), row(KV_WIDTH), row(SHIFT_COLS), row(GATE_COLS)),
        scratch_shapes=[pltpu.VMEM((8, SHIFT_COLS), F32)],
        compiler_params=_params("arbitrary"),
    )(x2, ln, w_in, mu, cos, sin)


def _attn_kernel(sink_ref, q_ref, kp_ref, kc_ref, vp_ref, vc_ref, o_ref):
    n = pl.program_id(1)
    bq = q_ref.shape[0]
    q = q_ref[...]
    kcat = jnp.concatenate([kp_ref[...], kc_ref[...]], axis=0)
    vcat = jnp.concatenate([vp_ref[...], vc_ref[...]], axis=0)
    qi = lax.broadcasted_iota(jnp.int32, (bq, 2 * bq), 0)
    si = lax.broadcasted_iota(jnp.int32, (bq, 2 * bq), 1)
    diff = qi + bq - si
    valid = (diff >= 0) & (diff < WINDOW) & ((si >= bq) | (n > 0))
    grp = ATT_HEADS // ATT_KV_HEADS
    outs = []
    for hd in range(ATT_HEADS):
        j = hd // grp
        qh = q[:, hd * HEAD_DIM:(hd + 1) * HEAD_DIM]
        kj = kcat[:, j * HEAD_DIM:(j + 1) * HEAD_DIM]
        vj = vcat[:, j * HEAD_DIM:(j + 1) * HEAD_DIM]
        s = lax.dot_general(qh, kj, (((1,), (1,)), ((), ())), preferred_element_type=F32)
        s = jnp.where(valid, s, NEG_INF)
        sink = sink_ref[hd]
        m = jnp.maximum(jnp.max(s, axis=-1, keepdims=True), sink)
        e = jnp.exp(s - m)
        denom = jnp.sum(e, axis=-1, keepdims=True) + jnp.exp(sink - m)
        pv = jnp.dot(e.astype(BF16), vj, preferred_element_type=F32)
        outs.append(pv / denom)
    o_ref[...] = jnp.concatenate(outs, axis=1).astype(o_ref.dtype)


def _attention(q, k, v, sinks, batch, seq):
    T = q.shape[0]
    bq = WINDOW
    nb = seq // bq
    cur = lambda b, n: (b * nb + n, 0)
    prev = lambda b, n: (jnp.maximum(b * nb + n - 1, 0), 0)
    return pl.pallas_call(
        _attn_kernel,
        out_shape=jax.ShapeDtypeStruct((T, ATT_WIDTH), BF16),
        grid=(batch, nb),
        in_specs=[pl.BlockSpec(memory_space=pltpu.SMEM),
                  pl.BlockSpec((bq, ATT_WIDTH), cur),
                  pl.BlockSpec((bq, KV_WIDTH), prev), pl.BlockSpec((bq, KV_WIDTH), cur),
                  pl.BlockSpec((bq, KV_WIDTH), prev), pl.BlockSpec((bq, KV_WIDTH), cur)],
        out_specs=pl.BlockSpec((bq, ATT_WIDTH), cur),
        compiler_params=_params("arbitrary", "arbitrary"),
    )(sinks, q, k, k, v, v)


def _rwkv_kernel(z_ref, w0_ref, wd_ref, a0_ref, wa_ref, wg_ref, kk_ref, ka_ref, rk_ref, lnw_ref, lnb_ref,
                 y_ref, s_ref):
    c = pl.program_id(1)

    @pl.when(c == 0)
    def _():
        s_ref[...] = jnp.zeros_like(s_ref)

    L = RWKV_CHUNK
    rows = z_ref.shape[0]
    n_chunks = rows // L
    C = RWKV_WIDTH
    N = RWKV_HEAD
    r = z_ref[:, 0:C]
    k = z_ref[:, C:2 * C]
    v = z_ref[:, 2 * C:3 * C]
    xw = z_ref[:, 3 * C:3 * C + DECAY_LORA]
    xa = z_ref[:, 3 * C + DECAY_LORA:3 * C + DECAY_LORA + AAA_LORA]
    xg = z_ref[:, 3 * C + DECAY_LORA + AAA_LORA:]

    wlin = w0_ref[...] + _bdot(jnp.tanh(xw), wd_ref[...])
    softplus = jnp.maximum(-wlin, 0.0) + jnp.log(1.0 + jnp.exp(-jnp.abs(wlin)))
    logdecay = -jnp.exp(-softplus - 0.5)
    a = _sigmoid(a0_ref[...] + _bdot(xa, wa_ref[...]))
    g = _bdot(_sigmoid(xg), wg_ref[...])
    kk = k * kk_ref[...]
    k2 = k * (1.0 + (a - 1.0) * ka_ref[...])

    hr = lax.broadcasted_iota(jnp.int32, (C, C), 0) // N
    hc = lax.broadcasted_iota(jnp.int32, (C, C), 1) // N
    head_ones = jnp.where(hr == hc, 1.0, 0.0).astype(BF16)
    head_sum = lambda t: jnp.dot(t.astype(BF16), head_ones, preferred_element_type=F32)

    kkn = kk / jnp.maximum(jnp.sqrt(head_sum(kk * kk)), 1e-12)

    row = lax.broadcasted_iota(jnp.int32, (L, L), 0)
    col = lax.broadcasted_iota(jnp.int32, (L, L), 1)
    lower = row > col
    lower_eq = row >= col
    eye = jnp.where(row == col, 1.0, 0.0)
    tri = jnp.where(lower_eq, 1.0, 0.0).astype(BF16)
    ld_1 = logdecay.astype(BF16)
    rest = logdecay - ld_1.astype(F32)
    ld_2 = rest.astype(BF16)
    ld_3 = (rest - ld_2.astype(F32)).astype(BF16)
    cums = []
    for ci in range(n_chunks):
        rs = slice(ci * L, (ci + 1) * L)
        cums.append(jnp.dot(tri, ld_1[rs], preferred_element_type=F32)
                    + jnp.dot(tri, ld_2[rs], preferred_element_type=F32)
                    + jnp.dot(tri, ld_3[rs], preferred_element_type=F32))
    cum = jnp.concatenate(cums, axis=0) if n_chunks > 1 else cums[0]
    last = [cums[ci][L - 1:L, :] for ci in range(n_chunks)]
    cum_last = jnp.concatenate([jnp.broadcast_to(t, (L, C)) for t in last], axis=0) if n_chunks > 1 \
        else jnp.broadcast_to(last[0], (L, C))
    p_in = jnp.exp(cum)
    p_inv = jnp.exp(-cum)
    p_rem = jnp.exp(cum_last - cum)
    b = kkn * a
    at_f = (-kkn * jnp.exp(cum - logdecay)).astype(BF16)
    rt_f = r * p_in
    rt_b = rt_f.astype(BF16)
    bt_b = (b * p_inv).astype(BF16)
    kt_b = (k2 * p_inv).astype(BF16)
    br_b = (b * p_rem).astype(BF16)
    kr_b = (k2 * p_rem).astype(BF16)
    v_b = v.astype(BF16)

    dot = lambda x, y: jnp.dot(x.astype(BF16), y.astype(BF16), preferred_element_type=F32)
    dot_nt = lambda x, y: lax.dot_general(x.astype(BF16), y.astype(BF16), (((1,), (1,)), ((), ())),
                                          preferred_element_type=F32)
    dot_tn = lambda x, y: lax.dot_general(x.astype(BF16), y.astype(BF16), (((0,), (0,)), ((), ())),
                                          preferred_element_type=F32)

    pieces = [(ci, hd) for ci in range(n_chunks) for hd in range(RWKV_HEADS)]
    cut = lambda t, ci, hd: t[ci * L:(ci + 1) * L, hd * N:(hd + 1) * N]
    at = {p: cut(at_f, *p) for p in pieces}
    vv = {p: cut(v_b, *p) for p in pieces}
    brm = {p: cut(br_b, *p) for p in pieces}
    m = {p: dot_nt(jnp.concatenate([at[p], cut(rt_b, *p)], axis=0),
                   jnp.concatenate([cut(bt_b, *p), cut(kt_b, *p)], axis=0)) for p in pieces}
    m_ab = {p: jnp.where(lower, m[p][:L, :L], 0.0) for p in pieces}
    m_ak = {p: jnp.where(lower, m[p][:L, L:], 0.0).astype(BF16) for p in pieces}
    m_rb = {p: jnp.where(lower_eq, m[p][L:, :L], 0.0).astype(BF16) for p in pieces}
    m_rk = {p: jnp.where(lower_eq, m[p][L:, L:], 0.0).astype(BF16) for p in pieces}
    inv = {p: eye + m_ab[p] for p in pieces}
    pw = {p: m_ab[p].astype(BF16) for p in pieces}
    for _ in range(max(1, (L - 1).bit_length()) - 1):
        pw_f = {p: dot(pw[p], pw[p]) for p in pieces}
        pw = {p: pw_f[p].astype(BF16) for p in pieces}
        inv = {p: inv[p] + dot(inv[p], pw[p]) for p in pieces}
    inv = {p: inv[p].astype(BF16) for p in pieces}
    x1 = {p: dot(m_ak[p], vv[p]) for p in pieces}
    w = {p: dot(inv[p], at[p]).astype(BF16) for p in pieces}
    u0 = {p: dot(inv[p], x1[p]).astype(BF16) for p in pieces}
    rq = {p: (cut(rt_f, *p) + dot(m_rb[p], w[p])).astype(BF16) for p in pieces}
    y0 = {p: dot(m_rb[p], u0[p]) + dot(m_rk[p], vv[p]) for p in pieces}
    gg = {p: dot_tn(w[p], brm[p]).astype(BF16) for p in pieces}
    hh = {p: dot_tn(u0[p], brm[p]) + dot_tn(vv[p], cut(kr_b, *p)) for p in pieces}

    ys = {}
    for hd in range(RWKV_HEADS):
        state = s_ref[hd]
        for ci in range(n_chunks):
            p = (ci, hd)
            sb = state.astype(BF16)
            ys[p] = y0[p] + dot_nt(rq[p], sb)
            p_tot = p_in[(ci + 1) * L - 1:(ci + 1) * L, hd * N:(hd + 1) * N]
            state = state * p_tot + dot(sb, gg[p]) + hh[p]
        s_ref[hd] = state
    y = jnp.concatenate([jnp.concatenate([ys[(ci, hd)] for hd in range(RWKV_HEADS)], axis=1)
                         for ci in range(n_chunks)], axis=0)

    mu = head_sum(y) * (1.0 / N)
    dev = y - mu
    var = head_sum(dev * dev) * (1.0 / N)
    yn = dev * lax.rsqrt(var + RWKV_GN_EPS)
    bonus = head_sum(r * k2 * rk_ref[...]) * v
    y_ref[...] = ((yn * lnw_ref[...] + lnb_ref[...] + bonus) * g).astype(y_ref.dtype)


def _rwkv(zr, w0, wd, a0, wa, wg, k_k, k_a, r_k, ln_w, ln_b, batch, seq):
    T = zr.shape[0]
    L = RWKV_CHUNK * RWKV_CHUNKS_PER_STEP
    nc = seq // L
    vec = lambda a: a.reshape(1, RWKV_WIDTH)
    full = lambda a: pl.BlockSpec(a.shape, lambda b, c: (0,) * a.ndim)
    args = (vec(w0), wd, vec(a0), wa, wg, vec(k_k), vec(k_a), vec(r_k), vec(ln_w), vec(ln_b))
    return pl.pallas_call(
        _rwkv_kernel,
        out_shape=jax.ShapeDtypeStruct((T, RWKV_WIDTH), BF16),
        grid=(batch, nc),
        in_specs=[pl.BlockSpec((L, SHIFT_COLS), lambda b, c: (b * nc + c, 0))] + [full(a) for a in args],
        out_specs=pl.BlockSpec((L, RWKV_WIDTH), lambda b, c: (b * nc + c, 0)),
        scratch_shapes=[pltpu.VMEM((RWKV_HEADS, RWKV_HEAD, RWKV_HEAD), F32)],
        compiler_params=_params("arbitrary", "arbitrary"),
    )(zr, *args)


def _merge_kernel(x_ref, ya_ref, yr_ref, gate_ref, wba_ref, wbr_ref, wo_ref, lnm_ref, wr_ref, br_ref,
                  x1_ref, h2_ref, route_ref, cnt_ref, cnt_scr):
    i = pl.program_id(0)

    @pl.when(i == 0)
    def _():
        cnt_scr[...] = jnp.zeros_like(cnt_scr)

    tm = x_ref.shape[0]
    D = D_MODEL
    gates = gate_ref[...].astype(F32)
    ya = jnp.dot(ya_ref[...], wba_ref[...], preferred_element_type=F32)
    yr = jnp.dot(yr_ref[...], wbr_ref[...], preferred_element_type=F32)
    merged = gates[:, :D] * ya + gates[:, D:] * yr
    x1 = x_ref[...] + jnp.dot(merged.astype(BF16), wo_ref[...], preferred_element_type=F32)
    x1_ref[...] = x1
    h2 = _rms(x1, lnm_ref[...])
    _store_slabs(h2_ref, h2)

    h_hi = h2.astype(BF16)
    h_lo = (h2 - h_hi.astype(F32)).astype(BF16)
    parts = (jnp.dot(h_hi, wr_ref[...], preferred_element_type=F32)
             + jnp.dot(h_lo, wr_ref[...], preferred_element_type=F32))
    logits = parts[:, :LANES] + parts[:, LANES:] + br_ref[...]
    lane = lax.broadcasted_iota(jnp.int32, logits.shape, 1)
    big = jnp.int32(1 << 20)
    gl = jnp.where(lane < N_GROUPS, logits, NEG_INF)
    gmax = jnp.max(gl, axis=-1, keepdims=True)
    gidx = jnp.min(jnp.where(gl == gmax, lane, big), axis=-1, keepdims=True)
    gsum = jnp.sum(jnp.where(lane < N_GROUPS, jnp.exp(logits - gmax), 0.0), axis=-1, keepdims=True)
    g_w = 1.0 / gsum
    lo = N_GROUPS + EXPERTS_PER_GROUP * gidx
    el = jnp.where((lane >= lo) & (lane < lo + EXPERTS_PER_GROUP), logits, NEG_INF)
    m1 = jnp.max(el, axis=-1, keepdims=True)
    i1 = jnp.min(jnp.where(el == m1, lane, big), axis=-1, keepdims=True)
    el2 = jnp.where(lane == i1, NEG_INF, el)
    m2 = jnp.max(el2, axis=-1, keepdims=True)
    i2 = jnp.min(jnp.where(el2 == m2, lane, big), axis=-1, keepdims=True)
    d = jnp.exp(m2 - m1)
    w1 = g_w / (1.0 + d)
    w2 = g_w * d / (1.0 + d)
    e1 = i1 - N_GROUPS
    e2 = i2 - N_GROUPS

    hit1 = lane == e1
    hit2 = lane == e2
    onehot = jnp.where(hit1 | hit2, 1.0, 0.0).astype(BF16)
    r_i = lax.broadcasted_iota(jnp.int32, (tm, tm), 0)
    c_i = lax.broadcasted_iota(jnp.int32, (tm, tm), 1)
    before = jnp.dot(jnp.where(r_i > c_i, 1.0, 0.0).astype(BF16), onehot, preferred_element_type=F32)
    before = before + cnt_scr[...]
    rank1 = jnp.sum(jnp.where(hit1, before, 0.0), axis=-1, keepdims=True)
    rank2 = jnp.sum(jnp.where(hit2, before, 0.0), axis=-1, keepdims=True)
    cnt_scr[...] = cnt_scr[...] + jnp.sum(onehot.astype(F32), axis=0, keepdims=True)
    cnt_ref[...] = cnt_scr[...]

    route = jnp.where(lane == 0, e1.astype(F32), 0.0)
    route = jnp.where(lane == 1, e2.astype(F32), route)
    route = jnp.where(lane == 2, rank1, route)
    route = jnp.where(lane == 3, rank2, route)
    route = jnp.where(lane == 4, w1, route)
    route = jnp.where(lane == 5, w2, route)
    route_ref[...] = route


def _merge(x2, y_att, y_rwkv, gates, wba, wbr, wo, ln_moe, w_router, b_router):
    T, D = x2.shape
    tm = ROW_TILE
    row = lambda n: pl.BlockSpec((tm, n), lambda i: (i, 0))
    full = lambda a: pl.BlockSpec(a.shape, lambda i: (0,) * a.ndim)
    ln_moe = ln_moe.reshape(1, D)
    return pl.pallas_call(
        _merge_kernel,
        out_shape=(jax.ShapeDtypeStruct((T, D), F32), jax.ShapeDtypeStruct((T * SLAB_ROWS, LANES), F32),
                   jax.ShapeDtypeStruct((T, LANES), F32), jax.ShapeDtypeStruct((1, LANES), F32)),
        grid=(T // tm,),
        in_specs=[row(D), row(ATT_WIDTH), row(RWKV_WIDTH), row(GATE_COLS),
                  full(wba), full(wbr), full(wo), full(ln_moe), full(w_router), full(b_router)],
        out_specs=(row(D), pl.BlockSpec((tm * SLAB_ROWS, LANES), lambda i: (i, 0)), row(LANES),
                   pl.BlockSpec((1, LANES), lambda i: (0, 0))),
        scratch_shapes=[pltpu.VMEM((1, LANES), F32)],
        compiler_params=_params("arbitrary"),
    )(x2, y_att, y_rwkv, gates, wba, wbr, wo, ln_moe, w_router, b_router)


def _dispatch_kernel(dest_ref, h_ref, xrows_hbm, row_sem, *, tile):
    base = pl.program_id(0) * tile

    def row_copy(j, slot):
        return pltpu.make_async_copy(_slab(h_ref, j), _slab(xrows_hbm, dest_ref[2 * (base + j) + slot]), row_sem)

    def issue(j, carry):
        row_copy(j, 0).start()
        row_copy(j, 1).start()
        return carry

    lax.fori_loop(0, tile, issue, 0, unroll=8)

    def drain(j, carry):
        row_copy(j, 0).wait()
        row_copy(j, 1).wait()
        return carry

    lax.fori_loop(0, tile, drain, 0, unroll=8)


def _dispatch(h2_slabs, dest):
    T = h2_slabs.shape[0] // SLAB_ROWS
    tile = min(DISPATCH_TILE, T)
    assert T % tile == 0
    return pl.pallas_call(
        functools.partial(_dispatch_kernel, tile=tile),
        out_shape=jax.ShapeDtypeStruct((2 * T * SLAB_ROWS, LANES), F32),
        grid_spec=pltpu.PrefetchScalarGridSpec(
            num_scalar_prefetch=1,
            grid=(T // tile,),
            in_specs=[pl.BlockSpec((tile * SLAB_ROWS, LANES), lambda i, d: (i, 0))],
            out_specs=pl.BlockSpec(memory_space=pl.ANY),
            scratch_shapes=[pltpu.SemaphoreType.DMA],
        ),
        compiler_params=_params("arbitrary"),
    )(dest, h2_slabs)


def _expert_kernel(item_e, item_b, item_lo, item_hi, x_ref, wg_ref, wu_ref, wd_ref, y_ref):
    w = pl.program_id(0)
    lo = item_lo[w]
    hi = item_hi[w]

    @pl.when(hi > lo)
    def _():
        bm = x_ref.shape[0] // SLAB_ROWS
        xb = _load_slabs(x_ref, bm).astype(BF16)
        hg = jnp.dot(xb, wg_ref[0], preferred_element_type=F32)
        hu = jnp.dot(xb, wu_ref[0], preferred_element_type=F32)
        hid = hg * _sigmoid(hg) * hu
        y = jnp.dot(hid.astype(BF16), wd_ref[0], preferred_element_type=F32)

        @pl.when(lo == 0)
        def _():
            _store_slabs(y_ref, y)

        @pl.when(lo > 0)
        def _():
            rows = lax.broadcasted_iota(jnp.int32, y.shape, 0)
            _store_slabs(y_ref, jnp.where((rows >= lo) & (rows < hi), y, _load_slabs(y_ref, bm)))


def _experts(xrows, items, wg, wu, wd):
    bm = EXPERT_BLOCK
    D = D_MODEL
    n_items = items[0].shape[0]
    block = pl.BlockSpec((bm * SLAB_ROWS, LANES), lambda w, ie, ib, lo, hi: (ib[w], 0))
    return pl.pallas_call(
        _expert_kernel,
        out_shape=jax.ShapeDtypeStruct(xrows.shape, F32),
        grid_spec=pltpu.PrefetchScalarGridSpec(
            num_scalar_prefetch=4,
            grid=(n_items,),
            in_specs=[block,
                      pl.BlockSpec((1, D, D_EXPERT), lambda w, ie, ib, lo, hi: (ie[w], 0, 0)),
                      pl.BlockSpec((1, D, D_EXPERT), lambda w, ie, ib, lo, hi: (ie[w], 0, 0)),
                      pl.BlockSpec((1, D_EXPERT, D), lambda w, ie, ib, lo, hi: (ie[w], 0, 0))],
            out_specs=block,
        ),
        compiler_params=_params("arbitrary"),
    )(*items, xrows, wg, wu, wd)


def _work_items(counts, n_rows):
    bm = EXPERT_BLOCK
    nb = n_rows // bm
    n_items = nb + N_EXPERTS
    ends = jnp.cumsum(counts)
    starts = ends - counts
    first_blk = starts // bm
    last_blk = (ends - 1) // bm
    per_e = jnp.where(counts > 0, last_blk - first_blk + 1, 0)
    item_end = jnp.cumsum(per_e)
    total = item_end[-1]
    w = jnp.arange(n_items, dtype=jnp.int32)
    wc = jnp.minimum(w, total - 1)
    e = jnp.minimum(jnp.sum((item_end[None, :] <= wc[:, None]).astype(jnp.int32), axis=1), N_EXPERTS - 1)
    b = first_blk[e] + (wc - (item_end[e] - per_e[e]))
    lo = jnp.maximum(starts[e], b * bm) - b * bm
    hi = jnp.minimum(ends[e], (b + 1) * bm) - b * bm
    live = w < total
    lo = jnp.where(live, lo, 0)
    hi = jnp.where(live, hi, 0)
    return (e.astype(jnp.int32), b.astype(jnp.int32), lo.astype(jnp.int32), hi.astype(jnp.int32)), starts


def _final_kernel(dest_ref, yrows_hbm, x1_ref, route_ref, p_ref, lnp_ref, wpg_ref, wpp_ref, lnf_ref,
                  o_ref, rows_ref, row_sems):
    i = pl.program_id(0)
    n = pl.num_programs(0)
    tm = x1_ref.shape[0]

    def row_copy(tile, j, slot):
        par = tile % 2
        return pltpu.make_async_copy(_slab(yrows_hbm, dest_ref[2 * (tile * tm + j) + slot]),
                                     _slab(rows_ref.at[par, slot], j), row_sems.at[par])

    def issue_tile(tile):
        def body(j, carry):
            row_copy(tile, j, 0).start()
            row_copy(tile, j, 1).start()
            return carry
        lax.fori_loop(0, tm, body, 0, unroll=8)

    @pl.when(i == 0)
    def _():
        issue_tile(i)

    @pl.when(i + 1 < n)
    def _():
        issue_tile(i + 1)

    pp = jnp.dot(p_ref[...].astype(BF16), wpp_ref[...], preferred_element_type=F32)

    def drain(j, carry):
        row_copy(i, j, 0).wait()
        row_copy(i, j, 1).wait()
        return carry

    lax.fori_loop(0, tm, drain, 0, unroll=8)

    route = route_ref[...]
    par = i % 2
    x2 = (x1_ref[...] + route[:, 4:5] * _load_slabs(rows_ref.at[par, 0], tm)
          + route[:, 5:6] * _load_slabs(rows_ref.at[par, 1], tm))
    gate = _sigmoid(jnp.dot(_rms(x2, lnp_ref[...]).astype(BF16), wpg_ref[...], preferred_element_type=F32))
    x3 = x2 + gate * pp
    o_ref[...] = _rms(x3, lnf_ref[...])


def _final(dest, yrows, x1, route, p2, ln_ple, wpg, wpp, ln_final):
    T, D = x1.shape
    tm = GATHER_TILE
    row = lambda n: pl.BlockSpec((tm, n), lambda i, d: (i, 0))
    full = lambda a: pl.BlockSpec(a.shape, lambda i, d: (0,) * a.ndim)
    ln_ple = ln_ple.reshape(1, D)
    ln_final = ln_final.reshape(1, D)
    return pl.pallas_call(
        _final_kernel,
        out_shape=jax.ShapeDtypeStruct((T, D), F32),
        grid_spec=pltpu.PrefetchScalarGridSpec(
            num_scalar_prefetch=1,
            grid=(T // tm,),
            in_specs=[pl.BlockSpec(memory_space=pl.ANY),
                      row(D), row(LANES), row(PLE_DIM), full(ln_ple), full(wpg), full(wpp), full(ln_final)],
            out_specs=row(D),
            scratch_shapes=[pltpu.VMEM((2, 2, tm * SLAB_ROWS, LANES), F32), pltpu.SemaphoreType.DMA((2,))],
        ),
        compiler_params=_params("arbitrary"),
    )(dest, yrows, x1, route, p2, ln_ple, wpg, wpp, ln_final)


def kernel(x, p, positions, ln_mix, w_in, mu_shift, w0, w_decay_up, a0, w_aaa_up, w_gate_up, k_k, k_a, r_k, ln_x_w, ln_x_b, sinks, w_branch_att, w_branch_rwkv, w_out, ln_moe, w_group, b_group, w_expert, b_expert, w_gate_e, w_up_e, w_down_e, ln_ple, w_ple_gate, w_ple_proj, ln_final):
    B, S, D = x.shape
    T = B * S
    depth = w_in.shape[0]
    assert D == D_MODEL and S % ROW_TILE == 0 and S % WINDOW == 0 and S % (RWKV_CHUNK * RWKV_CHUNKS_PER_STEP) == 0
    assert T % GATHER_TILE == 0 and (2 * T) % EXPERT_BLOCK == 0
    cos, sin = _rope_tables(positions)
    x2 = x.reshape(T, D)
    out = None
    for i in range(depth):
        q, k, v, zr, gates = _inproj(x2, ln_mix[i], w_in[i].astype(BF16), mu_shift[i], cos, sin, S)
        y_att = _attention(q, k, v, sinks[i], B, S)
        y_rwkv = _rwkv(zr, w0[i], w_decay_up[i], a0[i], w_aaa_up[i], w_gate_up[i], k_k[i], k_a[i], r_k[i],
                       ln_x_w[i], ln_x_b[i], B, S)
        pad = LANES - N_GROUPS - N_EXPERTS
        w_router = jnp.concatenate([w_group[i], w_expert[i], jnp.zeros((D, pad), F32)], axis=1)
        w_router_hi = w_router.astype(BF16)
        w_router = jnp.concatenate([w_router_hi, (w_router - w_router_hi.astype(F32)).astype(BF16)], axis=1)
        b_router = jnp.concatenate([b_group[i], b_expert[i], jnp.zeros((pad,), F32)]).reshape(1, LANES)
        x1, h2, route, cnt = _merge(x2, y_att, y_rwkv, gates, w_branch_att[i].astype(BF16),
                                    w_branch_rwkv[i].astype(BF16), w_out[i].astype(BF16), ln_moe[i],
                                    w_router, b_router)
        counts = cnt[0, :N_EXPERTS].astype(jnp.int32)
        items, starts = _work_items(counts, 2 * T)
        dest = (starts[route[:, 0:2].astype(jnp.int32)] + route[:, 2:4].astype(jnp.int32)).reshape(-1)
        xrows = _dispatch(h2, dest)
        yrows = _experts(xrows, items, w_gate_e[i].astype(BF16), w_up_e[i].astype(BF16),
                         w_down_e[i].astype(BF16))
        last = i == depth - 1
        assert last, "the final-norm kernel closes the only layer"
        out = _final(dest, yrows, x1, route, p[i].reshape(T, PLE_DIM), ln_ple[i], w_ple_gate[i].astype(BF16),
                     w_ple_proj[i].astype(BF16), ln_final)
    return out.reshape(B, S, D)
```

```python
import functools

import jax
import jax.numpy as jnp
from jax import lax
from jax.experimental import pallas as pl
from jax.experimental.pallas import tpu as pltpu

F32 = jnp.float32
BF16 = jnp.bfloat16
HIGHEST = lax.Precision.HIGHEST

D_MODEL = 1024
PLE_DIM = 256
ATT_HEADS = 8
ATT_KV_HEADS = 2
HEAD_DIM = 64
ATT_WIDTH = ATT_HEADS * HEAD_DIM
KV_WIDTH = ATT_KV_HEADS * HEAD_DIM
WINDOW = 128
ROPE_THETA = 10000.0
RWKV_HEADS = 8
RWKV_HEAD = 64
RWKV_WIDTH = RWKV_HEADS * RWKV_HEAD
DECAY_LORA = 64
AAA_LORA = 64
GATE_LORA = 128
RWKV_GN_EPS = 64e-5
ATT_COLS = ATT_WIDTH + 2 * KV_WIDTH
SHIFT_COLS = 3 * RWKV_WIDTH + DECAY_LORA + AAA_LORA + GATE_LORA
GATE_COLS = 2 * D_MODEL
N_GROUPS = 4
EXPERTS_PER_GROUP = 8
N_EXPERTS = N_GROUPS * EXPERTS_PER_GROUP
D_EXPERT = 512
NORM_EPS = 1e-6
NEG_INF = -1e30

LANES = 128
SLAB_ROWS = D_MODEL // LANES
VMEM_LIMIT = 56 * 1024 * 1024

ROW_TILE = 256
ATT_BLOCKS_PER_STEP = 4
RWKV_CHUNK = 64
RWKV_CHUNKS_PER_STEP = 4
EXPERT_BLOCK = 512
GATHER_TILE = 256
DISPATCH_TILE = 512


def _params(*sem):
    return pltpu.CompilerParams(dimension_semantics=sem, vmem_limit_bytes=VMEM_LIMIT)


def _bdot(a, b):
    return jnp.dot(a.astype(BF16), b.astype(BF16), preferred_element_type=F32)


def _bdot_nt(a, b):
    return lax.dot_general(a.astype(BF16), b.astype(BF16), (((1,), (1,)), ((), ())),
                           preferred_element_type=F32)


def _fdot(a, b):
    return jnp.dot(a, b, preferred_element_type=F32, precision=HIGHEST)


def _rms(x, g):
    return x * lax.rsqrt(jnp.mean(x * x, axis=-1, keepdims=True) + NORM_EPS) * g


def _sigmoid(x):
    return 1.0 / (1.0 + jnp.exp(-x))


def _store_slabs(ref, val):
    m = val.shape[0]
    for j in range(SLAB_ROWS):
        ref[pl.ds(j, m, stride=SLAB_ROWS), :] = val[:, j * LANES:(j + 1) * LANES]


def _load_slabs(ref, m):
    return jnp.concatenate([ref[pl.ds(j, m, stride=SLAB_ROWS), :] for j in range(SLAB_ROWS)], axis=1)


def _slab(ref, index):
    return ref.at[pl.ds(pl.multiple_of(index * SLAB_ROWS, SLAB_ROWS), SLAB_ROWS)]


def _rope_table_kernel(pos_ref, invf_ref, cos_ref, sin_ref):
    ang = pos_ref[...].astype(F32) * invf_ref[...]
    cos_ref[...] = jnp.cos(ang)
    sin_ref[...] = jnp.sin(ang)


def _rope_tables(positions):
    T = positions.size
    half = HEAD_DIM // 2
    per_row = LANES // half
    inv_freq = ROPE_THETA ** (-jnp.arange(half, dtype=F32) / half)
    invf = jnp.tile(inv_freq, per_row).reshape(1, LANES)
    pos = jnp.repeat(positions.reshape(T // per_row, per_row), half, axis=1)
    rows = T // per_row
    tr = min(rows, 2048)
    cos, sin = pl.pallas_call(
        _rope_table_kernel,
        out_shape=(jax.ShapeDtypeStruct((rows, LANES), F32),) * 2,
        grid=(rows // tr,),
        in_specs=[pl.BlockSpec((tr, LANES), lambda i: (i, 0)),
                  pl.BlockSpec((1, LANES), lambda i: (0, 0))],
        out_specs=(pl.BlockSpec((tr, LANES), lambda i: (i, 0)),) * 2,
        compiler_params=_params("arbitrary"),
    )(pos, invf)
    cos = cos.reshape(T, half)
    sin = sin.reshape(T, half)
    return jnp.tile(cos, (1, 4)), jnp.tile(jnp.concatenate([-sin, sin], axis=1), (1, 2))


def _rope(t, cos, sin):
    n = t.shape[1]
    reps = n // LANES
    c = jnp.tile(cos, (1, reps)) if reps > 1 else cos
    s = jnp.tile(sin, (1, reps)) if reps > 1 else sin
    lane = lax.broadcasted_iota(jnp.int32, t.shape, 1)
    first_half = (lane % HEAD_DIM) < (HEAD_DIM // 2)
    partner = jnp.where(first_half, pltpu.roll(t, n - HEAD_DIM // 2, 1), pltpu.roll(t, HEAD_DIM // 2, 1))
    return t * c + partner * s


def _inproj_kernel(x_ref, g_ref, w_ref, mu_ref, cos_ref, sin_ref,
                   q_ref, k_ref, v_ref, zr_ref, gate_ref, carry_ref, *, tiles_per_seq):
    i = pl.program_id(0)
    tm = x_ref.shape[0]
    h = _rms(x_ref[...], g_ref[...]).astype(BF16)
    cos = cos_ref[...]
    sin = sin_ref[...]

    za = jnp.dot(h, w_ref[:, :ATT_COLS], preferred_element_type=F32)
    q_ref[...] = (_rope(za[:, :ATT_WIDTH], cos, sin) * (HEAD_DIM ** -0.5)).astype(BF16)
    k_ref[...] = _rope(za[:, ATT_WIDTH:ATT_WIDTH + KV_WIDTH], cos, sin).astype(BF16)
    v_ref[...] = za[:, ATT_WIDTH + KV_WIDTH:].astype(BF16)

    zs = jnp.dot(h, w_ref[:, ATT_COLS:ATT_COLS + SHIFT_COLS], preferred_element_type=F32)
    row = lax.broadcasted_iota(jnp.int32, zs.shape, 0)
    seq_start = (i % tiles_per_seq) == 0
    before = jnp.where(seq_start, 0.0, carry_ref[0:1, :])
    prev = jnp.where(row == 0, before, pltpu.roll(zs, 1, 0))
    carry_ref[0:1, :] = zs[tm - 1:tm, :]
    zr_ref[...] = zs + (prev - zs) * mu_ref[...]

    zg = jnp.dot(h, w_ref[:, ATT_COLS + SHIFT_COLS:], preferred_element_type=F32)
    gate_ref[...] = _sigmoid(zg).astype(BF16)


def _inproj(x2, ln, w_in, mu, cos, sin, seq):
    T, D = x2.shape
    tm = ROW_TILE
    in_width = w_in.shape[1]
    row = lambda n: pl.BlockSpec((tm, n), lambda i: (i, 0))
    full = lambda a: pl.BlockSpec(a.shape, lambda i: (0,) * a.ndim)
    ln = ln.reshape(1, D)
    mu = mu.reshape(1, SHIFT_COLS)
    return pl.pallas_call(
        functools.partial(_inproj_kernel, tiles_per_seq=seq // tm),
        out_shape=(jax.ShapeDtypeStruct((T, ATT_WIDTH), BF16),
                   jax.ShapeDtypeStruct((T, KV_WIDTH), BF16),
                   jax.ShapeDtypeStruct((T, KV_WIDTH), BF16),
                   jax.ShapeDtypeStruct((T, SHIFT_COLS), F32),
                   jax.ShapeDtypeStruct((T, GATE_COLS), BF16)),
        grid=(T // tm,),
        in_specs=[row(D), full(ln), full(w_in), full(mu), row(LANES), row(LANES)],
        out_specs=(row(ATT_WIDTH), row(KV_WIDTH), row(KV_WIDTH), row(SHIFT_COLS), row(GATE_COLS)),
        scratch_shapes=[pltpu.VMEM((8, SHIFT_COLS), F32)],
        compiler_params=_params("arbitrary"),
    )(x2, ln, w_in, mu, cos, sin)


def _attn_kernel(sink_ref, q_ref, kp_ref, kc_ref, vp_ref, vc_ref, o_ref):
    n = pl.program_id(1)
    bq = WINDOW
    kall = jnp.concatenate([kp_ref[...], kc_ref[...]], axis=0)
    vall = jnp.concatenate([vp_ref[...], vc_ref[...]], axis=0)
    qi = lax.broadcasted_iota(jnp.int32, (bq, 2 * bq), 0)
    si = lax.broadcasted_iota(jnp.int32, (bq, 2 * bq), 1)
    diff = qi + bq - si
    band = (diff >= 0) & (diff < WINDOW)
    grp = ATT_HEADS // ATT_KV_HEADS
    for blk in range(q_ref.shape[0] // bq):
        valid = band & ((si >= bq) | (n > 0)) if blk == 0 else band
        q = q_ref[blk * bq:(blk + 1) * bq, :]
        kcat = kall[blk * bq:(blk + 2) * bq]
        vcat = vall[blk * bq:(blk + 2) * bq]
        outs = []
        for hd in range(ATT_HEADS):
            j = hd // grp
            qh = q[:, hd * HEAD_DIM:(hd + 1) * HEAD_DIM]
            kj = kcat[:, j * HEAD_DIM:(j + 1) * HEAD_DIM]
            vj = vcat[:, j * HEAD_DIM:(j + 1) * HEAD_DIM]
            s = lax.dot_general(qh, kj, (((1,), (1,)), ((), ())), preferred_element_type=F32)
            s = jnp.where(valid, s, NEG_INF)
            sink = sink_ref[hd]
            m = jnp.maximum(jnp.max(s, axis=-1, keepdims=True), sink)
            e = jnp.exp(s - m)
            denom = jnp.sum(e, axis=-1, keepdims=True) + jnp.exp(sink - m)
            pv = jnp.dot(e.astype(BF16), vj, preferred_element_type=F32)
            outs.append(pv / denom)
        o_ref[blk * bq:(blk + 1) * bq, :] = jnp.concatenate(outs, axis=1).astype(o_ref.dtype)


def _attention(q, k, v, sinks, batch, seq):
    T = q.shape[0]
    bq = WINDOW
    per_step = ATT_BLOCKS_PER_STEP
    ns = seq // (bq * per_step)
    cur = lambda b, n: (b * ns + n, 0)
    prev = lambda b, n: (jnp.maximum((b * ns + n) * per_step - 1, 0), 0)
    return pl.pallas_call(
        _attn_kernel,
        out_shape=jax.ShapeDtypeStruct((T, ATT_WIDTH), BF16),
        grid=(batch, ns),
        in_specs=[pl.BlockSpec(memory_space=pltpu.SMEM),
                  pl.BlockSpec((bq * per_step, ATT_WIDTH), cur),
                  pl.BlockSpec((bq, KV_WIDTH), prev), pl.BlockSpec((bq * per_step, KV_WIDTH), cur),
                  pl.BlockSpec((bq, KV_WIDTH), prev), pl.BlockSpec((bq * per_step, KV_WIDTH), cur)],
        out_specs=pl.BlockSpec((bq * per_step, ATT_WIDTH), cur),
        compiler_params=_params("arbitrary", "arbitrary"),
    )(sinks, q, k, k, v, v)


def _rwkv_kernel(z_ref, w0_ref, wd_ref, a0_ref, wa_ref, wg_ref, kk_ref, ka_ref, rk_ref, lnw_ref, lnb_ref,
                 y_ref, s_ref):
    c = pl.program_id(1)

    @pl.when(c == 0)
    def _():
        s_ref[...] = jnp.zeros_like(s_ref)

    L = RWKV_CHUNK
    rows = z_ref.shape[0]
    n_chunks = rows // L
    C = RWKV_WIDTH
    N = RWKV_HEAD
    r = z_ref[:, 0:C]
    k = z_ref[:, C:2 * C]
    v = z_ref[:, 2 * C:3 * C]
    xw = z_ref[:, 3 * C:3 * C + DECAY_LORA]
    xa = z_ref[:, 3 * C + DECAY_LORA:3 * C + DECAY_LORA + AAA_LORA]
    xg = z_ref[:, 3 * C + DECAY_LORA + AAA_LORA:]

    wlin = w0_ref[...] + _bdot(jnp.tanh(xw), wd_ref[...])
    softplus = jnp.maximum(-wlin, 0.0) + jnp.log(1.0 + jnp.exp(-jnp.abs(wlin)))
    logdecay = -jnp.exp(-softplus - 0.5)
    a = _sigmoid(a0_ref[...] + _bdot(xa, wa_ref[...]))
    g = _bdot(_sigmoid(xg), wg_ref[...])
    kk = k * kk_ref[...]
    k2 = k * (1.0 + (a - 1.0) * ka_ref[...])

    hr = lax.broadcasted_iota(jnp.int32, (C, C), 0) // N
    hc = lax.broadcasted_iota(jnp.int32, (C, C), 1) // N
    head_ones = jnp.where(hr == hc, 1.0, 0.0).astype(BF16)
    head_sum = lambda t: jnp.dot(t.astype(BF16), head_ones, preferred_element_type=F32)

    kkn = kk / jnp.maximum(jnp.sqrt(head_sum(kk * kk)), 1e-12)

    assert L == N and 2 * N == LANES, "the pair-packed block-diagonal products need chunk == head size == 64"
    row = lax.broadcasted_iota(jnp.int32, (L, L), 0)
    col = lax.broadcasted_iota(jnp.int32, (L, L), 1)
    tri = jnp.where(row >= col, 1.0, 0.0).astype(BF16)
    ld_1 = logdecay.astype(BF16)
    rest = logdecay - ld_1.astype(F32)
    ld_2 = rest.astype(BF16)
    ld_3 = (rest - ld_2.astype(F32)).astype(BF16)
    cums = []
    for ci in range(n_chunks):
        rs = slice(ci * L, (ci + 1) * L)
        cums.append(jnp.dot(tri, ld_1[rs], preferred_element_type=F32)
                    + jnp.dot(tri, ld_2[rs], preferred_element_type=F32)
                    + jnp.dot(tri, ld_3[rs], preferred_element_type=F32))
    cum = jnp.concatenate(cums, axis=0) if n_chunks > 1 else cums[0]
    last = [cums[ci][L - 1:L, :] for ci in range(n_chunks)]
    cum_last = jnp.concatenate([jnp.broadcast_to(t, (L, C)) for t in last], axis=0) if n_chunks > 1 \
        else jnp.broadcast_to(last[0], (L, C))
    p_in = jnp.exp(cum)
    p_inv = jnp.exp(-cum)
    p_rem = jnp.exp(cum_last - cum)
    b = kkn * a
    at_f = (-kkn * jnp.exp(cum - logdecay)).astype(BF16)
    rt_f = r * p_in
    rt_b = rt_f.astype(BF16)
    bt_b = (b * p_inv).astype(BF16)
    kt_b = (k2 * p_inv).astype(BF16)
    br_b = (b * p_rem).astype(BF16)
    kr_b = (k2 * p_rem).astype(BF16)
    v_b = v.astype(BF16)

    dot = lambda x, y: jnp.dot(x.astype(BF16), y.astype(BF16), preferred_element_type=F32)
    dot_nt = lambda x, y: lax.dot_general(x.astype(BF16), y.astype(BF16), (((1,), (1,)), ((), ())),
                                          preferred_element_type=F32)
    dot_tn = lambda x, y: lax.dot_general(x.astype(BF16), y.astype(BF16), (((0,), (0,)), ((), ())),
                                          preferred_element_type=F32)

    n_pairs = C // LANES
    lane = lax.broadcasted_iota(jnp.int32, (L, LANES), 1)
    prow = lax.broadcasted_iota(jnp.int32, (L, LANES), 0)
    first = lane < N
    pcol = jnp.where(first, lane, lane - N)
    p_lower = prow > pcol
    p_lower_eq = prow >= pcol
    p_eye = jnp.where(prow == pcol, 1.0, 0.0)
    zero = jnp.zeros((), BF16)

    def bdiag(t):
        t = t.astype(BF16)
        return jnp.concatenate([jnp.where(first, t, zero), jnp.where(first, zero, t)], axis=0)

    pieces = [(ci, pj) for ci in range(n_chunks) for pj in range(n_pairs)]
    cut = lambda t, ci, pj: t[ci * L:(ci + 1) * L, pj * LANES:(pj + 1) * LANES]
    at = {p: cut(at_f, *p) for p in pieces}
    vv = {p: cut(v_b, *p) for p in pieces}
    m = {p: dot_nt(jnp.concatenate([at[p], cut(rt_b, *p)], axis=0),
                   jnp.concatenate([bdiag(cut(bt_b, *p)), bdiag(cut(kt_b, *p))], axis=0)) for p in pieces}
    m_ab = {p: jnp.where(p_lower, m[p][:L, :LANES], 0.0) for p in pieces}
    m_ak = {p: jnp.where(p_lower, m[p][:L, LANES:], 0.0).astype(BF16) for p in pieces}
    m_rb = {p: jnp.where(p_lower_eq, m[p][L:, :LANES], 0.0).astype(BF16) for p in pieces}
    m_rk = {p: jnp.where(p_lower_eq, m[p][L:, LANES:], 0.0).astype(BF16) for p in pieces}
    inv = {p: p_eye + m_ab[p] for p in pieces}
    pw = {p: m_ab[p].astype(BF16) for p in pieces}
    pw = {p: dot(pw[p], bdiag(pw[p])).astype(BF16) for p in pieces}
    for _ in range(max(2, (L - 1).bit_length()) - 2):
        both = {p: dot(jnp.concatenate([pw[p], inv[p].astype(BF16)], axis=0), bdiag(pw[p])) for p in pieces}
        pw = {p: both[p][:L].astype(BF16) for p in pieces}
        inv = {p: inv[p] + both[p][L:] for p in pieces}
    inv = {p: (inv[p] + dot(inv[p], bdiag(pw[p]))).astype(BF16) for p in pieces}
    mv = {p: dot(jnp.concatenate([m_ak[p], m_rk[p]], axis=0), bdiag(vv[p])) for p in pieces}
    wu = {p: dot(inv[p], jnp.concatenate([bdiag(at[p]), bdiag(mv[p][:L])], axis=1)) for p in pieces}
    w = {p: wu[p][:, :LANES].astype(BF16) for p in pieces}
    u0 = {p: wu[p][:, LANES:].astype(BF16) for p in pieces}
    ry = {p: dot(m_rb[p], jnp.concatenate([bdiag(w[p]), bdiag(u0[p])], axis=1)) for p in pieces}
    rq = {p: (cut(rt_f, *p) + ry[p][:, :LANES]).astype(BF16) for p in pieces}
    y0 = {p: ry[p][:, LANES:] + mv[p][L:] for p in pieces}
    same_head = (lax.broadcasted_iota(jnp.int32, (LANES, LANES), 0) < N) == \
                (lax.broadcasted_iota(jnp.int32, (LANES, LANES), 1) < N)
    gg = {p: jnp.where(same_head, dot_tn(w[p], cut(br_b, *p)), 0.0).astype(BF16) for p in pieces}
    hh_full = {p: dot_tn(jnp.concatenate([u0[p], vv[p]], axis=0),
                         jnp.concatenate([cut(br_b, *p), cut(kr_b, *p)], axis=0)) for p in pieces}
    hh = {p: jnp.where(first, hh_full[p][:N], hh_full[p][N:]) for p in pieces}

    ys = {}
    for pj in range(n_pairs):
        state = s_ref[pj]
        for ci in range(n_chunks):
            p = (ci, pj)
            sb = state.astype(BF16)
            ys[p] = y0[p] + dot_nt(rq[p], bdiag(sb))
            p_tot = p_in[(ci + 1) * L - 1:(ci + 1) * L, pj * LANES:(pj + 1) * LANES]
            state = state * p_tot + dot(sb, gg[p]) + hh[p]
        s_ref[pj] = state
    y = jnp.concatenate([jnp.concatenate([ys[(ci, pj)] for pj in range(n_pairs)], axis=1)
                         for ci in range(n_chunks)], axis=0)

    mu = head_sum(y) * (1.0 / N)
    dev = y - mu
    var = head_sum(dev * dev) * (1.0 / N)
    yn = dev * lax.rsqrt(var + RWKV_GN_EPS)
    bonus = head_sum(r * k2 * rk_ref[...]) * v
    y_ref[...] = ((yn * lnw_ref[...] + lnb_ref[...] + bonus) * g).astype(y_ref.dtype)


def _rwkv(zr, w0, wd, a0, wa, wg, k_k, k_a, r_k, ln_w, ln_b, batch, seq):
    T = zr.shape[0]
    L = RWKV_CHUNK * RWKV_CHUNKS_PER_STEP
    nc = seq // L
    vec = lambda a: a.reshape(1, RWKV_WIDTH)
    full = lambda a: pl.BlockSpec(a.shape, lambda b, c: (0,) * a.ndim)
    args = (vec(w0), wd, vec(a0), wa, wg, vec(k_k), vec(k_a), vec(r_k), vec(ln_w), vec(ln_b))
    return pl.pallas_call(
        _rwkv_kernel,
        out_shape=jax.ShapeDtypeStruct((T, RWKV_WIDTH), BF16),
        grid=(batch, nc),
        in_specs=[pl.BlockSpec((L, SHIFT_COLS), lambda b, c: (b * nc + c, 0))] + [full(a) for a in args],
        out_specs=pl.BlockSpec((L, RWKV_WIDTH), lambda b, c: (b * nc + c, 0)),
        scratch_shapes=[pltpu.VMEM((RWKV_WIDTH // LANES, RWKV_HEAD, LANES), F32)],
        compiler_params=_params("arbitrary", "arbitrary"),
    )(zr, *args)


def _merge_kernel(x_ref, ya_ref, yr_ref, gate_ref, wba_ref, wbr_ref, wo_ref, lnm_ref, wr_ref, br_ref,
                  x1_ref, h2_ref, route_ref, cnt_ref, cnt_scr):
    i = pl.program_id(0)

    @pl.when(i == 0)
    def _():
        cnt_scr[...] = jnp.zeros_like(cnt_scr)

    tm = x_ref.shape[0]
    D = D_MODEL
    gates = gate_ref[...].astype(F32)
    ya = jnp.dot(ya_ref[...], wba_ref[...], preferred_element_type=F32)
    yr = jnp.dot(yr_ref[...], wbr_ref[...], preferred_element_type=F32)
    merged = gates[:, :D] * ya + gates[:, D:] * yr
    x1 = x_ref[...] + jnp.dot(merged.astype(BF16), wo_ref[...], preferred_element_type=F32)
    x1_ref[...] = x1
    h2 = _rms(x1, lnm_ref[...])
    _store_slabs(h2_ref, h2)

    h_hi = h2.astype(BF16)
    h_lo = (h2 - h_hi.astype(F32)).astype(BF16)
    parts = (jnp.dot(h_hi, wr_ref[...], preferred_element_type=F32)
             + jnp.dot(h_lo, wr_ref[...], preferred_element_type=F32))
    logits = parts[:, :LANES] + parts[:, LANES:] + br_ref[...]
    lane = lax.broadcasted_iota(jnp.int32, logits.shape, 1)
    big = jnp.int32(1 << 20)
    gl = jnp.where(lane < N_GROUPS, logits, NEG_INF)
    gmax = jnp.max(gl, axis=-1, keepdims=True)
    gidx = jnp.min(jnp.where(gl == gmax, lane, big), axis=-1, keepdims=True)
    gsum = jnp.sum(jnp.where(lane < N_GROUPS, jnp.exp(logits - gmax), 0.0), axis=-1, keepdims=True)
    g_w = 1.0 / gsum
    lo = N_GROUPS + EXPERTS_PER_GROUP * gidx
    el = jnp.where((lane >= lo) & (lane < lo + EXPERTS_PER_GROUP), logits, NEG_INF)
    m1 = jnp.max(el, axis=-1, keepdims=True)
    i1 = jnp.min(jnp.where(el == m1, lane, big), axis=-1, keepdims=True)
    el2 = jnp.where(lane == i1, NEG_INF, el)
    m2 = jnp.max(el2, axis=-1, keepdims=True)
    i2 = jnp.min(jnp.where(el2 == m2, lane, big), axis=-1, keepdims=True)
    d = jnp.exp(m2 - m1)
    w1 = g_w / (1.0 + d)
    w2 = g_w * d / (1.0 + d)
    e1 = i1 - N_GROUPS
    e2 = i2 - N_GROUPS

    hit1 = lane == e1
    hit2 = lane == e2
    onehot = jnp.where(hit1 | hit2, 1.0, 0.0).astype(BF16)
    r_i = lax.broadcasted_iota(jnp.int32, (tm, tm), 0)
    c_i = lax.broadcasted_iota(jnp.int32, (tm, tm), 1)
    before = jnp.dot(jnp.where(r_i > c_i, 1.0, 0.0).astype(BF16), onehot, preferred_element_type=F32)
    before = before + cnt_scr[...]
    rank1 = jnp.sum(jnp.where(hit1, before, 0.0), axis=-1, keepdims=True)
    rank2 = jnp.sum(jnp.where(hit2, before, 0.0), axis=-1, keepdims=True)
    cnt_scr[...] = cnt_scr[...] + jnp.sum(onehot.astype(F32), axis=0, keepdims=True)
    cnt_ref[...] = cnt_scr[...]

    route = jnp.where(lane == 0, e1.astype(F32), 0.0)
    route = jnp.where(lane == 1, e2.astype(F32), route)
    route = jnp.where(lane == 2, rank1, route)
    route = jnp.where(lane == 3, rank2, route)
    route = jnp.where(lane == 4, w1, route)
    route = jnp.where(lane == 5, w2, route)
    route_ref[...] = route


def _merge(x2, y_att, y_rwkv, gates, wba, wbr, wo, ln_moe, w_router, b_router):
    T, D = x2.shape
    tm = ROW_TILE
    row = lambda n: pl.BlockSpec((tm, n), lambda i: (i, 0))
    full = lambda a: pl.BlockSpec(a.shape, lambda i: (0,) * a.ndim)
    ln_moe = ln_moe.reshape(1, D)
    return pl.pallas_call(
        _merge_kernel,
        out_shape=(jax.ShapeDtypeStruct((T, D), F32), jax.ShapeDtypeStruct((T * SLAB_ROWS, LANES), F32),
                   jax.ShapeDtypeStruct((T, LANES), F32), jax.ShapeDtypeStruct((1, LANES), F32)),
        grid=(T // tm,),
        in_specs=[row(D), row(ATT_WIDTH), row(RWKV_WIDTH), row(GATE_COLS),
                  full(wba), full(wbr), full(wo), full(ln_moe), full(w_router), full(b_router)],
        out_specs=(row(D), pl.BlockSpec((tm * SLAB_ROWS, LANES), lambda i: (i, 0)), row(LANES),
                   pl.BlockSpec((1, LANES), lambda i: (0, 0))),
        scratch_shapes=[pltpu.VMEM((1, LANES), F32)],
        compiler_params=_params("arbitrary"),
    )(x2, y_att, y_rwkv, gates, wba, wbr, wo, ln_moe, w_router, b_router)


def _dispatch_kernel(dest_ref, h_ref, xrows_hbm, row_sem, *, tile):
    base = pl.program_id(0) * tile

    def row_copy(j, slot):
        return pltpu.make_async_copy(_slab(h_ref, j), _slab(xrows_hbm, dest_ref[2 * (base + j) + slot]), row_sem)

    def issue(j, carry):
        row_copy(j, 0).start(priority=0)
        row_copy(j, 1).start(priority=1)
        return carry

    lax.fori_loop(0, tile, issue, 0, unroll=8)

    def drain(j, carry):
        row_copy(j, 0).wait()
        row_copy(j, 1).wait()
        return carry

    lax.fori_loop(0, tile, drain, 0, unroll=8)


def _dispatch(h2_slabs, dest):
    T = h2_slabs.shape[0] // SLAB_ROWS
    tile = min(DISPATCH_TILE, T)
    assert T % tile == 0
    return pl.pallas_call(
        functools.partial(_dispatch_kernel, tile=tile),
        out_shape=jax.ShapeDtypeStruct((2 * T * SLAB_ROWS, LANES), F32),
        grid_spec=pltpu.PrefetchScalarGridSpec(
            num_scalar_prefetch=1,
            grid=(T // tile,),
            in_specs=[pl.BlockSpec((tile * SLAB_ROWS, LANES), lambda i, d: (i, 0))],
            out_specs=pl.BlockSpec(memory_space=pl.ANY),
            scratch_shapes=[pltpu.SemaphoreType.DMA],
        ),
        compiler_params=_params("arbitrary"),
    )(dest, h2_slabs)


def _expert_kernel(item_e, item_b, item_lo, item_hi, x_ref, wg_ref, wu_ref, wd_ref, y_ref):
    w = pl.program_id(0)
    lo = item_lo[w]
    hi = item_hi[w]

    @pl.when(hi > lo)
    def _():
        bm = x_ref.shape[0] // SLAB_ROWS
        xb = _load_slabs(x_ref, bm).astype(BF16)
        hg = jnp.dot(xb, wg_ref[0], preferred_element_type=F32)
        hu = jnp.dot(xb, wu_ref[0], preferred_element_type=F32)
        hid = hg * _sigmoid(hg) * hu
        y = jnp.dot(hid.astype(BF16), wd_ref[0], preferred_element_type=F32)

        @pl.when(lo == 0)
        def _():
            _store_slabs(y_ref, y)

        @pl.when(lo > 0)
        def _():
            rows = lax.broadcasted_iota(jnp.int32, y.shape, 0)
            _store_slabs(y_ref, jnp.where((rows >= lo) & (rows < hi), y, _load_slabs(y_ref, bm)))


def _experts(xrows, items, wg, wu, wd):
    bm = EXPERT_BLOCK
    D = D_MODEL
    n_items = items[0].shape[0]
    block = pl.BlockSpec((bm * SLAB_ROWS, LANES), lambda w, ie, ib, lo, hi: (ib[w], 0))
    return pl.pallas_call(
        _expert_kernel,
        out_shape=jax.ShapeDtypeStruct(xrows.shape, F32),
        grid_spec=pltpu.PrefetchScalarGridSpec(
            num_scalar_prefetch=4,
            grid=(n_items,),
            in_specs=[block,
                      pl.BlockSpec((1, D, D_EXPERT), lambda w, ie, ib, lo, hi: (ie[w], 0, 0)),
                      pl.BlockSpec((1, D, D_EXPERT), lambda w, ie, ib, lo, hi: (ie[w], 0, 0)),
                      pl.BlockSpec((1, D_EXPERT, D), lambda w, ie, ib, lo, hi: (ie[w], 0, 0))],
            out_specs=block,
        ),
        compiler_params=_params("arbitrary"),
    )(*items, xrows, wg, wu, wd)


def _work_items(counts, n_rows):
    bm = EXPERT_BLOCK
    nb = n_rows // bm
    n_items = nb + N_EXPERTS
    ends = jnp.cumsum(counts)
    starts = ends - counts
    first_blk = starts // bm
    last_blk = (ends - 1) // bm
    per_e = jnp.where(counts > 0, last_blk - first_blk + 1, 0)
    item_end = jnp.cumsum(per_e)
    total = item_end[-1]
    w = jnp.arange(n_items, dtype=jnp.int32)
    wc = jnp.minimum(w, total - 1)
    e = jnp.minimum(jnp.sum((item_end[None, :] <= wc[:, None]).astype(jnp.int32), axis=1), N_EXPERTS - 1)
    b = first_blk[e] + (wc - (item_end[e] - per_e[e]))
    lo = jnp.maximum(starts[e], b * bm) - b * bm
    hi = jnp.minimum(ends[e], (b + 1) * bm) - b * bm
    live = w < total
    lo = jnp.where(live, lo, 0)
    hi = jnp.where(live, hi, 0)
    return (e.astype(jnp.int32), b.astype(jnp.int32), lo.astype(jnp.int32), hi.astype(jnp.int32)), starts


def _final_kernel(dest_ref, yrows_hbm, x1_ref, route_ref, p_ref, lnp_ref, wpg_ref, wpp_ref, lnf_ref,
                  o_ref, rows_ref, row_sems):
    i = pl.program_id(0)
    n = pl.num_programs(0)
    tm = x1_ref.shape[0]

    def row_copy(tile, j, slot):
        par = tile % 2
        return pltpu.make_async_copy(_slab(yrows_hbm, dest_ref[2 * (tile * tm + j) + slot]),
                                     _slab(rows_ref.at[par, slot], j), row_sems.at[par])

    def issue_tile(tile):
        def body(j, carry):
            row_copy(tile, j, 0).start(priority=0)
            row_copy(tile, j, 1).start(priority=1)
            return carry
        lax.fori_loop(0, tm, body, 0, unroll=8)

    @pl.when(i == 0)
    def _():
        issue_tile(i)

    @pl.when(i + 1 < n)
    def _():
        issue_tile(i + 1)

    pp = jnp.dot(p_ref[...].astype(BF16), wpp_ref[...], preferred_element_type=F32)

    def drain(j, carry):
        row_copy(i, j, 0).wait()
        row_copy(i, j, 1).wait()
        return carry

    lax.fori_loop(0, tm, drain, 0, unroll=8)

    route = route_ref[...]
    par = i % 2
    x2 = (x1_ref[...] + route[:, 4:5] * _load_slabs(rows_ref.at[par, 0], tm)
          + route[:, 5:6] * _load_slabs(rows_ref.at[par, 1], tm))
    gate = _sigmoid(jnp.dot(_rms(x2, lnp_ref[...]).astype(BF16), wpg_ref[...], preferred_element_type=F32))
    x3 = x2 + gate * pp
    o_ref[...] = _rms(x3, lnf_ref[...])


def _final(dest, yrows, x1, route, p2, ln_ple, wpg, wpp, ln_final):
    T, D = x1.shape
    tm = GATHER_TILE
    row = lambda n: pl.BlockSpec((tm, n), lambda i, d: (i, 0))
    full = lambda a: pl.BlockSpec(a.shape, lambda i, d: (0,) * a.ndim)
    ln_ple = ln_ple.reshape(1, D)
    ln_final = ln_final.reshape(1, D)
    return pl.pallas_call(
        _final_kernel,
        out_shape=jax.ShapeDtypeStruct((T, D), F32),
        grid_spec=pltpu.PrefetchScalarGridSpec(
            num_scalar_prefetch=1,
            grid=(T // tm,),
            in_specs=[pl.BlockSpec(memory_space=pl.ANY),
                      row(D), row(LANES), row(PLE_DIM), full(ln_ple), full(wpg), full(wpp), full(ln_final)],
            out_specs=row(D),
            scratch_shapes=[pltpu.VMEM((2, 2, tm * SLAB_ROWS, LANES), F32), pltpu.SemaphoreType.DMA((2,))],
        ),
        compiler_params=_params("arbitrary"),
    )(dest, yrows, x1, route, p2, ln_ple, wpg, wpp, ln_final)


def kernel(x, p, positions, ln_mix, w_in, mu_shift, w0, w_decay_up, a0, w_aaa_up, w_gate_up, k_k, k_a, r_k, ln_x_w, ln_x_b, sinks, w_branch_att, w_branch_rwkv, w_out, ln_moe, w_group, b_group, w_expert, b_expert, w_gate_e, w_up_e, w_down_e, ln_ple, w_ple_gate, w_ple_proj, ln_final):
    B, S, D = x.shape
    T = B * S
    depth = w_in.shape[0]
    assert D == D_MODEL and S % ROW_TILE == 0 and S % (WINDOW * ATT_BLOCKS_PER_STEP) == 0 and S % (RWKV_CHUNK * RWKV_CHUNKS_PER_STEP) == 0
    assert T % GATHER_TILE == 0 and (2 * T) % EXPERT_BLOCK == 0
    cos, sin = _rope_tables(positions)
    x2 = x.reshape(T, D)
    out = None
    for i in range(depth):
        q, k, v, zr, gates = _inproj(x2, ln_mix[i], w_in[i].astype(BF16), mu_shift[i], cos, sin, S)
        y_att = _attention(q, k, v, sinks[i], B, S)
        y_rwkv = _rwkv(zr, w0[i], w_decay_up[i], a0[i], w_aaa_up[i], w_gate_up[i], k_k[i], k_a[i], r_k[i],
                       ln_x_w[i], ln_x_b[i], B, S)
        pad = LANES - N_GROUPS - N_EXPERTS
        w_router = jnp.concatenate([w_group[i], w_expert[i], jnp.zeros((D, pad), F32)], axis=1)
        w_router_hi = w_router.astype(BF16)
        w_router = jnp.concatenate([w_router_hi, (w_router - w_router_hi.astype(F32)).astype(BF16)], axis=1)
        b_router = jnp.concatenate([b_group[i], b_expert[i], jnp.zeros((pad,), F32)]).reshape(1, LANES)
        x1, h2, route, cnt = _merge(x2, y_att, y_rwkv, gates, w_branch_att[i].astype(BF16),
                                    w_branch_rwkv[i].astype(BF16), w_out[i].astype(BF16), ln_moe[i],
                                    w_router, b_router)
        counts = cnt[0, :N_EXPERTS].astype(jnp.int32)
        items, starts = _work_items(counts, 2 * T)
        dest = (starts[route[:, 0:2].astype(jnp.int32)] + route[:, 2:4].astype(jnp.int32)).reshape(-1)
        xrows = _dispatch(h2, dest)
        yrows = _experts(xrows, items, w_gate_e[i].astype(BF16), w_up_e[i].astype(BF16),
                         w_down_e[i].astype(BF16))
        last = i == depth - 1
        assert last, "the final-norm kernel closes the only layer"
        out = _final(dest, yrows, x1, route, p[i].reshape(T, PLE_DIM), ln_ple[i], w_ple_gate[i].astype(BF16),
                     w_ple_proj[i].astype(BF16), ln_final)
    return out.reshape(B, S, D)
```

```python
import functools

import jax
import jax.numpy as jnp
from jax import lax
from jax.experimental import pallas as pl
from jax.experimental.pallas import tpu as pltpu

F32 = jnp.float32
BF16 = jnp.bfloat16
HIGHEST = lax.Precision.HIGHEST

D_MODEL = 1024
PLE_DIM = 256
ATT_HEADS = 8
ATT_KV_HEADS = 2
HEAD_DIM = 64
ATT_WIDTH = ATT_HEADS * HEAD_DIM
KV_WIDTH = ATT_KV_HEADS * HEAD_DIM
WINDOW = 128
ROPE_THETA = 10000.0
RWKV_HEADS = 8
RWKV_HEAD = 64
RWKV_WIDTH = RWKV_HEADS * RWKV_HEAD
DECAY_LORA = 64
AAA_LORA = 64
GATE_LORA = 128
RWKV_GN_EPS = 64e-5
ATT_COLS = ATT_WIDTH + 2 * KV_WIDTH
SHIFT_COLS = 3 * RWKV_WIDTH + DECAY_LORA + AAA_LORA + GATE_LORA
GATE_COLS = 2 * D_MODEL
N_GROUPS = 4
EXPERTS_PER_GROUP = 8
N_EXPERTS = N_GROUPS * EXPERTS_PER_GROUP
D_EXPERT = 512
NORM_EPS = 1e-6
NEG_INF = -1e30

LANES = 128
SLAB_ROWS = D_MODEL // LANES
VMEM_LIMIT = 56 * 1024 * 1024

ROW_TILE = 256
ATT_BLOCKS_PER_STEP = 4
RWKV_CHUNK = 64
RWKV_CHUNKS_PER_STEP = 4
EXPERT_BLOCK = 512
GATHER_TILE = 256
DISPATCH_TILE = 512


def _params(*sem):
    return pltpu.CompilerParams(dimension_semantics=sem, vmem_limit_bytes=VMEM_LIMIT)


def _bdot(a, b):
    return jnp.dot(a.astype(BF16), b.astype(BF16), preferred_element_type=F32)


def _bdot_nt(a, b):
    return lax.dot_general(a.astype(BF16), b.astype(BF16), (((1,), (1,)), ((), ())),
                           preferred_element_type=F32)


def _fdot(a, b):
    return jnp.dot(a, b, preferred_element_type=F32, precision=HIGHEST)


def _rms(x, g):
    return x * lax.rsqrt(jnp.mean(x * x, axis=-1, keepdims=True) + NORM_EPS) * g


def _sigmoid(x):
    return 1.0 / (1.0 + jnp.exp(-x))


def _store_slabs(ref, val):
    m = val.shape[0]
    for j in range(SLAB_ROWS):
        ref[pl.ds(j, m, stride=SLAB_ROWS), :] = val[:, j * LANES:(j + 1) * LANES]


def _load_slabs(ref, m):
    return jnp.concatenate([ref[pl.ds(j, m, stride=SLAB_ROWS), :] for j in range(SLAB_ROWS)], axis=1)


def _slab(ref, index):
    return ref.at[pl.ds(pl.multiple_of(index * SLAB_ROWS, SLAB_ROWS), SLAB_ROWS)]


def _rope_table_kernel(pos_ref, invf_ref, cos_ref, sin_ref):
    half = HEAD_DIM // 2
    per_row = LANES // half
    rows = pos_ref.shape[0]
    ang = pos_ref[...].astype(F32) * invf_ref[...]
    group = lax.broadcasted_iota(jnp.int32, ang.shape, 1) // half
    sign = jnp.where(group % 2 == 0, -1.0, 1.0)
    for table, out_ref, scale in ((jnp.cos(ang), cos_ref, None), (jnp.sin(ang), sin_ref, sign)):
        rolled = [table] + [pltpu.roll(table, half * j, 1) for j in range(1, per_row)]
        for m in range(per_row):
            out = rolled[(per_row - 1 - m) % per_row]
            for g in range(per_row - 1):
                out = jnp.where(group == g, rolled[(g - m) % per_row], out)
            out_ref[pl.ds(m, rows, stride=per_row), :] = out if scale is None else out * scale


def _rope_tables(positions):
    T = positions.size
    half = HEAD_DIM // 2
    per_row = LANES // half
    inv_freq = ROPE_THETA ** (-jnp.arange(half, dtype=F32) / half)
    invf = jnp.tile(inv_freq, per_row).reshape(1, LANES)
    pos = jnp.repeat(positions.reshape(T // per_row, per_row), half, axis=1)
    rows = T // per_row
    tr = min(rows, 1024)
    return pl.pallas_call(
        _rope_table_kernel,
        out_shape=(jax.ShapeDtypeStruct((T, LANES), F32),) * 2,
        grid=(rows // tr,),
        in_specs=[pl.BlockSpec((tr, LANES), lambda i: (i, 0)),
                  pl.BlockSpec((1, LANES), lambda i: (0, 0))],
        out_specs=(pl.BlockSpec((tr * per_row, LANES), lambda i: (i, 0)),) * 2,
        compiler_params=_params("arbitrary"),
    )(pos, invf)


def _rope(t, cos, sin):
    n = t.shape[1]
    reps = n // LANES
    c = jnp.tile(cos, (1, reps)) if reps > 1 else cos
    s = jnp.tile(sin, (1, reps)) if reps > 1 else sin
    lane = lax.broadcasted_iota(jnp.int32, t.shape, 1)
    first_half = (lane % HEAD_DIM) < (HEAD_DIM // 2)
    partner = jnp.where(first_half, pltpu.roll(t, n - HEAD_DIM // 2, 1), pltpu.roll(t, HEAD_DIM // 2, 1))
    return t * c + partner * s


def _inproj_kernel(x_ref, g_ref, w_ref, mu_ref, cos_ref, sin_ref,
                   q_ref, k_ref, v_ref, zr_ref, gate_ref, carry_ref, *, tiles_per_seq):
    i = pl.program_id(0)
    tm = x_ref.shape[0]
    h = _rms(x_ref[...], g_ref[...]).astype(BF16)
    cos = cos_ref[...]
    sin = sin_ref[...]

    za = jnp.dot(h, w_ref[:, :ATT_COLS], preferred_element_type=F32)
    q_ref[...] = (_rope(za[:, :ATT_WIDTH], cos, sin) * (HEAD_DIM ** -0.5)).astype(BF16)
    k_ref[...] = _rope(za[:, ATT_WIDTH:ATT_WIDTH + KV_WIDTH], cos, sin).astype(BF16)
    v_ref[...] = za[:, ATT_WIDTH + KV_WIDTH:].astype(BF16)

    zs = jnp.dot(h, w_ref[:, ATT_COLS:ATT_COLS + SHIFT_COLS], preferred_element_type=F32)
    row = lax.broadcasted_iota(jnp.int32, zs.shape, 0)
    seq_start = (i % tiles_per_seq) == 0
    before = jnp.where(seq_start, 0.0, carry_ref[0:1, :])
    prev = jnp.where(row == 0, before, pltpu.roll(zs, 1, 0))
    carry_ref[0:1, :] = zs[tm - 1:tm, :]
    zr_ref[...] = zs + (prev - zs) * mu_ref[...]

    zg = jnp.dot(h, w_ref[:, ATT_COLS + SHIFT_COLS:], preferred_element_type=F32)
    gate_ref[...] = _sigmoid(zg).astype(BF16)


def _inproj(x2, ln, w_in, mu, cos, sin, seq):
    T, D = x2.shape
    tm = ROW_TILE
    in_width = w_in.shape[1]
    row = lambda n: pl.BlockSpec((tm, n), lambda i: (i, 0))
    full = lambda a: pl.BlockSpec(a.shape, lambda i: (0,) * a.ndim)
    ln = ln.reshape(1, D)
    mu = mu.reshape(1, SHIFT_COLS)
    return pl.pallas_call(
        functools.partial(_inproj_kernel, tiles_per_seq=seq // tm),
        out_shape=(jax.ShapeDtypeStruct((T, ATT_WIDTH), BF16),
                   jax.ShapeDtypeStruct((T, KV_WIDTH), BF16),
                   jax.ShapeDtypeStruct((T, KV_WIDTH), BF16),
                   jax.ShapeDtypeStruct((T, SHIFT_COLS), F32),
                   jax.ShapeDtypeStruct((T, GATE_COLS), BF16)),
        grid=(T // tm,),
        in_specs=[row(D), full(ln), full(w_in), full(mu), row(LANES), row(LANES)],
        out_specs=(row(ATT_WIDTH), row(KV_WIDTH), row(KV_WIDTH), row(SHIFT_COLS), row(GATE_COLS)),
        scratch_shapes=[pltpu.VMEM((8, SHIFT_COLS), F32)],
        compiler_params=_params("arbitrary"),
    )(x2, ln, w_in, mu, cos, sin)


def _attn_kernel(sink_ref, q_ref, kp_ref, kc_ref, vp_ref, vc_ref, o_ref):
    n = pl.program_id(1)
    bq = WINDOW
    kall = jnp.concatenate([kp_ref[...], kc_ref[...]], axis=0)
    vall = jnp.concatenate([vp_ref[...], vc_ref[...]], axis=0)
    qi = lax.broadcasted_iota(jnp.int32, (bq, 2 * bq), 0)
    si = lax.broadcasted_iota(jnp.int32, (bq, 2 * bq), 1)
    diff = qi + bq - si
    band = (diff >= 0) & (diff < WINDOW)
    grp = ATT_HEADS // ATT_KV_HEADS
    for blk in range(q_ref.shape[0] // bq):
        valid = band & ((si >= bq) | (n > 0)) if blk == 0 else band
        q = q_ref[blk * bq:(blk + 1) * bq, :]
        kcat = kall[blk * bq:(blk + 2) * bq]
        vcat = vall[blk * bq:(blk + 2) * bq]
        outs = []
        for hd in range(ATT_HEADS):
            j = hd // grp
            qh = q[:, hd * HEAD_DIM:(hd + 1) * HEAD_DIM]
            kj = kcat[:, j * HEAD_DIM:(j + 1) * HEAD_DIM]
            vj = vcat[:, j * HEAD_DIM:(j + 1) * HEAD_DIM]
            s = lax.dot_general(qh, kj, (((1,), (1,)), ((), ())), preferred_element_type=F32)
            s = jnp.where(valid, s, NEG_INF)
            sink = sink_ref[hd]
            m = jnp.maximum(jnp.max(s, axis=-1, keepdims=True), sink)
            e = jnp.exp(s - m)
            denom = jnp.sum(e, axis=-1, keepdims=True) + jnp.exp(sink - m)
            pv = jnp.dot(e.astype(BF16), vj, preferred_element_type=F32)
            outs.append(pv / denom)
        o_ref[blk * bq:(blk + 1) * bq, :] = jnp.concatenate(outs, axis=1).astype(o_ref.dtype)


def _attention(q, k, v, sinks, batch, seq):
    T = q.shape[0]
    bq = WINDOW
    per_step = ATT_BLOCKS_PER_STEP
    ns = seq // (bq * per_step)
    cur = lambda b, n: (b * ns + n, 0)
    prev = lambda b, n: (jnp.maximum((b * ns + n) * per_step - 1, 0), 0)
    return pl.pallas_call(
        _attn_kernel,
        out_shape=jax.ShapeDtypeStruct((T, ATT_WIDTH), BF16),
        grid=(batch, ns),
        in_specs=[pl.BlockSpec(memory_space=pltpu.SMEM),
                  pl.BlockSpec((bq * per_step, ATT_WIDTH), cur),
                  pl.BlockSpec((bq, KV_WIDTH), prev), pl.BlockSpec((bq * per_step, KV_WIDTH), cur),
                  pl.BlockSpec((bq, KV_WIDTH), prev), pl.BlockSpec((bq * per_step, KV_WIDTH), cur)],
        out_specs=pl.BlockSpec((bq * per_step, ATT_WIDTH), cur),
        compiler_params=_params("arbitrary", "arbitrary"),
    )(sinks, q, k, k, v, v)


def _rwkv_kernel(z_ref, w0_ref, wd_ref, a0_ref, wa_ref, wg_ref, kk_ref, ka_ref, rk_ref, lnw_ref, lnb_ref,
                 y_ref, s_ref):
    c = pl.program_id(1)

    @pl.when(c == 0)
    def _():
        s_ref[...] = jnp.zeros_like(s_ref)

    L = RWKV_CHUNK
    rows = z_ref.shape[0]
    n_chunks = rows // L
    C = RWKV_WIDTH
    N = RWKV_HEAD
    r = z_ref[:, 0:C]
    k = z_ref[:, C:2 * C]
    v = z_ref[:, 2 * C:3 * C]
    xw = z_ref[:, 3 * C:3 * C + DECAY_LORA]
    xa = z_ref[:, 3 * C + DECAY_LORA:3 * C + DECAY_LORA + AAA_LORA]
    xg = z_ref[:, 3 * C + DECAY_LORA + AAA_LORA:]

    wlin = w0_ref[...] + _bdot(jnp.tanh(xw), wd_ref[...])
    softplus = jnp.maximum(-wlin, 0.0) + jnp.log(1.0 + jnp.exp(-jnp.abs(wlin)))
    logdecay = -jnp.exp(-softplus - 0.5)
    a = _sigmoid(a0_ref[...] + _bdot(xa, wa_ref[...]))
    g = _bdot(_sigmoid(xg), wg_ref[...])
    kk = k * kk_ref[...]
    k2 = k * (1.0 + (a - 1.0) * ka_ref[...])

    hr = lax.broadcasted_iota(jnp.int32, (C, C), 0) // N
    hc = lax.broadcasted_iota(jnp.int32, (C, C), 1) // N
    head_ones = jnp.where(hr == hc, 1.0, 0.0).astype(BF16)
    head_sum = lambda t: jnp.dot(t.astype(BF16), head_ones, preferred_element_type=F32)

    kkn = kk / jnp.maximum(jnp.sqrt(head_sum(kk * kk)), 1e-12)

    assert L == N and 2 * N == LANES, "the pair-packed block-diagonal products need chunk == head size == 64"
    row = lax.broadcasted_iota(jnp.int32, (L, L), 0)
    col = lax.broadcasted_iota(jnp.int32, (L, L), 1)
    tri = jnp.where(row >= col, 1.0, 0.0).astype(BF16)
    ld_1 = logdecay.astype(BF16)
    rest = logdecay - ld_1.astype(F32)
    ld_2 = rest.astype(BF16)
    ld_3 = (rest - ld_2.astype(F32)).astype(BF16)
    cums = []
    for ci in range(n_chunks):
        rs = slice(ci * L, (ci + 1) * L)
        cums.append(jnp.dot(tri, ld_1[rs], preferred_element_type=F32)
                    + jnp.dot(tri, ld_2[rs], preferred_element_type=F32)
                    + jnp.dot(tri, ld_3[rs], preferred_element_type=F32))
    cum = jnp.concatenate(cums, axis=0) if n_chunks > 1 else cums[0]
    last = [cums[ci][L - 1:L, :] for ci in range(n_chunks)]
    cum_last = jnp.concatenate([jnp.broadcast_to(t, (L, C)) for t in last], axis=0) if n_chunks > 1 \
        else jnp.broadcast_to(last[0], (L, C))
    p_in = jnp.exp(cum)
    p_inv = jnp.exp(-cum)
    p_rem = jnp.exp(cum_last - cum)
    b = kkn * a
    at_f = (-kkn * jnp.exp(cum - logdecay)).astype(BF16)
    rt_f = r * p_in
    rt_b = rt_f.astype(BF16)
    bt_b = (b * p_inv).astype(BF16)
    kt_b = (k2 * p_inv).astype(BF16)
    br_b = (b * p_rem).astype(BF16)
    kr_b = (k2 * p_rem).astype(BF16)
    v_b = v.astype(BF16)

    dot = lambda x, y: jnp.dot(x.astype(BF16), y.astype(BF16), preferred_element_type=F32)
    dot_nt = lambda x, y: lax.dot_general(x.astype(BF16), y.astype(BF16), (((1,), (1,)), ((), ())),
                                          preferred_element_type=F32)
    dot_tn = lambda x, y: lax.dot_general(x.astype(BF16), y.astype(BF16), (((0,), (0,)), ((), ())),
                                          preferred_element_type=F32)

    n_pairs = C // LANES
    lane = lax.broadcasted_iota(jnp.int32, (L, LANES), 1)
    prow = lax.broadcasted_iota(jnp.int32, (L, LANES), 0)
    first = lane < N
    pcol = jnp.where(first, lane, lane - N)
    p_lower = prow > pcol
    p_lower_eq = prow >= pcol
    p_eye = jnp.where(prow == pcol, 1.0, 0.0)
    zero = jnp.zeros((), BF16)

    def bdiag(t):
        t = t.astype(BF16)
        return jnp.concatenate([jnp.where(first, t, zero), jnp.where(first, zero, t)], axis=0)

    pieces = [(ci, pj) for ci in range(n_chunks) for pj in range(n_pairs)]
    cut = lambda t, ci, pj: t[ci * L:(ci + 1) * L, pj * LANES:(pj + 1) * LANES]
    at = {p: cut(at_f, *p) for p in pieces}
    vv = {p: cut(v_b, *p) for p in pieces}
    m = {p: dot_nt(jnp.concatenate([at[p], cut(rt_b, *p)], axis=0),
                   jnp.concatenate([bdiag(cut(bt_b, *p)), bdiag(cut(kt_b, *p))], axis=0)) for p in pieces}
    m_ab = {p: jnp.where(p_lower, m[p][:L, :LANES], 0.0) for p in pieces}
    m_ak = {p: jnp.where(p_lower, m[p][:L, LANES:], 0.0).astype(BF16) for p in pieces}
    m_rb = {p: jnp.where(p_lower_eq, m[p][L:, :LANES], 0.0).astype(BF16) for p in pieces}
    m_rk = {p: jnp.where(p_lower_eq, m[p][L:, LANES:], 0.0).astype(BF16) for p in pieces}
    inv = {p: p_eye + m_ab[p] for p in pieces}
    pw = {p: m_ab[p].astype(BF16) for p in pieces}
    pw = {p: dot(pw[p], bdiag(pw[p])).astype(BF16) for p in pieces}
    for _ in range(max(2, (L - 1).bit_length()) - 2):
        both = {p: dot(jnp.concatenate([pw[p], inv[p].astype(BF16)], axis=0), bdiag(pw[p])) for p in pieces}
        pw = {p: both[p][:L].astype(BF16) for p in pieces}
        inv = {p: inv[p] + both[p][L:] for p in pieces}
    inv = {p: (inv[p] + dot(inv[p], bdiag(pw[p]))).astype(BF16) for p in pieces}
    mv = {p: dot(jnp.concatenate([m_ak[p], m_rk[p]], axis=0), bdiag(vv[p])) for p in pieces}
    wu = {p: dot(inv[p], jnp.concatenate([bdiag(at[p]), bdiag(mv[p][:L])], axis=1)) for p in pieces}
    w = {p: wu[p][:, :LANES].astype(BF16) for p in pieces}
    u0 = {p: wu[p][:, LANES:].astype(BF16) for p in pieces}
    ry = {p: dot(m_rb[p], jnp.concatenate([bdiag(w[p]), bdiag(u0[p])], axis=1)) for p in pieces}
    rq = {p: (cut(rt_f, *p) + ry[p][:, :LANES]).astype(BF16) for p in pieces}
    y0 = {p: ry[p][:, LANES:] + mv[p][L:] for p in pieces}
    same_head = (lax.broadcasted_iota(jnp.int32, (LANES, LANES), 0) < N) == \
                (lax.broadcasted_iota(jnp.int32, (LANES, LANES), 1) < N)
    gg = {p: jnp.where(same_head, dot_tn(w[p], cut(br_b, *p)), 0.0).astype(BF16) for p in pieces}
    hh_full = {p: dot_tn(jnp.concatenate([u0[p], vv[p]], axis=0),
                         jnp.concatenate([cut(br_b, *p), cut(kr_b, *p)], axis=0)) for p in pieces}
    hh = {p: jnp.where(first, hh_full[p][:N], hh_full[p][N:]) for p in pieces}

    ys = {}
    for pj in range(n_pairs):
        state = s_ref[pj]
        for ci in range(n_chunks):
            p = (ci, pj)
            sb = state.astype(BF16)
            ys[p] = y0[p] + dot_nt(rq[p], bdiag(sb))
            p_tot = p_in[(ci + 1) * L - 1:(ci + 1) * L, pj * LANES:(pj + 1) * LANES]
            state = state * p_tot + dot(sb, gg[p]) + hh[p]
        s_ref[pj] = state
    y = jnp.concatenate([jnp.concatenate([ys[(ci, pj)] for pj in range(n_pairs)], axis=1)
                         for ci in range(n_chunks)], axis=0)

    mu = head_sum(y) * (1.0 / N)
    dev = y - mu
    var = head_sum(dev * dev) * (1.0 / N)
    yn = dev * lax.rsqrt(var + RWKV_GN_EPS)
    bonus = head_sum(r * k2 * rk_ref[...]) * v
    y_ref[...] = ((yn * lnw_ref[...] + lnb_ref[...] + bonus) * g).astype(y_ref.dtype)


def _rwkv(zr, w0, wd, a0, wa, wg, k_k, k_a, r_k, ln_w, ln_b, batch, seq):
    T = zr.shape[0]
    L = RWKV_CHUNK * RWKV_CHUNKS_PER_STEP
    nc = seq // L
    vec = lambda a: a.reshape(1, RWKV_WIDTH)
    full = lambda a: pl.BlockSpec(a.shape, lambda b, c: (0,) * a.ndim)
    args = (vec(w0), wd, vec(a0), wa, wg, vec(k_k), vec(k_a), vec(r_k), vec(ln_w), vec(ln_b))
    return pl.pallas_call(
        _rwkv_kernel,
        out_shape=jax.ShapeDtypeStruct((T, RWKV_WIDTH), BF16),
        grid=(batch, nc),
        in_specs=[pl.BlockSpec((L, SHIFT_COLS), lambda b, c: (b * nc + c, 0))] + [full(a) for a in args],
        out_specs=pl.BlockSpec((L, RWKV_WIDTH), lambda b, c: (b * nc + c, 0)),
        scratch_shapes=[pltpu.VMEM((RWKV_WIDTH // LANES, RWKV_HEAD, LANES), F32)],
        compiler_params=_params("arbitrary", "arbitrary"),
    )(zr, *args)


def _merge_kernel(x_ref, ya_ref, yr_ref, gate_ref, wba_ref, wbr_ref, wo_ref, lnm_ref, wr_ref, br_ref,
                  x1_ref, h2_ref, route_ref, cnt_ref, cnt_scr):
    i = pl.program_id(0)

    @pl.when(i == 0)
    def _():
        cnt_scr[...] = jnp.zeros_like(cnt_scr)

    tm = x_ref.shape[0]
    D = D_MODEL
    gates = gate_ref[...].astype(F32)
    ya = jnp.dot(ya_ref[...], wba_ref[...], preferred_element_type=F32)
    yr = jnp.dot(yr_ref[...], wbr_ref[...], preferred_element_type=F32)
    merged = gates[:, :D] * ya + gates[:, D:] * yr
    x1 = x_ref[...] + jnp.dot(merged.astype(BF16), wo_ref[...], preferred_element_type=F32)
    x1_ref[...] = x1
    h2 = _rms(x1, lnm_ref[...])
    _store_slabs(h2_ref, h2)

    h_hi = h2.astype(BF16)
    h_lo = (h2 - h_hi.astype(F32)).astype(BF16)
    parts = (jnp.dot(h_hi, wr_ref[...], preferred_element_type=F32)
             + jnp.dot(h_lo, wr_ref[...], preferred_element_type=F32))
    logits = parts[:, :LANES] + parts[:, LANES:] + br_ref[...]
    lane = lax.broadcasted_iota(jnp.int32, logits.shape, 1)
    big = jnp.int32(1 << 20)
    gl = jnp.where(lane < N_GROUPS, logits, NEG_INF)
    gmax = jnp.max(gl, axis=-1, keepdims=True)
    gidx = jnp.min(jnp.where(gl == gmax, lane, big), axis=-1, keepdims=True)
    gsum = jnp.sum(jnp.where(lane < N_GROUPS, jnp.exp(logits - gmax), 0.0), axis=-1, keepdims=True)
    g_w = 1.0 / gsum
    lo = N_GROUPS + EXPERTS_PER_GROUP * gidx
    el = jnp.where((lane >= lo) & (lane < lo + EXPERTS_PER_GROUP), logits, NEG_INF)
    m1 = jnp.max(el, axis=-1, keepdims=True)
    i1 = jnp.min(jnp.where(el == m1, lane, big), axis=-1, keepdims=True)
    el2 = jnp.where(lane == i1, NEG_INF, el)
    m2 = jnp.max(el2, axis=-1, keepdims=True)
    i2 = jnp.min(jnp.where(el2 == m2, lane, big), axis=-1, keepdims=True)
    d = jnp.exp(m2 - m1)
    w1 = g_w / (1.0 + d)
    w2 = g_w * d / (1.0 + d)
    e1 = i1 - N_GROUPS
    e2 = i2 - N_GROUPS

    hit1 = lane == e1
    hit2 = lane == e2
    onehot = jnp.where(hit1 | hit2, 1.0, 0.0).astype(BF16)
    r_i = lax.broadcasted_iota(jnp.int32, (tm, tm), 0)
    c_i = lax.broadcasted_iota(jnp.int32, (tm, tm), 1)
    before = jnp.dot(jnp.where(r_i > c_i, 1.0, 0.0).astype(BF16), onehot, preferred_element_type=F32)
    before = before + cnt_scr[...]
    rank1 = jnp.sum(jnp.where(hit1, before, 0.0), axis=-1, keepdims=True)
    rank2 = jnp.sum(jnp.where(hit2, before, 0.0), axis=-1, keepdims=True)
    cnt_scr[...] = cnt_scr[...] + jnp.sum(onehot.astype(F32), axis=0, keepdims=True)
    cnt_ref[...] = cnt_scr[...]

    route = jnp.where(lane == 0, e1.astype(F32), 0.0)
    route = jnp.where(lane == 1, e2.astype(F32), route)
    route = jnp.where(lane == 2, rank1, route)
    route = jnp.where(lane == 3, rank2, route)
    route = jnp.where(lane == 4, w1, route)
    route = jnp.where(lane == 5, w2, route)
    route_ref[...] = route


def _merge(x2, y_att, y_rwkv, gates, wba, wbr, wo, ln_moe, w_router, b_router):
    T, D = x2.shape
    tm = ROW_TILE
    row = lambda n: pl.BlockSpec((tm, n), lambda i: (i, 0))
    full = lambda a: pl.BlockSpec(a.shape, lambda i: (0,) * a.ndim)
    ln_moe = ln_moe.reshape(1, D)
    return pl.pallas_call(
        _merge_kernel,
        out_shape=(jax.ShapeDtypeStruct((T, D), F32), jax.ShapeDtypeStruct((T * SLAB_ROWS, LANES), F32),
                   jax.ShapeDtypeStruct((T, LANES), F32), jax.ShapeDtypeStruct((1, LANES), F32)),
        grid=(T // tm,),
        in_specs=[row(D), row(ATT_WIDTH), row(RWKV_WIDTH), row(GATE_COLS),
                  full(wba), full(wbr), full(wo), full(ln_moe), full(w_router), full(b_router)],
        out_specs=(row(D), pl.BlockSpec((tm * SLAB_ROWS, LANES), lambda i: (i, 0)), row(LANES),
                   pl.BlockSpec((1, LANES), lambda i: (0, 0))),
        scratch_shapes=[pltpu.VMEM((1, LANES), F32)],
        compiler_params=_params("arbitrary"),
    )(x2, y_att, y_rwkv, gates, wba, wbr, wo, ln_moe, w_router, b_router)


def _dispatch_kernel(dest_ref, h_ref, xrows_hbm, row_sem, *, tile):
    base = pl.program_id(0) * tile

    def row_copy(j, slot):
        return pltpu.make_async_copy(_slab(h_ref, j), _slab(xrows_hbm, dest_ref[2 * (base + j) + slot]), row_sem)

    def issue(j, carry):
        row_copy(j, 0).start(priority=0)
        row_copy(j, 1).start(priority=1)
        return carry

    lax.fori_loop(0, tile, issue, 0, unroll=8)

    def drain(j, carry):
        row_copy(j, 0).wait()
        row_copy(j, 1).wait()
        return carry

    lax.fori_loop(0, tile, drain, 0, unroll=8)


def _dispatch(h2_slabs, dest):
    T = h2_slabs.shape[0] // SLAB_ROWS
    tile = min(DISPATCH_TILE, T)
    assert T % tile == 0
    return pl.pallas_call(
        functools.partial(_dispatch_kernel, tile=tile),
        out_shape=jax.ShapeDtypeStruct((2 * T * SLAB_ROWS, LANES), F32),
        grid_spec=pltpu.PrefetchScalarGridSpec(
            num_scalar_prefetch=1,
            grid=(T // tile,),
            in_specs=[pl.BlockSpec((tile * SLAB_ROWS, LANES), lambda i, d: (i, 0))],
            out_specs=pl.BlockSpec(memory_space=pl.ANY),
            scratch_shapes=[pltpu.SemaphoreType.DMA],
        ),
        compiler_params=_params("arbitrary"),
    )(dest, h2_slabs)


def _expert_kernel(item_e, item_b, item_lo, item_hi, x_ref, wg_ref, wu_ref, wd_ref, y_ref, wgu_bf, wd_bf):
    w = pl.program_id(0)
    lo = item_lo[w]
    hi = item_hi[w]

    @pl.when((w == 0) | (item_e[w] != item_e[jnp.maximum(w - 1, 0)]))
    def _():
        wgu_bf[:, :D_EXPERT] = wg_ref[0].astype(BF16)
        wgu_bf[:, D_EXPERT:] = wu_ref[0].astype(BF16)
        wd_bf[...] = wd_ref[0].astype(BF16)

    @pl.when(hi > lo)
    def _():
        bm = x_ref.shape[0] // SLAB_ROWS
        xb = _load_slabs(x_ref, bm).astype(BF16)
        hgu = jnp.dot(xb, wgu_bf[...], preferred_element_type=F32)
        hg = hgu[:, :D_EXPERT]
        hid = hg * _sigmoid(hg) * hgu[:, D_EXPERT:]
        y = jnp.dot(hid.astype(BF16), wd_bf[...], preferred_element_type=F32)

        @pl.when(lo == 0)
        def _():
            _store_slabs(y_ref, y)

        @pl.when(lo > 0)
        def _():
            rows = lax.broadcasted_iota(jnp.int32, y.shape, 0)
            _store_slabs(y_ref, jnp.where((rows >= lo) & (rows < hi), y, _load_slabs(y_ref, bm)))


def _experts(xrows, items, wg, wu, wd):
    bm = EXPERT_BLOCK
    D = D_MODEL
    n_items = items[0].shape[0]
    block = pl.BlockSpec((bm * SLAB_ROWS, LANES), lambda w, ie, ib, lo, hi: (ib[w], 0))
    return pl.pallas_call(
        _expert_kernel,
        out_shape=jax.ShapeDtypeStruct(xrows.shape, F32),
        grid_spec=pltpu.PrefetchScalarGridSpec(
            num_scalar_prefetch=4,
            grid=(n_items,),
            in_specs=[block,
                      pl.BlockSpec((1, D, D_EXPERT), lambda w, ie, ib, lo, hi: (ie[w], 0, 0)),
                      pl.BlockSpec((1, D, D_EXPERT), lambda w, ie, ib, lo, hi: (ie[w], 0, 0)),
                      pl.BlockSpec((1, D_EXPERT, D), lambda w, ie, ib, lo, hi: (ie[w], 0, 0))],
            out_specs=block,
            scratch_shapes=[pltpu.VMEM((D, 2 * D_EXPERT), BF16), pltpu.VMEM((D_EXPERT, D), BF16)],
        ),
        compiler_params=_params("arbitrary"),
    )(*items, xrows, wg, wu, wd)


def _work_items(counts, n_rows):
    bm = EXPERT_BLOCK
    nb = n_rows // bm
    n_items = nb + N_EXPERTS
    ends = jnp.cumsum(counts)
    starts = ends - counts
    first_blk = starts // bm
    last_blk = (ends - 1) // bm
    per_e = jnp.where(counts > 0, last_blk - first_blk + 1, 0)
    item_end = jnp.cumsum(per_e)
    total = item_end[-1]
    w = jnp.arange(n_items, dtype=jnp.int32)
    wc = jnp.minimum(w, total - 1)
    e = jnp.minimum(jnp.sum((item_end[None, :] <= wc[:, None]).astype(jnp.int32), axis=1), N_EXPERTS - 1)
    b = first_blk[e] + (wc - (item_end[e] - per_e[e]))
    lo = jnp.maximum(starts[e], b * bm) - b * bm
    hi = jnp.minimum(ends[e], (b + 1) * bm) - b * bm
    live = w < total
    lo = jnp.where(live, lo, 0)
    hi = jnp.where(live, hi, 0)
    return (e.astype(jnp.int32), b.astype(jnp.int32), lo.astype(jnp.int32), hi.astype(jnp.int32)), starts


def _final_kernel(dest_ref, yrows_hbm, x1_ref, route_ref, p_ref, lnp_ref, wpg_ref, wpp_ref, lnf_ref,
                  o_ref, rows_ref, row_sems):
    i = pl.program_id(0)
    n = pl.num_programs(0)
    tm = x1_ref.shape[0]

    def row_copy(tile, j, slot, par):
        return pltpu.make_async_copy(_slab(yrows_hbm, dest_ref[2 * (tile * tm + j) + slot]),
                                     _slab(rows_ref.at[par, slot], j), row_sems.at[par])

    def wait_tile(par):
        pltpu.make_async_copy(rows_ref.at[par], rows_ref.at[par], row_sems.at[par]).wait()

    @pl.when(i == 0)
    def _():
        def body(j, carry):
            row_copy(0, j, 0, 0).start(priority=0)
            row_copy(0, j, 1, 0).start(priority=1)
            return carry
        lax.fori_loop(0, tm, body, 0, unroll=8)

    def step(par):
        wait_tile(par)
        nxt = jnp.minimum(i + 1, n - 1)
        for j in range(tm):
            row_copy(nxt, j, 0, 1 - par).start(priority=0)
            row_copy(nxt, j, 1, 1 - par).start(priority=1)
        pp = jnp.dot(p_ref[...].astype(BF16), wpp_ref[...], preferred_element_type=F32)
        route = route_ref[...]
        x2 = (x1_ref[...] + route[:, 4:5] * _load_slabs(rows_ref.at[par, 0], tm)
              + route[:, 5:6] * _load_slabs(rows_ref.at[par, 1], tm))
        gate = _sigmoid(jnp.dot(_rms(x2, lnp_ref[...]).astype(BF16), wpg_ref[...], preferred_element_type=F32))
        x3 = x2 + gate * pp
        o_ref[...] = _rms(x3, lnf_ref[...])

        @pl.when(i == n - 1)
        def _():
            wait_tile(1 - par)

    for par in range(2):
        pl.when(i % 2 == par)(functools.partial(step, par))


def _final(dest, yrows, x1, route, p2, ln_ple, wpg, wpp, ln_final):
    T, D = x1.shape
    tm = GATHER_TILE
    row = lambda n: pl.BlockSpec((tm, n), lambda i, d: (i, 0))
    full = lambda a: pl.BlockSpec(a.shape, lambda i, d: (0,) * a.ndim)
    ln_ple = ln_ple.reshape(1, D)
    ln_final = ln_final.reshape(1, D)
    return pl.pallas_call(
        _final_kernel,
        out_shape=jax.ShapeDtypeStruct((T, D), F32),
        grid_spec=pltpu.PrefetchScalarGridSpec(
            num_scalar_prefetch=1,
            grid=(T // tm,),
            in_specs=[pl.BlockSpec(memory_space=pl.ANY),
                      row(D), row(LANES), row(PLE_DIM), full(ln_ple), full(wpg), full(wpp), full(ln_final)],
            out_specs=row(D),
            scratch_shapes=[pltpu.VMEM((2, 2, tm * SLAB_ROWS, LANES), F32), pltpu.SemaphoreType.DMA((2,))],
        ),
        compiler_params=_params("arbitrary"),
    )(dest, yrows, x1, route, p2, ln_ple, wpg, wpp, ln_final)


def kernel(x, p, positions, ln_mix, w_in, mu_shift, w0, w_decay_up, a0, w_aaa_up, w_gate_up, k_k, k_a, r_k, ln_x_w, ln_x_b, sinks, w_branch_att, w_branch_rwkv, w_out, ln_moe, w_group, b_group, w_expert, b_expert, w_gate_e, w_up_e, w_down_e, ln_ple, w_ple_gate, w_ple_proj, ln_final):
    B, S, D = x.shape
    T = B * S
    depth = w_in.shape[0]
    assert D == D_MODEL and S % ROW_TILE == 0 and S % (WINDOW * ATT_BLOCKS_PER_STEP) == 0 and S % (RWKV_CHUNK * RWKV_CHUNKS_PER_STEP) == 0
    assert T % GATHER_TILE == 0 and (2 * T) % EXPERT_BLOCK == 0
    cos, sin = _rope_tables(positions)
    x2 = x.reshape(T, D)
    out = None
    for i in range(depth):
        q, k, v, zr, gates = _inproj(x2, ln_mix[i], w_in[i].astype(BF16), mu_shift[i], cos, sin, S)
        y_att = _attention(q, k, v, sinks[i], B, S)
        y_rwkv = _rwkv(zr, w0[i], w_decay_up[i], a0[i], w_aaa_up[i], w_gate_up[i], k_k[i], k_a[i], r_k[i],
                       ln_x_w[i], ln_x_b[i], B, S)
        pad = LANES - N_GROUPS - N_EXPERTS
        w_router = jnp.concatenate([w_group[i], w_expert[i], jnp.zeros((D, pad), F32)], axis=1)
        w_router_hi = w_router.astype(BF16)
        w_router = jnp.concatenate([w_router_hi, (w_router - w_router_hi.astype(F32)).astype(BF16)], axis=1)
        b_router = jnp.concatenate([b_group[i], b_expert[i], jnp.zeros((pad,), F32)]).reshape(1, LANES)
        x1, h2, route, cnt = _merge(x2, y_att, y_rwkv, gates, w_branch_att[i].astype(BF16),
                                    w_branch_rwkv[i].astype(BF16), w_out[i].astype(BF16), ln_moe[i],
                                    w_router, b_router)
        counts = cnt[0, :N_EXPERTS].astype(jnp.int32)
        items, starts = _work_items(counts, 2 * T)
        expert = route[:, 0:2].astype(jnp.int32)
        first_row = jnp.sum(jnp.where(expert[..., None] == jnp.arange(N_EXPERTS, dtype=jnp.int32), starts, 0), axis=-1)
        dest = (first_row + route[:, 2:4].astype(jnp.int32)).reshape(-1)
        xrows = _dispatch(h2, dest)
        yrows = _experts(xrows, items, w_gate_e[i], w_up_e[i], w_down_e[i])
        last = i == depth - 1
        assert last, "the final-norm kernel closes the only layer"
        out = _final(dest, yrows, x1, route, p[i].reshape(T, PLE_DIM), ln_ple[i], w_ple_gate[i].astype(BF16),
                     w_ple_proj[i].astype(BF16), ln_final)
    return out.reshape(B, S, D)
```

```python
import functools

import jax
import jax.numpy as jnp
from jax import lax
from jax.experimental import pallas as pl
from jax.experimental.pallas import tpu as pltpu

F32 = jnp.float32
BF16 = jnp.bfloat16
HIGHEST = lax.Precision.HIGHEST

D_MODEL = 1024
PLE_DIM = 256
ATT_HEADS = 8
ATT_KV_HEADS = 2
HEAD_DIM = 64
ATT_WIDTH = ATT_HEADS * HEAD_DIM
KV_WIDTH = ATT_KV_HEADS * HEAD_DIM
WINDOW = 128
ROPE_THETA = 10000.0
RWKV_HEADS = 8
RWKV_HEAD = 64
RWKV_WIDTH = RWKV_HEADS * RWKV_HEAD
DECAY_LORA = 64
AAA_LORA = 64
GATE_LORA = 128
RWKV_GN_EPS = 64e-5
ATT_COLS = ATT_WIDTH + 2 * KV_WIDTH
SHIFT_COLS = 3 * RWKV_WIDTH + DECAY_LORA + AAA_LORA + GATE_LORA
GATE_COLS = 2 * D_MODEL
N_GROUPS = 4
EXPERTS_PER_GROUP = 8
N_EXPERTS = N_GROUPS * EXPERTS_PER_GROUP
D_EXPERT = 512
NORM_EPS = 1e-6
NEG_INF = -1e30

LANES = 128
SLAB_ROWS = D_MODEL // LANES
VMEM_LIMIT = 56 * 1024 * 1024

ROW_TILE = 256
MERGE_TILE = 512
ATT_BLOCKS_PER_STEP = 4
RWKV_CHUNK = 64
RWKV_CHUNKS_PER_STEP = 8
RWKV_STAGE_GROUP = 4
EXPERT_BLOCK = 512
EXPERT_SUB_BLOCK = 128
GATHER_TILE = 256
DISPATCH_TILE = 512


def _params(*sem):
    return pltpu.CompilerParams(dimension_semantics=sem, vmem_limit_bytes=VMEM_LIMIT)


def _bdot(a, b):
    return jnp.dot(a.astype(BF16), b.astype(BF16), preferred_element_type=F32)


def _bdot_nt(a, b):
    return lax.dot_general(a.astype(BF16), b.astype(BF16), (((1,), (1,)), ((), ())),
                           preferred_element_type=F32)


def _fdot(a, b):
    return jnp.dot(a, b, preferred_element_type=F32, precision=HIGHEST)


def _rms(x, g):
    return x * lax.rsqrt(jnp.mean(x * x, axis=-1, keepdims=True) + NORM_EPS) * g


def _sigmoid(x):
    return 1.0 / (1.0 + jnp.exp(-x))


def _store_slabs(ref, val, first=0):
    m = val.shape[0]
    for j in range(SLAB_ROWS):
        ref[pl.ds(first * SLAB_ROWS + j, m, stride=SLAB_ROWS), :] = val[:, j * LANES:(j + 1) * LANES]


def _load_slabs(ref, m, first=0):
    return jnp.concatenate([ref[pl.ds(first * SLAB_ROWS + j, m, stride=SLAB_ROWS), :] for j in range(SLAB_ROWS)],
                           axis=1)


def _slab(ref, index):
    return ref.at[pl.ds(pl.multiple_of(index * SLAB_ROWS, SLAB_ROWS), SLAB_ROWS)]


def _rope_table_kernel(pos_ref, invf_ref, cos_ref, sin_ref):
    half = HEAD_DIM // 2
    per_row = LANES // half
    rows = pos_ref.shape[0]
    ang = pos_ref[...].astype(F32) * invf_ref[...]
    group = lax.broadcasted_iota(jnp.int32, ang.shape, 1) // half
    sign = jnp.where(group % 2 == 0, -1.0, 1.0)
    for table, out_ref, scale in ((jnp.cos(ang), cos_ref, None), (jnp.sin(ang), sin_ref, sign)):
        rolled = [table] + [pltpu.roll(table, half * j, 1) for j in range(1, per_row)]
        for m in range(per_row):
            out = rolled[(per_row - 1 - m) % per_row]
            for g in range(per_row - 1):
                out = jnp.where(group == g, rolled[(g - m) % per_row], out)
            out_ref[pl.ds(m, rows, stride=per_row), :] = out if scale is None else out * scale


def _rope_tables(positions):
    T = positions.size
    half = HEAD_DIM // 2
    per_row = LANES // half
    inv_freq = ROPE_THETA ** (-jnp.arange(half, dtype=F32) / half)
    invf = jnp.tile(inv_freq, per_row).reshape(1, LANES)
    pos = jnp.repeat(positions.reshape(T // per_row, per_row), half, axis=1)
    rows = T // per_row
    tr = min(rows, 1024)
    return pl.pallas_call(
        _rope_table_kernel,
        out_shape=(jax.ShapeDtypeStruct((T, LANES), F32),) * 2,
        grid=(rows // tr,),
        in_specs=[pl.BlockSpec((tr, LANES), lambda i: (i, 0)),
                  pl.BlockSpec((1, LANES), lambda i: (0, 0))],
        out_specs=(pl.BlockSpec((tr * per_row, LANES), lambda i: (i, 0)),) * 2,
        compiler_params=_params("arbitrary"),
    )(pos, invf)


def _rope(t, cos, sin):
    n = t.shape[1]
    reps = n // LANES
    c = jnp.tile(cos, (1, reps)) if reps > 1 else cos
    s = jnp.tile(sin, (1, reps)) if reps > 1 else sin
    lane = lax.broadcasted_iota(jnp.int32, t.shape, 1)
    first_half = (lane % HEAD_DIM) < (HEAD_DIM // 2)
    partner = jnp.where(first_half, pltpu.roll(t, n - HEAD_DIM // 2, 1), pltpu.roll(t, HEAD_DIM // 2, 1))
    return t * c + partner * s


def _inproj_kernel(x_ref, g_ref, w_ref, mu_ref, cos_ref, sin_ref,
                   q_ref, k_ref, v_ref, zr_ref, gate_ref, carry_ref, *, tiles_per_seq):
    i = pl.program_id(0)
    tm = x_ref.shape[0]
    h = _rms(x_ref[...], g_ref[...]).astype(BF16)
    cos = cos_ref[...]
    sin = sin_ref[...]

    za = jnp.dot(h, w_ref[:, :ATT_COLS], preferred_element_type=F32)
    q_ref[...] = (_rope(za[:, :ATT_WIDTH], cos, sin) * (HEAD_DIM ** -0.5)).astype(BF16)
    k_ref[...] = _rope(za[:, ATT_WIDTH:ATT_WIDTH + KV_WIDTH], cos, sin).astype(BF16)
    v_ref[...] = za[:, ATT_WIDTH + KV_WIDTH:].astype(BF16)

    zs = jnp.dot(h, w_ref[:, ATT_COLS:ATT_COLS + SHIFT_COLS], preferred_element_type=F32)
    row = lax.broadcasted_iota(jnp.int32, zs.shape, 0)
    seq_start = (i % tiles_per_seq) == 0
    before = jnp.where(seq_start, 0.0, carry_ref[0:1, :])
    prev = jnp.where(row == 0, before, pltpu.roll(zs, 1, 0))
    carry_ref[0:1, :] = zs[tm - 1:tm, :]
    zr_ref[...] = zs + (prev - zs) * mu_ref[...]

    zg = jnp.dot(h, w_ref[:, ATT_COLS + SHIFT_COLS:], preferred_element_type=F32)
    gate_ref[...] = _sigmoid(zg).astype(BF16)


def _inproj(x2, ln, w_in, mu, cos, sin, seq):
    T, D = x2.shape
    tm = ROW_TILE
    in_width = w_in.shape[1]
    row = lambda n: pl.BlockSpec((tm, n), lambda i: (i, 0))
    full = lambda a: pl.BlockSpec(a.shape, lambda i: (0,) * a.ndim)
    ln = ln.reshape(1, D)
    mu = mu.reshape(1, SHIFT_COLS)
    return pl.pallas_call(
        functools.partial(_inproj_kernel, tiles_per_seq=seq // tm),
        out_shape=(jax.ShapeDtypeStruct((T, ATT_WIDTH), BF16),
                   jax.ShapeDtypeStruct((T, KV_WIDTH), BF16),
                   jax.ShapeDtypeStruct((T, KV_WIDTH), BF16),
                   jax.ShapeDtypeStruct((T, SHIFT_COLS), F32),
                   jax.ShapeDtypeStruct((T, GATE_COLS), BF16)),
        grid=(T // tm,),
        in_specs=[row(D), full(ln), full(w_in), full(mu), row(LANES), row(LANES)],
        out_specs=(row(ATT_WIDTH), row(KV_WIDTH), row(KV_WIDTH), row(SHIFT_COLS), row(GATE_COLS)),
        scratch_shapes=[pltpu.VMEM((8, SHIFT_COLS), F32)],
        compiler_params=_params("arbitrary"),
    )(x2, ln, w_in, mu, cos, sin)


def _attn_kernel(sink_ref, q_ref, kp_ref, kc_ref, vp_ref, vc_ref, o_ref):
    n = pl.program_id(1)
    bq = WINDOW
    kall = jnp.concatenate([kp_ref[...], kc_ref[...]], axis=0)
    vall = jnp.concatenate([vp_ref[...], vc_ref[...]], axis=0)
    qi = lax.broadcasted_iota(jnp.int32, (bq, 2 * bq), 0)
    si = lax.broadcasted_iota(jnp.int32, (bq, 2 * bq), 1)
    diff = qi + bq - si
    band = (diff >= 0) & (diff < WINDOW)
    grp = ATT_HEADS // ATT_KV_HEADS
    for blk in range(q_ref.shape[0] // bq):
        valid = band & ((si >= bq) | (n > 0)) if blk == 0 else band
        q = q_ref[blk * bq:(blk + 1) * bq, :]
        kcat = kall[blk * bq:(blk + 2) * bq]
        vcat = vall[blk * bq:(blk + 2) * bq]
        outs = []
        for hd in range(ATT_HEADS):
            j = hd // grp
            qh = q[:, hd * HEAD_DIM:(hd + 1) * HEAD_DIM]
            kj = kcat[:, j * HEAD_DIM:(j + 1) * HEAD_DIM]
            vj = vcat[:, j * HEAD_DIM:(j + 1) * HEAD_DIM]
            s = lax.dot_general(qh, kj, (((1,), (1,)), ((), ())), preferred_element_type=F32)
            s = jnp.where(valid, s, NEG_INF)
            sink = sink_ref[hd]
            m = jnp.maximum(jnp.max(s, axis=-1, keepdims=True), sink)
            e = jnp.exp(s - m)
            denom = jnp.sum(e, axis=-1, keepdims=True) + jnp.exp(sink - m)
            pv = jnp.dot(e.astype(BF16), vj, preferred_element_type=F32)
            outs.append(pv / denom)
        o_ref[blk * bq:(blk + 1) * bq, :] = jnp.concatenate(outs, axis=1).astype(o_ref.dtype)


def _attention(q, k, v, sinks, batch, seq):
    T = q.shape[0]
    bq = WINDOW
    per_step = ATT_BLOCKS_PER_STEP
    ns = seq // (bq * per_step)
    cur = lambda b, n: (b * ns + n, 0)
    prev = lambda b, n: (jnp.maximum((b * ns + n) * per_step - 1, 0), 0)
    return pl.pallas_call(
        _attn_kernel,
        out_shape=jax.ShapeDtypeStruct((T, ATT_WIDTH), BF16),
        grid=(batch, ns),
        in_specs=[pl.BlockSpec(memory_space=pltpu.SMEM),
                  pl.BlockSpec((bq * per_step, ATT_WIDTH), cur),
                  pl.BlockSpec((bq, KV_WIDTH), prev), pl.BlockSpec((bq * per_step, KV_WIDTH), cur),
                  pl.BlockSpec((bq, KV_WIDTH), prev), pl.BlockSpec((bq * per_step, KV_WIDTH), cur)],
        out_specs=pl.BlockSpec((bq * per_step, ATT_WIDTH), cur),
        compiler_params=_params("arbitrary", "arbitrary"),
    )(sinks, q, k, k, v, v)


def _rwkv_kernel(z_ref, w0_ref, wd_ref, a0_ref, wa_ref, wg_ref, kk_ref, ka_ref, rk_ref, lnw_ref, lnb_ref,
                 y_ref, s_ref):
    c = pl.program_id(1)

    @pl.when(c == 0)
    def _():
        s_ref[...] = jnp.zeros_like(s_ref)

    L = RWKV_CHUNK
    rows = z_ref.shape[0]
    n_chunks = rows // L
    C = RWKV_WIDTH
    N = RWKV_HEAD
    r = z_ref[:, 0:C]
    k = z_ref[:, C:2 * C]
    v = z_ref[:, 2 * C:3 * C]
    xw = z_ref[:, 3 * C:3 * C + DECAY_LORA]
    xa = z_ref[:, 3 * C + DECAY_LORA:3 * C + DECAY_LORA + AAA_LORA]
    xg = z_ref[:, 3 * C + DECAY_LORA + AAA_LORA:]

    wlin = w0_ref[...] + _bdot(jnp.tanh(xw), wd_ref[...])
    softplus = jnp.maximum(-wlin, 0.0) + jnp.log(1.0 + jnp.exp(-jnp.abs(wlin)))
    logdecay = -jnp.exp(-softplus - 0.5)
    a = _sigmoid(a0_ref[...] + _bdot(xa, wa_ref[...]))
    g = _bdot(_sigmoid(xg), wg_ref[...])
    kk = k * kk_ref[...]
    k2 = k * (1.0 + (a - 1.0) * ka_ref[...])

    hr = lax.broadcasted_iota(jnp.int32, (C, C), 0) // N
    hc = lax.broadcasted_iota(jnp.int32, (C, C), 1) // N
    head_ones = jnp.where(hr == hc, 1.0, 0.0).astype(BF16)
    head_sum = lambda t: jnp.dot(t.astype(BF16), head_ones, preferred_element_type=F32)

    kkn = kk / jnp.maximum(jnp.sqrt(head_sum(kk * kk)), 1e-12)

    assert L == N and 2 * N == LANES, "the pair-packed block-diagonal products need chunk == head size == 64"
    row = lax.broadcasted_iota(jnp.int32, (L, L), 0)
    col = lax.broadcasted_iota(jnp.int32, (L, L), 1)
    tri = jnp.where(row >= col, 1.0, 0.0).astype(BF16)
    ld_1 = logdecay.astype(BF16)
    rest = logdecay - ld_1.astype(F32)
    ld_2 = rest.astype(BF16)
    ld_3 = (rest - ld_2.astype(F32)).astype(BF16)
    cums = []
    for ci in range(n_chunks):
        rs = slice(ci * L, (ci + 1) * L)
        cums.append(jnp.dot(tri, ld_1[rs], preferred_element_type=F32)
                    + jnp.dot(tri, ld_2[rs], preferred_element_type=F32)
                    + jnp.dot(tri, ld_3[rs], preferred_element_type=F32))
    cum = jnp.concatenate(cums, axis=0) if n_chunks > 1 else cums[0]
    last = [cums[ci][L - 1:L, :] for ci in range(n_chunks)]
    cum_last = jnp.concatenate([jnp.broadcast_to(t, (L, C)) for t in last], axis=0) if n_chunks > 1 \
        else jnp.broadcast_to(last[0], (L, C))
    p_in = jnp.exp(cum)
    p_inv = jnp.exp(-cum)
    p_rem = jnp.exp(cum_last - cum)
    b = kkn * a
    at_f = (-kkn * jnp.exp(cum - logdecay)).astype(BF16)
    rt_f = r * p_in
    rt_b = rt_f.astype(BF16)
    bt_b = (b * p_inv).astype(BF16)
    kt_b = (k2 * p_inv).astype(BF16)
    br_b = (b * p_rem).astype(BF16)
    kr_b = (k2 * p_rem).astype(BF16)
    v_b = v.astype(BF16)

    dot = lambda x, y: jnp.dot(x.astype(BF16), y.astype(BF16), preferred_element_type=F32)
    dot_nt = lambda x, y: lax.dot_general(x.astype(BF16), y.astype(BF16), (((1,), (1,)), ((), ())),
                                          preferred_element_type=F32)
    dot_tn = lambda x, y: lax.dot_general(x.astype(BF16), y.astype(BF16), (((0,), (0,)), ((), ())),
                                          preferred_element_type=F32)

    n_pairs = C // LANES
    lane = lax.broadcasted_iota(jnp.int32, (L, LANES), 1)
    prow = lax.broadcasted_iota(jnp.int32, (L, LANES), 0)
    first = lane < N
    pcol = jnp.where(first, lane, lane - N)
    p_lower = prow > pcol
    p_lower_eq = prow >= pcol
    p_eye = jnp.where(prow == pcol, 1.0, 0.0)
    zero = jnp.zeros((), BF16)

    def bdiag(t):
        t = t.astype(BF16)
        return jnp.concatenate([jnp.where(first, t, zero), jnp.where(first, zero, t)], axis=0)

    cut = lambda t, ci, pj: t[ci * L:(ci + 1) * L, pj * LANES:(pj + 1) * LANES]
    same_head = (lax.broadcasted_iota(jnp.int32, (LANES, LANES), 0) < N) == \
                (lax.broadcasted_iota(jnp.int32, (LANES, LANES), 1) < N)
    rq, y0, gg, hh = {}, {}, {}, {}
    for g0 in range(0, n_chunks, RWKV_STAGE_GROUP):
        pieces = [(ci, pj) for ci in range(g0, min(g0 + RWKV_STAGE_GROUP, n_chunks)) for pj in range(n_pairs)]
        _rwkv_chunk_algebra(pieces, cut, bdiag, dot, dot_nt, dot_tn, L, N, at_f, rt_f, rt_b, bt_b, kt_b, br_b, kr_b, v_b,
                            p_lower, p_lower_eq, p_eye, same_head, first, rq, y0, gg, hh)

    ys = {}
    for pj in range(n_pairs):
        state = s_ref[pj]
        for ci in range(n_chunks):
            p = (ci, pj)
            sb = state.astype(BF16)
            ys[p] = y0[p] + dot_nt(rq[p], bdiag(sb))
            p_tot = p_in[(ci + 1) * L - 1:(ci + 1) * L, pj * LANES:(pj + 1) * LANES]
            state = state * p_tot + dot(sb, gg[p]) + hh[p]
        s_ref[pj] = state
    y = jnp.concatenate([jnp.concatenate([ys[(ci, pj)] for pj in range(n_pairs)], axis=1)
                         for ci in range(n_chunks)], axis=0)

    mu = head_sum(y) * (1.0 / N)
    dev = y - mu
    var = head_sum(dev * dev) * (1.0 / N)
    yn = dev * lax.rsqrt(var + RWKV_GN_EPS)
    bonus = head_sum(r * k2 * rk_ref[...]) * v
    y_ref[...] = ((yn * lnw_ref[...] + lnb_ref[...] + bonus) * g).astype(y_ref.dtype)


def _rwkv_chunk_algebra(pieces, cut, bdiag, dot, dot_nt, dot_tn, L, N, at_f, rt_f, rt_b, bt_b, kt_b, br_b, kr_b, v_b,
                        p_lower, p_lower_eq, p_eye, same_head, first, rq_out, y0_out, gg_out, hh_out):
    at = {p: cut(at_f, *p) for p in pieces}
    vv = {p: cut(v_b, *p) for p in pieces}
    m = {p: dot_nt(jnp.concatenate([at[p], cut(rt_b, *p)], axis=0),
                   jnp.concatenate([bdiag(cut(bt_b, *p)), bdiag(cut(kt_b, *p))], axis=0)) for p in pieces}
    m_ab = {p: jnp.where(p_lower, m[p][:L, :LANES], 0.0) for p in pieces}
    m_ak = {p: jnp.where(p_lower, m[p][:L, LANES:], 0.0).astype(BF16) for p in pieces}
    m_rb = {p: jnp.where(p_lower_eq, m[p][L:, :LANES], 0.0).astype(BF16) for p in pieces}
    m_rk = {p: jnp.where(p_lower_eq, m[p][L:, LANES:], 0.0).astype(BF16) for p in pieces}
    inv = {p: p_eye + m_ab[p] for p in pieces}
    pw = {p: m_ab[p].astype(BF16) for p in pieces}
    pw = {p: dot(pw[p], bdiag(pw[p])).astype(BF16) for p in pieces}
    for _ in range(max(2, (L - 1).bit_length()) - 2):
        both = {p: dot(jnp.concatenate([pw[p], inv[p].astype(BF16)], axis=0), bdiag(pw[p])) for p in pieces}
        pw = {p: both[p][:L].astype(BF16) for p in pieces}
        inv = {p: inv[p] + both[p][L:] for p in pieces}
    inv = {p: (inv[p] + dot(inv[p], bdiag(pw[p]))).astype(BF16) for p in pieces}
    mv = {p: dot(jnp.concatenate([m_ak[p], m_rk[p]], axis=0), bdiag(vv[p])) for p in pieces}
    wu = {p: dot(inv[p], jnp.concatenate([bdiag(at[p]), bdiag(mv[p][:L])], axis=1)) for p in pieces}
    w = {p: wu[p][:, :LANES].astype(BF16) for p in pieces}
    u0 = {p: wu[p][:, LANES:].astype(BF16) for p in pieces}
    ry = {p: dot(m_rb[p], jnp.concatenate([bdiag(w[p]), bdiag(u0[p])], axis=1)) for p in pieces}
    rq_out.update({p: (cut(rt_f, *p) + ry[p][:, :LANES]).astype(BF16) for p in pieces})
    y0_out.update({p: ry[p][:, LANES:] + mv[p][L:] for p in pieces})
    gg_out.update({p: jnp.where(same_head, dot_tn(w[p], cut(br_b, *p)), 0.0).astype(BF16) for p in pieces})
    hh_full = {p: dot_tn(jnp.concatenate([u0[p], vv[p]], axis=0),
                         jnp.concatenate([cut(br_b, *p), cut(kr_b, *p)], axis=0)) for p in pieces}
    hh_out.update({p: jnp.where(first, hh_full[p][:N], hh_full[p][N:]) for p in pieces})


def _rwkv(zr, w0, wd, a0, wa, wg, k_k, k_a, r_k, ln_w, ln_b, batch, seq):
    T = zr.shape[0]
    L = RWKV_CHUNK * RWKV_CHUNKS_PER_STEP
    nc = seq // L
    vec = lambda a: a.reshape(1, RWKV_WIDTH)
    full = lambda a: pl.BlockSpec(a.shape, lambda b, c: (0,) * a.ndim)
    args = (vec(w0), wd, vec(a0), wa, wg, vec(k_k), vec(k_a), vec(r_k), vec(ln_w), vec(ln_b))
    return pl.pallas_call(
        _rwkv_kernel,
        out_shape=jax.ShapeDtypeStruct((T, RWKV_WIDTH), BF16),
        grid=(batch, nc),
        in_specs=[pl.BlockSpec((L, SHIFT_COLS), lambda b, c: (b * nc + c, 0))] + [full(a) for a in args],
        out_specs=pl.BlockSpec((L, RWKV_WIDTH), lambda b, c: (b * nc + c, 0)),
        scratch_shapes=[pltpu.VMEM((RWKV_WIDTH // LANES, RWKV_HEAD, LANES), F32)],
        compiler_params=_params("arbitrary", "arbitrary"),
    )(zr, *args)


def _merge_kernel(x_ref, ya_ref, yr_ref, gate_ref, wba_ref, wbr_ref, wo_ref, lnm_ref, wr_ref, br_ref,
                  x1_ref, h2_ref, route_ref, cnt_ref, cnt_scr):
    i = pl.program_id(0)

    @pl.when(i == 0)
    def _():
        cnt_scr[...] = jnp.zeros_like(cnt_scr)

    tm = x_ref.shape[0]
    D = D_MODEL
    gates = gate_ref[...].astype(F32)
    ya = jnp.dot(ya_ref[...], wba_ref[...], preferred_element_type=F32)
    yr = jnp.dot(yr_ref[...], wbr_ref[...], preferred_element_type=F32)
    merged = gates[:, :D] * ya + gates[:, D:] * yr
    x1 = x_ref[...] + jnp.dot(merged.astype(BF16), wo_ref[...], preferred_element_type=F32)
    x1_ref[...] = x1
    h2 = _rms(x1, lnm_ref[...])
    _store_slabs(h2_ref, h2)

    h_hi = h2.astype(BF16)
    h_lo = (h2 - h_hi.astype(F32)).astype(BF16)
    parts = (jnp.dot(h_hi, wr_ref[...], preferred_element_type=F32)
             + jnp.dot(h_lo, wr_ref[...], preferred_element_type=F32))
    logits = parts[:, :LANES] + parts[:, LANES:] + br_ref[...]
    lane = lax.broadcasted_iota(jnp.int32, logits.shape, 1)
    big = jnp.int32(1 << 20)
    gl = jnp.where(lane < N_GROUPS, logits, NEG_INF)
    gmax = jnp.max(gl, axis=-1, keepdims=True)
    gidx = jnp.min(jnp.where(gl == gmax, lane, big), axis=-1, keepdims=True)
    gsum = jnp.sum(jnp.where(lane < N_GROUPS, jnp.exp(logits - gmax), 0.0), axis=-1, keepdims=True)
    g_w = 1.0 / gsum
    lo = N_GROUPS + EXPERTS_PER_GROUP * gidx
    el = jnp.where((lane >= lo) & (lane < lo + EXPERTS_PER_GROUP), logits, NEG_INF)
    m1 = jnp.max(el, axis=-1, keepdims=True)
    i1 = jnp.min(jnp.where(el == m1, lane, big), axis=-1, keepdims=True)
    el2 = jnp.where(lane == i1, NEG_INF, el)
    m2 = jnp.max(el2, axis=-1, keepdims=True)
    i2 = jnp.min(jnp.where(el2 == m2, lane, big), axis=-1, keepdims=True)
    d = jnp.exp(m2 - m1)
    w1 = g_w / (1.0 + d)
    w2 = g_w * d / (1.0 + d)
    e1 = i1 - N_GROUPS
    e2 = i2 - N_GROUPS

    hit1 = lane == e1
    hit2 = lane == e2
    onehot = jnp.where(hit1 | hit2, 1.0, 0.0).astype(BF16)
    r_i = lax.broadcasted_iota(jnp.int32, (tm, tm), 0)
    c_i = lax.broadcasted_iota(jnp.int32, (tm, tm), 1)
    before = jnp.dot(jnp.where(r_i > c_i, 1.0, 0.0).astype(BF16), onehot, preferred_element_type=F32)
    before = before + cnt_scr[...]
    rank1 = jnp.sum(jnp.where(hit1, before, 0.0), axis=-1, keepdims=True)
    rank2 = jnp.sum(jnp.where(hit2, before, 0.0), axis=-1, keepdims=True)
    cnt_scr[...] = cnt_scr[...] + jnp.sum(onehot.astype(F32), axis=0, keepdims=True)
    cnt_ref[...] = cnt_scr[...]

    route = jnp.where(lane == 0, e1.astype(F32), 0.0)
    route = jnp.where(lane == 1, e2.astype(F32), route)
    route = jnp.where(lane == 2, rank1, route)
    route = jnp.where(lane == 3, rank2, route)
    route = jnp.where(lane == 4, w1, route)
    route = jnp.where(lane == 5, w2, route)
    route_ref[...] = route


def _merge(x2, y_att, y_rwkv, gates, wba, wbr, wo, ln_moe, w_router, b_router):
    T, D = x2.shape
    tm = MERGE_TILE
    row = lambda n: pl.BlockSpec((tm, n), lambda i: (i, 0))
    full = lambda a: pl.BlockSpec(a.shape, lambda i: (0,) * a.ndim)
    ln_moe = ln_moe.reshape(1, D)
    return pl.pallas_call(
        _merge_kernel,
        out_shape=(jax.ShapeDtypeStruct((T, D), F32), jax.ShapeDtypeStruct((T * SLAB_ROWS, LANES), F32),
                   jax.ShapeDtypeStruct((T, LANES), F32), jax.ShapeDtypeStruct((1, LANES), F32)),
        grid=(T // tm,),
        in_specs=[row(D), row(ATT_WIDTH), row(RWKV_WIDTH), row(GATE_COLS),
                  full(wba), full(wbr), full(wo), full(ln_moe), full(w_router), full(b_router)],
        out_specs=(row(D), pl.BlockSpec((tm * SLAB_ROWS, LANES), lambda i: (i, 0)), row(LANES),
                   pl.BlockSpec((1, LANES), lambda i: (0, 0))),
        scratch_shapes=[pltpu.VMEM((1, LANES), F32)],
        compiler_params=_params("arbitrary"),
    )(x2, y_att, y_rwkv, gates, wba, wbr, wo, ln_moe, w_router, b_router)


def _dispatch_kernel(dest_ref, h_ref, xrows_hbm, row_sem, *, tile):
    base = pl.program_id(0) * tile

    def row_copy(j, slot):
        return pltpu.make_async_copy(_slab(h_ref, j), _slab(xrows_hbm, dest_ref[2 * (base + j) + slot]), row_sem)

    def issue(j, carry):
        row_copy(j, 0).start(priority=0)
        row_copy(j, 1).start(priority=1)
        return carry

    lax.fori_loop(0, tile, issue, 0, unroll=8)

    def drain(j, carry):
        row_copy(j, 0).wait()
        row_copy(j, 1).wait()
        return carry

    lax.fori_loop(0, tile, drain, 0, unroll=8)


def _dispatch(h2_slabs, dest):
    T = h2_slabs.shape[0] // SLAB_ROWS
    tile = min(DISPATCH_TILE, T)
    assert T % tile == 0
    return pl.pallas_call(
        functools.partial(_dispatch_kernel, tile=tile),
        out_shape=jax.ShapeDtypeStruct((2 * T * SLAB_ROWS, LANES), F32),
        grid_spec=pltpu.PrefetchScalarGridSpec(
            num_scalar_prefetch=1,
            grid=(T // tile,),
            in_specs=[pl.BlockSpec((tile * SLAB_ROWS, LANES), lambda i, d: (i, 0))],
            out_specs=pl.BlockSpec(memory_space=pl.ANY),
            scratch_shapes=[pltpu.SemaphoreType.DMA],
        ),
        compiler_params=_params("arbitrary"),
    )(dest, h2_slabs)


def _expert_kernel(item_e, item_b, item_lo, item_hi, x_ref, wg_ref, wu_ref, wd_ref, y_ref, wgu_bf, wd_bf):
    w = pl.program_id(0)
    lo = item_lo[w]
    hi = item_hi[w]

    @pl.when((w == 0) | (item_e[w] != item_e[jnp.maximum(w - 1, 0)]))
    def _():
        wgu_bf[:, :D_EXPERT] = wg_ref[0].astype(BF16)
        wgu_bf[:, D_EXPERT:] = wu_ref[0].astype(BF16)
        wd_bf[...] = wd_ref[0].astype(BF16)

    def run(keep_other_rows):
        sub = EXPERT_SUB_BLOCK
        for s in range(x_ref.shape[0] // SLAB_ROWS // sub):
            xb = _load_slabs(x_ref, sub, s * sub).astype(BF16)
            hgu = jnp.dot(xb, wgu_bf[...], preferred_element_type=F32)
            hg = hgu[:, :D_EXPERT]
            hid = hg * _sigmoid(hg) * hgu[:, D_EXPERT:]
            y = jnp.dot(hid.astype(BF16), wd_bf[...], preferred_element_type=F32)
            if keep_other_rows:
                rows = lax.broadcasted_iota(jnp.int32, y.shape, 0) + s * sub
                y = jnp.where((rows >= lo) & (rows < hi), y, _load_slabs(y_ref, sub, s * sub))
            _store_slabs(y_ref, y, s * sub)

    pl.when((hi > lo) & (lo == 0))(functools.partial(run, False))
    pl.when((hi > lo) & (lo > 0))(functools.partial(run, True))


def _experts(xrows, items, wg, wu, wd):
    bm = EXPERT_BLOCK
    D = D_MODEL
    n_items = items[0].shape[0]
    block = pl.BlockSpec((bm * SLAB_ROWS, LANES), lambda w, ie, ib, lo, hi: (ib[w], 0))
    return pl.pallas_call(
        _expert_kernel,
        out_shape=jax.ShapeDtypeStruct(xrows.shape, F32),
        grid_spec=pltpu.PrefetchScalarGridSpec(
            num_scalar_prefetch=4,
            grid=(n_items,),
            in_specs=[block,
                      pl.BlockSpec((1, D, D_EXPERT), lambda w, ie, ib, lo, hi: (ie[w], 0, 0)),
                      pl.BlockSpec((1, D, D_EXPERT), lambda w, ie, ib, lo, hi: (ie[w], 0, 0)),
                      pl.BlockSpec((1, D_EXPERT, D), lambda w, ie, ib, lo, hi: (ie[w], 0, 0))],
            out_specs=block,
            scratch_shapes=[pltpu.VMEM((D, 2 * D_EXPERT), BF16), pltpu.VMEM((D_EXPERT, D), BF16)],
        ),
        compiler_params=_params("arbitrary"),
    )(*items, xrows, wg, wu, wd)


def _work_items(counts, n_rows):
    bm = EXPERT_BLOCK
    nb = n_rows // bm
    n_items = nb + N_EXPERTS
    ends = jnp.cumsum(counts)
    starts = ends - counts
    first_blk = starts // bm
    last_blk = (ends - 1) // bm
    per_e = jnp.where(counts > 0, last_blk - first_blk + 1, 0)
    item_end = jnp.cumsum(per_e)
    total = item_end[-1]
    w = jnp.arange(n_items, dtype=jnp.int32)
    wc = jnp.minimum(w, total - 1)
    e = jnp.minimum(jnp.sum((item_end[None, :] <= wc[:, None]).astype(jnp.int32), axis=1), N_EXPERTS - 1)
    b = first_blk[e] + (wc - (item_end[e] - per_e[e]))
    lo = jnp.maximum(starts[e], b * bm) - b * bm
    hi = jnp.minimum(ends[e], (b + 1) * bm) - b * bm
    live = w < total
    lo = jnp.where(live, lo, 0)
    hi = jnp.where(live, hi, 0)
    return (e.astype(jnp.int32), b.astype(jnp.int32), lo.astype(jnp.int32), hi.astype(jnp.int32)), starts


def _final_kernel(dest_ref, yrows_hbm, x1_ref, route_ref, p_ref, lnp_ref, wpg_ref, wpp_ref, lnf_ref,
                  o_ref, rows_ref, row_sems):
    i = pl.program_id(0)
    n = pl.num_programs(0)
    tm = x1_ref.shape[0]

    def row_copy(tile, j, slot, par):
        return pltpu.make_async_copy(_slab(yrows_hbm, dest_ref[2 * (tile * tm + j) + slot]),
                                     _slab(rows_ref.at[par, slot], j), row_sems.at[par])

    def wait_tile(par):
        pltpu.make_async_copy(rows_ref.at[par], rows_ref.at[par], row_sems.at[par]).wait()

    @pl.when(i == 0)
    def _():
        def body(j, carry):
            row_copy(0, j, 0, 0).start(priority=0)
            row_copy(0, j, 1, 0).start(priority=1)
            return carry
        lax.fori_loop(0, tm, body, 0, unroll=8)

    def step(par):
        wait_tile(par)
        nxt = jnp.minimum(i + 1, n - 1)
        for j in range(tm):
            row_copy(nxt, j, 0, 1 - par).start(priority=0)
            row_copy(nxt, j, 1, 1 - par).start(priority=1)
        pp = jnp.dot(p_ref[...].astype(BF16), wpp_ref[...], preferred_element_type=F32)
        route = route_ref[...]
        x2 = (x1_ref[...] + route[:, 4:5] * _load_slabs(rows_ref.at[par, 0], tm)
              + route[:, 5:6] * _load_slabs(rows_ref.at[par, 1], tm))
        gate = _sigmoid(jnp.dot(_rms(x2, lnp_ref[...]).astype(BF16), wpg_ref[...], preferred_element_type=F32))
        x3 = x2 + gate * pp
        o_ref[...] = _rms(x3, lnf_ref[...])

        @pl.when(i == n - 1)
        def _():
            wait_tile(1 - par)

    for par in range(2):
        pl.when(i % 2 == par)(functools.partial(step, par))


def _final(dest, yrows, x1, route, p2, ln_ple, wpg, wpp, ln_final):
    T, D = x1.shape
    tm = GATHER_TILE
    row = lambda n: pl.BlockSpec((tm, n), lambda i, d: (i, 0))
    full = lambda a: pl.BlockSpec(a.shape, lambda i, d: (0,) * a.ndim)
    ln_ple = ln_ple.reshape(1, D)
    ln_final = ln_final.reshape(1, D)
    return pl.pallas_call(
        _final_kernel,
        out_shape=jax.ShapeDtypeStruct((T, D), F32),
        grid_spec=pltpu.PrefetchScalarGridSpec(
            num_scalar_prefetch=1,
            grid=(T // tm,),
            in_specs=[pl.BlockSpec(memory_space=pl.ANY),
                      row(D), row(LANES), row(PLE_DIM), full(ln_ple), full(wpg), full(wpp), full(ln_final)],
            out_specs=row(D),
            scratch_shapes=[pltpu.VMEM((2, 2, tm * SLAB_ROWS, LANES), F32), pltpu.SemaphoreType.DMA((2,))],
        ),
        compiler_params=_params("arbitrary"),
    )(dest, yrows, x1, route, p2, ln_ple, wpg, wpp, ln_final)


def kernel(x, p, positions, ln_mix, w_in, mu_shift, w0, w_decay_up, a0, w_aaa_up, w_gate_up, k_k, k_a, r_k, ln_x_w, ln_x_b, sinks, w_branch_att, w_branch_rwkv, w_out, ln_moe, w_group, b_group, w_expert, b_expert, w_gate_e, w_up_e, w_down_e, ln_ple, w_ple_gate, w_ple_proj, ln_final):
    B, S, D = x.shape
    T = B * S
    depth = w_in.shape[0]
    assert D == D_MODEL and S % ROW_TILE == 0 and S % (WINDOW * ATT_BLOCKS_PER_STEP) == 0 and S % (RWKV_CHUNK * RWKV_CHUNKS_PER_STEP) == 0
    assert T % GATHER_TILE == 0 and (2 * T) % EXPERT_BLOCK == 0
    cos, sin = _rope_tables(positions)
    x2 = x.reshape(T, D)
    out = None
    for i in range(depth):
        q, k, v, zr, gates = _inproj(x2, ln_mix[i], w_in[i].astype(BF16), mu_shift[i], cos, sin, S)
        y_att = _attention(q, k, v, sinks[i], B, S)
        y_rwkv = _rwkv(zr, w0[i], w_decay_up[i], a0[i], w_aaa_up[i], w_gate_up[i], k_k[i], k_a[i], r_k[i],
                       ln_x_w[i], ln_x_b[i], B, S)
        pad = LANES - N_GROUPS - N_EXPERTS
        w_router = jnp.concatenate([w_group[i], w_expert[i], jnp.zeros((D, pad), F32)], axis=1)
        w_router_hi = w_router.astype(BF16)
        w_router = jnp.concatenate([w_router_hi, (w_router - w_router_hi.astype(F32)).astype(BF16)], axis=1)
        b_router = jnp.concatenate([b_group[i], b_expert[i], jnp.zeros((pad,), F32)]).reshape(1, LANES)
        x1, h2, route, cnt = _merge(x2, y_att, y_rwkv, gates, w_branch_att[i].astype(BF16),
                                    w_branch_rwkv[i].astype(BF16), w_out[i].astype(BF16), ln_moe[i],
                                    w_router, b_router)
        counts = cnt[0, :N_EXPERTS].astype(jnp.int32)
        items, starts = _work_items(counts, 2 * T)
        expert = route[:, 0:2].astype(jnp.int32)
        first_row = jnp.sum(jnp.where(expert[..., None] == jnp.arange(N_EXPERTS, dtype=jnp.int32), starts, 0), axis=-1)
        dest = (first_row + route[:, 2:4].astype(jnp.int32)).reshape(-1)
        xrows = _dispatch(h2, dest)
        yrows = _experts(xrows, items, w_gate_e[i], w_up_e[i], w_down_e[i])
        last = i == depth - 1
        assert last, "the final-norm kernel closes the only layer"
        out = _final(dest, yrows, x1, route, p[i].reshape(T, PLE_DIM), ln_ple[i], w_ple_gate[i].astype(BF16),
                     w_ple_proj[i].astype(BF16), ln_final)
    return out.reshape(B, S, D)
```

```python
import functools

import jax
import jax.numpy as jnp
from jax import lax
from jax.experimental import pallas as pl
from jax.experimental.pallas import tpu as pltpu

F32 = jnp.float32
BF16 = jnp.bfloat16
HIGHEST = lax.Precision.HIGHEST

D_MODEL = 1024
PLE_DIM = 256
ATT_HEADS = 8
ATT_KV_HEADS = 2
HEAD_DIM = 64
ATT_WIDTH = ATT_HEADS * HEAD_DIM
KV_WIDTH = ATT_KV_HEADS * HEAD_DIM
WINDOW = 128
ROPE_THETA = 10000.0
RWKV_HEADS = 8
RWKV_HEAD = 64
RWKV_WIDTH = RWKV_HEADS * RWKV_HEAD
DECAY_LORA = 64
AAA_LORA = 64
GATE_LORA = 128
RWKV_GN_EPS = 64e-5
ATT_COLS = ATT_WIDTH + 2 * KV_WIDTH
SHIFT_COLS = 3 * RWKV_WIDTH + DECAY_LORA + AAA_LORA + GATE_LORA
GATE_COLS = 2 * D_MODEL
N_GROUPS = 4
EXPERTS_PER_GROUP = 8
N_EXPERTS = N_GROUPS * EXPERTS_PER_GROUP
D_EXPERT = 512
NORM_EPS = 1e-6
NEG_INF = -1e30

LANES = 128
SLAB_ROWS = D_MODEL // LANES
VMEM_LIMIT = 56 * 1024 * 1024

ROW_TILE = 512
MERGE_TILE = 512
ATT_BLOCKS_PER_STEP = 4
RWKV_CHUNK = 64
RWKV_CHUNKS_PER_STEP = 8
RWKV_STAGE_GROUP = 4
EXPERT_BLOCK = 1024
EXPERT_SUB_BLOCK = 1024
GATHER_TILE = 256
DISPATCH_TILE = 512


def _params(*sem):
    return pltpu.CompilerParams(dimension_semantics=sem, vmem_limit_bytes=VMEM_LIMIT)


def _bdot(a, b):
    return jnp.dot(a.astype(BF16), b.astype(BF16), preferred_element_type=F32)


def _bdot_nt(a, b):
    return lax.dot_general(a.astype(BF16), b.astype(BF16), (((1,), (1,)), ((), ())),
                           preferred_element_type=F32)


def _fdot(a, b):
    return jnp.dot(a, b, preferred_element_type=F32, precision=HIGHEST)


def _rms(x, g):
    return x * lax.rsqrt(jnp.mean(x * x, axis=-1, keepdims=True) + NORM_EPS) * g


def _sigmoid(x):
    return 1.0 / (1.0 + jnp.exp(-x))


def _store_slabs(ref, val, first=0):
    m = val.shape[0]
    for j in range(SLAB_ROWS):
        ref[pl.ds(first * SLAB_ROWS + j, m, stride=SLAB_ROWS), :] = val[:, j * LANES:(j + 1) * LANES]


def _load_slabs(ref, m, first=0):
    return jnp.concatenate([ref[pl.ds(first * SLAB_ROWS + j, m, stride=SLAB_ROWS), :] for j in range(SLAB_ROWS)],
                           axis=1)


def _slab(ref, index):
    return ref.at[pl.ds(pl.multiple_of(index * SLAB_ROWS, SLAB_ROWS), SLAB_ROWS)]


def _rope_table_kernel(pos_ref, invf_ref, cos_ref, sin_ref):
    half = HEAD_DIM // 2
    per_row = LANES // half
    rows = pos_ref.shape[0]
    ang = pos_ref[...].astype(F32) * invf_ref[...]
    group = lax.broadcasted_iota(jnp.int32, ang.shape, 1) // half
    sign = jnp.where(group % 2 == 0, -1.0, 1.0)
    for table, out_ref, scale in ((jnp.cos(ang), cos_ref, None), (jnp.sin(ang), sin_ref, sign)):
        rolled = [table] + [pltpu.roll(table, half * j, 1) for j in range(1, per_row)]
        for m in range(per_row):
            out = rolled[(per_row - 1 - m) % per_row]
            for g in range(per_row - 1):
                out = jnp.where(group == g, rolled[(g - m) % per_row], out)
            out_ref[pl.ds(m, rows, stride=per_row), :] = out if scale is None else out * scale


def _rope_tables(positions):
    T = positions.size
    half = HEAD_DIM // 2
    per_row = LANES // half
    inv_freq = ROPE_THETA ** (-jnp.arange(half, dtype=F32) / half)
    invf = jnp.tile(inv_freq, per_row).reshape(1, LANES)
    pos = jnp.repeat(positions.reshape(T // per_row, per_row), half, axis=1)
    rows = T // per_row
    tr = min(rows, 1024)
    return pl.pallas_call(
        _rope_table_kernel,
        out_shape=(jax.ShapeDtypeStruct((T, LANES), F32),) * 2,
        grid=(rows // tr,),
        in_specs=[pl.BlockSpec((tr, LANES), lambda i: (i, 0)),
                  pl.BlockSpec((1, LANES), lambda i: (0, 0))],
        out_specs=(pl.BlockSpec((tr * per_row, LANES), lambda i: (i, 0)),) * 2,
        compiler_params=_params("arbitrary"),
    )(pos, invf)


def _rope(t, cos, sin):
    n = t.shape[1]
    reps = n // LANES
    c = jnp.tile(cos, (1, reps)) if reps > 1 else cos
    s = jnp.tile(sin, (1, reps)) if reps > 1 else sin
    lane = lax.broadcasted_iota(jnp.int32, t.shape, 1)
    first_half = (lane % HEAD_DIM) < (HEAD_DIM // 2)
    partner = jnp.where(first_half, pltpu.roll(t, n - HEAD_DIM // 2, 1), pltpu.roll(t, HEAD_DIM // 2, 1))
    return t * c + partner * s


def _inproj_kernel(x_ref, g_ref, w_ref, mu_ref, cos_ref, sin_ref,
                   q_ref, k_ref, v_ref, zr_ref, gate_ref, carry_ref, *, tiles_per_seq):
    i = pl.program_id(0)
    tm = x_ref.shape[0]
    h = _rms(x_ref[...], g_ref[...]).astype(BF16)
    cos = cos_ref[...]
    sin = sin_ref[...]

    za = jnp.dot(h, w_ref[:, :ATT_COLS], preferred_element_type=F32)
    q_ref[...] = (_rope(za[:, :ATT_WIDTH], cos, sin) * (HEAD_DIM ** -0.5)).astype(BF16)

    def per_head_doubled(t):
        lane = lax.broadcasted_iota(jnp.int32, t.shape, 1)
        swapped = pltpu.roll(t, HEAD_DIM, 1)
        low = lane < HEAD_DIM
        return jnp.concatenate([jnp.where(low, t, swapped), jnp.where(low, swapped, t)], axis=1).astype(BF16)

    k_ref[...] = per_head_doubled(_rope(za[:, ATT_WIDTH:ATT_WIDTH + KV_WIDTH], cos, sin))
    v_ref[...] = per_head_doubled(za[:, ATT_WIDTH + KV_WIDTH:])

    zs = jnp.dot(h, w_ref[:, ATT_COLS:ATT_COLS + SHIFT_COLS], preferred_element_type=F32)
    row = lax.broadcasted_iota(jnp.int32, zs.shape, 0)
    seq_start = (i % tiles_per_seq) == 0
    before = jnp.where(seq_start, 0.0, carry_ref[0:1, :])
    prev = jnp.where(row == 0, before, pltpu.roll(zs, 1, 0))
    carry_ref[0:1, :] = zs[tm - 1:tm, :]
    zr_ref[...] = zs + (prev - zs) * mu_ref[...]

    zg = jnp.dot(h, w_ref[:, ATT_COLS + SHIFT_COLS:], preferred_element_type=F32)
    gate_ref[...] = _sigmoid(zg).astype(BF16)


def _inproj(x2, ln, w_in, mu, cos, sin, seq):
    T, D = x2.shape
    tm = ROW_TILE
    row = lambda n: pl.BlockSpec((tm, n), lambda i: (i, 0))
    full = lambda a: pl.BlockSpec(a.shape, lambda i: (0,) * a.ndim)
    resident = lambda a: pl.BlockSpec(a.shape, lambda i: (0,) * a.ndim, pipeline_mode=pl.Buffered(1))
    ln = ln.reshape(1, D)
    mu = mu.reshape(1, SHIFT_COLS)
    return pl.pallas_call(
        functools.partial(_inproj_kernel, tiles_per_seq=seq // tm),
        out_shape=(jax.ShapeDtypeStruct((T, ATT_WIDTH), BF16),
                   jax.ShapeDtypeStruct((T, 2 * KV_WIDTH), BF16),
                   jax.ShapeDtypeStruct((T, 2 * KV_WIDTH), BF16),
                   jax.ShapeDtypeStruct((T, SHIFT_COLS), F32),
                   jax.ShapeDtypeStruct((T, GATE_COLS), BF16)),
        grid=(T // tm,),
        in_specs=[row(D), full(ln), resident(w_in), full(mu), row(LANES), row(LANES)],
        out_specs=(row(ATT_WIDTH), row(2 * KV_WIDTH), row(2 * KV_WIDTH), row(SHIFT_COLS), row(GATE_COLS)),
        scratch_shapes=[pltpu.VMEM((8, SHIFT_COLS), F32)],
        compiler_params=_params("arbitrary"),
    )(x2, ln, w_in, mu, cos, sin)


def _attn_kernel(sink_ref, q_ref, kp_ref, kc_ref, vp_ref, vc_ref, o_ref):
    n = pl.program_id(1)
    bq = WINDOW
    grp = ATT_HEADS // ATT_KV_HEADS
    gw = grp * HEAD_DIM
    kall = jnp.concatenate([kp_ref[...], kc_ref[...]], axis=0)
    vall = jnp.concatenate([vp_ref[...], vc_ref[...]], axis=0)
    qi = lax.broadcasted_iota(jnp.int32, (grp * bq, 2 * bq), 0) % bq
    si = lax.broadcasted_iota(jnp.int32, (grp * bq, 2 * bq), 1)
    diff = qi + bq - si
    band = (diff >= 0) & (diff < WINDOW)
    head_row = lax.broadcasted_iota(jnp.int32, (grp * bq, 1), 0) // bq
    lane_head = lax.broadcasted_iota(jnp.int32, (bq, gw), 1) // HEAD_DIM
    zero = jnp.zeros((), BF16)
    for blk in range(q_ref.shape[0] // bq):
        valid = band & ((si >= bq) | (n > 0)) if blk == 0 else band
        outs = []
        for j in range(ATT_KV_HEADS):
            qg = q_ref[blk * bq:(blk + 1) * bq, j * gw:(j + 1) * gw]
            lhs = jnp.concatenate([jnp.where(lane_head == g, qg, zero) for g in range(grp)], axis=0)
            k_half = kall[blk * bq:(blk + 2) * bq, j * 2 * HEAD_DIM:(j + 1) * 2 * HEAD_DIM]
            v_half = vall[blk * bq:(blk + 2) * bq, j * 2 * HEAD_DIM:(j + 1) * 2 * HEAD_DIM]
            k_rep = jnp.concatenate([k_half, k_half], axis=1)
            v_rep = jnp.concatenate([v_half, v_half], axis=1)
            s = lax.dot_general(lhs, k_rep, (((1,), (1,)), ((), ())), preferred_element_type=F32)
            s = jnp.where(valid, s, NEG_INF)
            sink = jnp.zeros((grp * bq, 1), F32)
            for g in range(grp):
                sink = jnp.where(head_row == g, sink_ref[j * grp + g], sink)
            m = jnp.maximum(jnp.max(s, axis=-1, keepdims=True), sink)
            e = jnp.exp(s - m)
            denom = jnp.sum(e, axis=-1, keepdims=True) + jnp.exp(sink - m)
            pv = jnp.dot(e.astype(BF16), v_rep, preferred_element_type=F32) / denom
            out = pv[:bq]
            for g in range(1, grp):
                out = jnp.where(lane_head == g, pv[g * bq:(g + 1) * bq], out)
            outs.append(out)
        o_ref[blk * bq:(blk + 1) * bq, :] = jnp.concatenate(outs, axis=1).astype(o_ref.dtype)


def _attention(q, k, v, sinks, batch, seq):
    T = q.shape[0]
    bq = WINDOW
    per_step = ATT_BLOCKS_PER_STEP
    ns = seq // (bq * per_step)
    cur = lambda b, n: (b * ns + n, 0)
    prev = lambda b, n: (jnp.maximum((b * ns + n) * per_step - 1, 0), 0)
    return pl.pallas_call(
        _attn_kernel,
        out_shape=jax.ShapeDtypeStruct((T, ATT_WIDTH), BF16),
        grid=(batch, ns),
        in_specs=[pl.BlockSpec(memory_space=pltpu.SMEM),
                  pl.BlockSpec((bq * per_step, ATT_WIDTH), cur),
                  pl.BlockSpec((bq, 2 * KV_WIDTH), prev), pl.BlockSpec((bq * per_step, 2 * KV_WIDTH), cur),
                  pl.BlockSpec((bq, 2 * KV_WIDTH), prev), pl.BlockSpec((bq * per_step, 2 * KV_WIDTH), cur)],
        out_specs=pl.BlockSpec((bq * per_step, ATT_WIDTH), cur),
        compiler_params=_params("arbitrary", "arbitrary"),
    )(sinks, q, k, k, v, v)


def _rwkv_kernel(z_ref, w0_ref, wd_ref, a0_ref, wa_ref, wg_ref, kk_ref, ka_ref, rk_ref, lnw_ref, lnb_ref,
                 y_ref, s_ref):
    c = pl.program_id(1)

    @pl.when(c == 0)
    def _():
        s_ref[...] = jnp.zeros_like(s_ref)

    L = RWKV_CHUNK
    rows = z_ref.shape[0]
    n_chunks = rows // L
    C = RWKV_WIDTH
    N = RWKV_HEAD
    r = z_ref[:, 0:C]
    k = z_ref[:, C:2 * C]
    v = z_ref[:, 2 * C:3 * C]
    xw = z_ref[:, 3 * C:3 * C + DECAY_LORA]
    xa = z_ref[:, 3 * C + DECAY_LORA:3 * C + DECAY_LORA + AAA_LORA]
    xg = z_ref[:, 3 * C + DECAY_LORA + AAA_LORA:]

    wlin = w0_ref[...] + _bdot(jnp.tanh(xw), wd_ref[...])
    softplus = jnp.maximum(-wlin, 0.0) + jnp.log(1.0 + jnp.exp(-jnp.abs(wlin)))
    logdecay = -jnp.exp(-softplus - 0.5)
    a = _sigmoid(a0_ref[...] + _bdot(xa, wa_ref[...]))
    g = _bdot(_sigmoid(xg), wg_ref[...])
    kk = k * kk_ref[...]
    k2 = k * (1.0 + (a - 1.0) * ka_ref[...])

    hr = lax.broadcasted_iota(jnp.int32, (C, C), 0) // N
    hc = lax.broadcasted_iota(jnp.int32, (C, C), 1) // N
    head_ones = jnp.where(hr == hc, 1.0, 0.0).astype(BF16)
    head_sum = lambda t: jnp.dot(t.astype(BF16), head_ones, preferred_element_type=F32)

    kkn = kk / jnp.maximum(jnp.sqrt(head_sum(kk * kk)), 1e-12)

    assert L == N and 2 * N == LANES, "the pair-packed block-diagonal products need chunk == head size == 64"
    row = lax.broadcasted_iota(jnp.int32, (L, L), 0)
    col = lax.broadcasted_iota(jnp.int32, (L, L), 1)
    tri = jnp.where(row >= col, 1.0, 0.0).astype(BF16)
    ld_1 = logdecay.astype(BF16)
    rest = logdecay - ld_1.astype(F32)
    ld_2 = rest.astype(BF16)
    ld_3 = (rest - ld_2.astype(F32)).astype(BF16)
    cums = []
    for ci in range(n_chunks):
        rs = slice(ci * L, (ci + 1) * L)
        cums.append(jnp.dot(tri, ld_1[rs], preferred_element_type=F32)
                    + jnp.dot(tri, ld_2[rs], preferred_element_type=F32)
                    + jnp.dot(tri, ld_3[rs], preferred_element_type=F32))
    cum = jnp.concatenate(cums, axis=0) if n_chunks > 1 else cums[0]
    last = [cums[ci][L - 1:L, :] for ci in range(n_chunks)]
    cum_last = jnp.concatenate([jnp.broadcast_to(t, (L, C)) for t in last], axis=0) if n_chunks > 1 \
        else jnp.broadcast_to(last[0], (L, C))
    p_in = jnp.exp(cum)
    p_inv = jnp.exp(-cum)
    p_rem = jnp.exp(cum_last - cum)
    b = kkn * a
    at_f = (-kkn * jnp.exp(cum - logdecay)).astype(BF16)
    rt_f = r * p_in
    rt_b = rt_f.astype(BF16)
    bt_b = (b * p_inv).astype(BF16)
    kt_b = (k2 * p_inv).astype(BF16)
    br_b = (b * p_rem).astype(BF16)
    kr_b = (k2 * p_rem).astype(BF16)
    v_b = v.astype(BF16)

    dot = lambda x, y: jnp.dot(x.astype(BF16), y.astype(BF16), preferred_element_type=F32)
    dot_nt = lambda x, y: lax.dot_general(x.astype(BF16), y.astype(BF16), (((1,), (1,)), ((), ())),
                                          preferred_element_type=F32)
    dot_tn = lambda x, y: lax.dot_general(x.astype(BF16), y.astype(BF16), (((0,), (0,)), ((), ())),
                                          preferred_element_type=F32)

    n_pairs = C // LANES
    lane = lax.broadcasted_iota(jnp.int32, (L, LANES), 1)
    prow = lax.broadcasted_iota(jnp.int32, (L, LANES), 0)
    first = lane < N
    pcol = jnp.where(first, lane, lane - N)
    p_lower = prow > pcol
    p_lower_eq = prow >= pcol
    p_eye = jnp.where(prow == pcol, 1.0, 0.0)
    zero = jnp.zeros((), BF16)

    def bdiag(t):
        t = t.astype(BF16)
        return jnp.concatenate([jnp.where(first, t, zero), jnp.where(first, zero, t)], axis=0)

    cut = lambda t, ci, pj: t[ci * L:(ci + 1) * L, pj * LANES:(pj + 1) * LANES]
    same_head = (lax.broadcasted_iota(jnp.int32, (LANES, LANES), 0) < N) == \
                (lax.broadcasted_iota(jnp.int32, (LANES, LANES), 1) < N)
    rq, y0, gg, hh = {}, {}, {}, {}
    for g0 in range(0, n_chunks, RWKV_STAGE_GROUP):
        pieces = [(ci, pj) for ci in range(g0, min(g0 + RWKV_STAGE_GROUP, n_chunks)) for pj in range(n_pairs)]
        _rwkv_chunk_algebra(pieces, cut, bdiag, dot, dot_nt, dot_tn, L, N, at_f, rt_f, rt_b, bt_b, kt_b, br_b, kr_b, v_b,
                            p_lower, p_lower_eq, p_eye, same_head, first, rq, y0, gg, hh)

    ys = {}
    for pj in range(n_pairs):
        state = s_ref[pj]
        for ci in range(n_chunks):
            p = (ci, pj)
            sb = state.astype(BF16)
            ys[p] = y0[p] + dot_nt(rq[p], bdiag(sb))
            p_tot = p_in[(ci + 1) * L - 1:(ci + 1) * L, pj * LANES:(pj + 1) * LANES]
            state = state * p_tot + dot(sb, gg[p]) + hh[p]
        s_ref[pj] = state
    y = jnp.concatenate([jnp.concatenate([ys[(ci, pj)] for pj in range(n_pairs)], axis=1)
                         for ci in range(n_chunks)], axis=0)

    mu = head_sum(y) * (1.0 / N)
    dev = y - mu
    var = head_sum(dev * dev) * (1.0 / N)
    yn = dev * lax.rsqrt(var + RWKV_GN_EPS)
    bonus = head_sum(r * k2 * rk_ref[...]) * v
    y_ref[...] = ((yn * lnw_ref[...] + lnb_ref[...] + bonus) * g).astype(y_ref.dtype)


def _rwkv_chunk_algebra(pieces, cut, bdiag, dot, dot_nt, dot_tn, L, N, at_f, rt_f, rt_b, bt_b, kt_b, br_b, kr_b, v_b,
                        p_lower, p_lower_eq, p_eye, same_head, first, rq_out, y0_out, gg_out, hh_out):
    at = {p: cut(at_f, *p) for p in pieces}
    vv = {p: cut(v_b, *p) for p in pieces}
    m = {p: dot_nt(jnp.concatenate([at[p], cut(rt_b, *p)], axis=0),
                   jnp.concatenate([bdiag(cut(bt_b, *p)), bdiag(cut(kt_b, *p))], axis=0)) for p in pieces}
    m_ab = {p: jnp.where(p_lower, m[p][:L, :LANES], 0.0) for p in pieces}
    m_ak = {p: jnp.where(p_lower, m[p][:L, LANES:], 0.0).astype(BF16) for p in pieces}
    m_rb = {p: jnp.where(p_lower_eq, m[p][L:, :LANES], 0.0).astype(BF16) for p in pieces}
    m_rk = {p: jnp.where(p_lower_eq, m[p][L:, LANES:], 0.0).astype(BF16) for p in pieces}
    inv = {p: p_eye + m_ab[p] for p in pieces}
    pw = {p: m_ab[p].astype(BF16) for p in pieces}
    pw = {p: dot(pw[p], bdiag(pw[p])).astype(BF16) for p in pieces}
    for _ in range(max(2, (L - 1).bit_length()) - 2):
        both = {p: dot(jnp.concatenate([pw[p], inv[p].astype(BF16)], axis=0), bdiag(pw[p])) for p in pieces}
        pw = {p: both[p][:L].astype(BF16) for p in pieces}
        inv = {p: inv[p] + both[p][L:] for p in pieces}
    inv = {p: (inv[p] + dot(inv[p], bdiag(pw[p]))).astype(BF16) for p in pieces}
    mv = {p: dot(jnp.concatenate([m_ak[p], m_rk[p]], axis=0), bdiag(vv[p])) for p in pieces}
    wu = {p: dot(inv[p], jnp.concatenate([bdiag(at[p]), bdiag(mv[p][:L])], axis=1)) for p in pieces}
    w = {p: wu[p][:, :LANES].astype(BF16) for p in pieces}
    u0 = {p: wu[p][:, LANES:].astype(BF16) for p in pieces}
    ry = {p: dot(m_rb[p], jnp.concatenate([bdiag(w[p]), bdiag(u0[p])], axis=1)) for p in pieces}
    rq_out.update({p: (cut(rt_f, *p) + ry[p][:, :LANES]).astype(BF16) for p in pieces})
    y0_out.update({p: ry[p][:, LANES:] + mv[p][L:] for p in pieces})
    gg_out.update({p: jnp.where(same_head, dot_tn(w[p], cut(br_b, *p)), 0.0).astype(BF16) for p in pieces})
    hh_full = {p: dot_tn(jnp.concatenate([u0[p], vv[p]], axis=0),
                         jnp.concatenate([cut(br_b, *p), cut(kr_b, *p)], axis=0)) for p in pieces}
    hh_out.update({p: jnp.where(first, hh_full[p][:N], hh_full[p][N:]) for p in pieces})


def _rwkv(zr, w0, wd, a0, wa, wg, k_k, k_a, r_k, ln_w, ln_b, batch, seq):
    T = zr.shape[0]
    L = RWKV_CHUNK * RWKV_CHUNKS_PER_STEP
    nc = seq // L
    vec = lambda a: a.reshape(1, RWKV_WIDTH)
    full = lambda a: pl.BlockSpec(a.shape, lambda b, c: (0,) * a.ndim)
    args = (vec(w0), wd, vec(a0), wa, wg, vec(k_k), vec(k_a), vec(r_k), vec(ln_w), vec(ln_b))
    return pl.pallas_call(
        _rwkv_kernel,
        out_shape=jax.ShapeDtypeStruct((T, RWKV_WIDTH), BF16),
        grid=(batch, nc),
        in_specs=[pl.BlockSpec((L, SHIFT_COLS), lambda b, c: (b * nc + c, 0))] + [full(a) for a in args],
        out_specs=pl.BlockSpec((L, RWKV_WIDTH), lambda b, c: (b * nc + c, 0)),
        scratch_shapes=[pltpu.VMEM((RWKV_WIDTH // LANES, RWKV_HEAD, LANES), F32)],
        compiler_params=_params("arbitrary", "arbitrary"),
    )(zr, *args)


def _merge_kernel(x_ref, ya_ref, yr_ref, gate_ref, wba_ref, wbr_ref, wo_ref, lnm_ref, wr_ref, br_ref,
                  x1_ref, h2_ref, route_ref, cnt_ref, cnt_scr):
    i = pl.program_id(0)

    @pl.when(i == 0)
    def _():
        cnt_scr[...] = jnp.zeros_like(cnt_scr)

    tm = x_ref.shape[0]
    D = D_MODEL
    gates = gate_ref[...].astype(F32)
    ya = jnp.dot(ya_ref[...], wba_ref[...], preferred_element_type=F32)
    yr = jnp.dot(yr_ref[...], wbr_ref[...], preferred_element_type=F32)
    merged = gates[:, :D] * ya + gates[:, D:] * yr
    x1 = x_ref[...] + jnp.dot(merged.astype(BF16), wo_ref[...], preferred_element_type=F32)
    x1_ref[...] = x1
    h2 = _rms(x1, lnm_ref[...])
    _store_slabs(h2_ref, h2)

    h_hi = h2.astype(BF16)
    h_lo = (h2 - h_hi.astype(F32)).astype(BF16)
    parts = (jnp.dot(h_hi, wr_ref[...], preferred_element_type=F32)
             + jnp.dot(h_lo, wr_ref[...], preferred_element_type=F32))
    logits = parts[:, :LANES] + parts[:, LANES:] + br_ref[...]
    lane = lax.broadcasted_iota(jnp.int32, logits.shape, 1)
    big = jnp.int32(1 << 20)
    gl = jnp.where(lane < N_GROUPS, logits, NEG_INF)
    gmax = jnp.max(gl, axis=-1, keepdims=True)
    gidx = jnp.min(jnp.where(gl == gmax, lane, big), axis=-1, keepdims=True)
    gsum = jnp.sum(jnp.where(lane < N_GROUPS, jnp.exp(logits - gmax), 0.0), axis=-1, keepdims=True)
    g_w = 1.0 / gsum
    lo = N_GROUPS + EXPERTS_PER_GROUP * gidx
    el = jnp.where((lane >= lo) & (lane < lo + EXPERTS_PER_GROUP), logits, NEG_INF)
    m1 = jnp.max(el, axis=-1, keepdims=True)
    i1 = jnp.min(jnp.where(el == m1, lane, big), axis=-1, keepdims=True)
    el2 = jnp.where(lane == i1, NEG_INF, el)
    m2 = jnp.max(el2, axis=-1, keepdims=True)
    i2 = jnp.min(jnp.where(el2 == m2, lane, big), axis=-1, keepdims=True)
    d = jnp.exp(m2 - m1)
    w1 = g_w / (1.0 + d)
    w2 = g_w * d / (1.0 + d)
    e1 = i1 - N_GROUPS
    e2 = i2 - N_GROUPS

    hit1 = lane == e1
    hit2 = lane == e2
    onehot = jnp.where(hit1 | hit2, 1.0, 0.0).astype(BF16)
    r_i = lax.broadcasted_iota(jnp.int32, (tm, tm), 0)
    c_i = lax.broadcasted_iota(jnp.int32, (tm, tm), 1)
    before = jnp.dot(jnp.where(r_i > c_i, 1.0, 0.0).astype(BF16), onehot, preferred_element_type=F32)
    before = before + cnt_scr[...]
    rank1 = jnp.sum(jnp.where(hit1, before, 0.0), axis=-1, keepdims=True)
    rank2 = jnp.sum(jnp.where(hit2, before, 0.0), axis=-1, keepdims=True)
    cnt_scr[...] = cnt_scr[...] + jnp.sum(onehot.astype(F32), axis=0, keepdims=True)
    cnt_ref[...] = cnt_scr[...]

    route = jnp.where(lane == 0, e1.astype(F32), 0.0)
    route = jnp.where(lane == 1, e2.astype(F32), route)
    route = jnp.where(lane == 2, rank1, route)
    route = jnp.where(lane == 3, rank2, route)
    route = jnp.where(lane == 4, w1, route)
    route = jnp.where(lane == 5, w2, route)
    route_ref[...] = route


def _merge(x2, y_att, y_rwkv, gates, wba, wbr, wo, ln_moe, w_router, b_router):
    T, D = x2.shape
    tm = MERGE_TILE
    row = lambda n: pl.BlockSpec((tm, n), lambda i: (i, 0))
    full = lambda a: pl.BlockSpec(a.shape, lambda i: (0,) * a.ndim)
    ln_moe = ln_moe.reshape(1, D)
    return pl.pallas_call(
        _merge_kernel,
        out_shape=(jax.ShapeDtypeStruct((T, D), F32), jax.ShapeDtypeStruct((T * SLAB_ROWS, LANES), F32),
                   jax.ShapeDtypeStruct((T, LANES), F32), jax.ShapeDtypeStruct((1, LANES), F32)),
        grid=(T // tm,),
        in_specs=[row(D), row(ATT_WIDTH), row(RWKV_WIDTH), row(GATE_COLS),
                  full(wba), full(wbr), full(wo), full(ln_moe), full(w_router), full(b_router)],
        out_specs=(row(D), pl.BlockSpec((tm * SLAB_ROWS, LANES), lambda i: (i, 0)), row(LANES),
                   pl.BlockSpec((1, LANES), lambda i: (0, 0))),
        scratch_shapes=[pltpu.VMEM((1, LANES), F32)],
        compiler_params=_params("arbitrary"),
    )(x2, y_att, y_rwkv, gates, wba, wbr, wo, ln_moe, w_router, b_router)


def _dispatch_kernel(dest_ref, h_ref, xrows_hbm, row_sem, *, tile):
    base = pl.program_id(0) * tile

    def row_copy(j, slot):
        return pltpu.make_async_copy(_slab(h_ref, j), _slab(xrows_hbm, dest_ref[2 * (base + j) + slot]), row_sem)

    def issue(j, carry):
        row_copy(j, 0).start(priority=0)
        row_copy(j, 1).start(priority=1)
        return carry

    lax.fori_loop(0, tile, issue, 0, unroll=8)

    def drain(j, carry):
        row_copy(j, 0).wait()
        row_copy(j, 1).wait()
        return carry

    lax.fori_loop(0, tile, drain, 0, unroll=8)


def _dispatch(h2_slabs, dest):
    T = h2_slabs.shape[0] // SLAB_ROWS
    tile = min(DISPATCH_TILE, T)
    assert T % tile == 0
    return pl.pallas_call(
        functools.partial(_dispatch_kernel, tile=tile),
        out_shape=jax.ShapeDtypeStruct((2 * T * SLAB_ROWS, LANES), F32),
        grid_spec=pltpu.PrefetchScalarGridSpec(
            num_scalar_prefetch=1,
            grid=(T // tile,),
            in_specs=[pl.BlockSpec((tile * SLAB_ROWS, LANES), lambda i, d: (i, 0))],
            out_specs=pl.BlockSpec(memory_space=pl.ANY),
            scratch_shapes=[pltpu.SemaphoreType.DMA],
        ),
        compiler_params=_params("arbitrary"),
    )(dest, h2_slabs)


def _expert_kernel(item_e, item_b, item_lo, item_hi, x_ref, wg_ref, wu_ref, wd_ref, y_ref, wgu_bf, wd_bf):
    w = pl.program_id(0)
    lo = item_lo[w]
    hi = item_hi[w]

    @pl.when((w == 0) | (item_e[w] != item_e[jnp.maximum(w - 1, 0)]))
    def _():
        wgu_bf[:, :D_EXPERT] = wg_ref[0].astype(BF16)
        wgu_bf[:, D_EXPERT:] = wu_ref[0].astype(BF16)
        wd_bf[...] = wd_ref[0].astype(BF16)

    def run(keep_other_rows):
        sub = EXPERT_SUB_BLOCK
        for s in range(x_ref.shape[0] // SLAB_ROWS // sub):
            xb = _load_slabs(x_ref, sub, s * sub).astype(BF16)
            hgu = jnp.dot(xb, wgu_bf[...], preferred_element_type=F32)
            hg = hgu[:, :D_EXPERT]
            hid = hg * _sigmoid(hg) * hgu[:, D_EXPERT:]
            y = jnp.dot(hid.astype(BF16), wd_bf[...], preferred_element_type=F32)
            if keep_other_rows:
                rows = lax.broadcasted_iota(jnp.int32, y.shape, 0) + s * sub
                y = jnp.where((rows >= lo) & (rows < hi), y, _load_slabs(y_ref, sub, s * sub))
            _store_slabs(y_ref, y, s * sub)

    pl.when((hi > lo) & (lo == 0))(functools.partial(run, False))
    pl.when((hi > lo) & (lo > 0))(functools.partial(run, True))


def _experts(xrows, items, wg, wu, wd):
    bm = EXPERT_BLOCK
    D = D_MODEL
    n_items = items[0].shape[0]
    block = pl.BlockSpec((bm * SLAB_ROWS, LANES), lambda w, ie, ib, lo, hi: (ib[w], 0))
    return pl.pallas_call(
        _expert_kernel,
        out_shape=jax.ShapeDtypeStruct(xrows.shape, F32),
        grid_spec=pltpu.PrefetchScalarGridSpec(
            num_scalar_prefetch=4,
            grid=(n_items,),
            in_specs=[block,
                      pl.BlockSpec((1, D, D_EXPERT), lambda w, ie, ib, lo, hi: (ie[w], 0, 0)),
                      pl.BlockSpec((1, D, D_EXPERT), lambda w, ie, ib, lo, hi: (ie[w], 0, 0)),
                      pl.BlockSpec((1, D_EXPERT, D), lambda w, ie, ib, lo, hi: (ie[w], 0, 0))],
            out_specs=block,
            scratch_shapes=[pltpu.VMEM((D, 2 * D_EXPERT), BF16), pltpu.VMEM((D_EXPERT, D), BF16)],
        ),
        compiler_params=_params("arbitrary"),
    )(*items, xrows, wg, wu, wd)


def _work_items(counts, n_rows):
    bm = EXPERT_BLOCK
    nb = n_rows // bm
    n_items = nb + N_EXPERTS
    ends = jnp.cumsum(counts)
    starts = ends - counts
    first_blk = starts // bm
    last_blk = (ends - 1) // bm
    per_e = jnp.where(counts > 0, last_blk - first_blk + 1, 0)
    item_end = jnp.cumsum(per_e)
    total = item_end[-1]
    w = jnp.arange(n_items, dtype=jnp.int32)
    wc = jnp.minimum(w, total - 1)
    e = jnp.minimum(jnp.sum((item_end[None, :] <= wc[:, None]).astype(jnp.int32), axis=1), N_EXPERTS - 1)
    b = first_blk[e] + (wc - (item_end[e] - per_e[e]))
    lo = jnp.maximum(starts[e], b * bm) - b * bm
    hi = jnp.minimum(ends[e], (b + 1) * bm) - b * bm
    live = w < total
    lo = jnp.where(live, lo, 0)
    hi = jnp.where(live, hi, 0)
    return (e.astype(jnp.int32), b.astype(jnp.int32), lo.astype(jnp.int32), hi.astype(jnp.int32)), starts


def _final_kernel(dest_ref, yrows_hbm, x1_ref, route_ref, p_ref, lnp_ref, wpg_ref, wpp_ref, lnf_ref,
                  o_ref, rows_ref, row_sems):
    i = pl.program_id(0)
    n = pl.num_programs(0)
    tm = x1_ref.shape[0]

    def row_copy(tile, j, slot, par):
        return pltpu.make_async_copy(_slab(yrows_hbm, dest_ref[2 * (tile * tm + j) + slot]),
                                     _slab(rows_ref.at[par, slot], j), row_sems.at[par])

    def wait_tile(par):
        pltpu.make_async_copy(rows_ref.at[par], rows_ref.at[par], row_sems.at[par]).wait()

    @pl.when(i == 0)
    def _():
        def body(j, carry):
            row_copy(0, j, 0, 0).start(priority=0)
            row_copy(0, j, 1, 0).start(priority=1)
            return carry
        lax.fori_loop(0, tm, body, 0, unroll=8)

    def step(par):
        wait_tile(par)
        nxt = jnp.minimum(i + 1, n - 1)
        for j in range(tm):
            row_copy(nxt, j, 0, 1 - par).start(priority=0)
            row_copy(nxt, j, 1, 1 - par).start(priority=1)
        pp = jnp.dot(p_ref[...].astype(BF16), wpp_ref[...], preferred_element_type=F32)
        route = route_ref[...]
        x2 = (x1_ref[...] + route[:, 4:5] * _load_slabs(rows_ref.at[par, 0], tm)
              + route[:, 5:6] * _load_slabs(rows_ref.at[par, 1], tm))
        gate = _sigmoid(jnp.dot(_rms(x2, lnp_ref[...]).astype(BF16), wpg_ref[...], preferred_element_type=F32))
        x3 = x2 + gate * pp
        o_ref[...] = _rms(x3, lnf_ref[...])

        @pl.when(i == n - 1)
        def _():
            wait_tile(1 - par)

    for par in range(2):
        pl.when(i % 2 == par)(functools.partial(step, par))


def _final(dest, yrows, x1, route, p2, ln_ple, wpg, wpp, ln_final):
    T, D = x1.shape
    tm = GATHER_TILE
    row = lambda n: pl.BlockSpec((tm, n), lambda i, d: (i, 0))
    full = lambda a: pl.BlockSpec(a.shape, lambda i, d: (0,) * a.ndim)
    ln_ple = ln_ple.reshape(1, D)
    ln_final = ln_final.reshape(1, D)
    return pl.pallas_call(
        _final_kernel,
        out_shape=jax.ShapeDtypeStruct((T, D), F32),
        grid_spec=pltpu.PrefetchScalarGridSpec(
            num_scalar_prefetch=1,
            grid=(T // tm,),
            in_specs=[pl.BlockSpec(memory_space=pl.ANY),
                      row(D), row(LANES), row(PLE_DIM), full(ln_ple), full(wpg), full(wpp), full(ln_final)],
            out_specs=row(D),
            scratch_shapes=[pltpu.VMEM((2, 2, tm * SLAB_ROWS, LANES), F32), pltpu.SemaphoreType.DMA((2,))],
        ),
        compiler_params=_params("arbitrary"),
    )(dest, yrows, x1, route, p2, ln_ple, wpg, wpp, ln_final)


def kernel(x, p, positions, ln_mix, w_in, mu_shift, w0, w_decay_up, a0, w_aaa_up, w_gate_up, k_k, k_a, r_k, ln_x_w, ln_x_b, sinks, w_branch_att, w_branch_rwkv, w_out, ln_moe, w_group, b_group, w_expert, b_expert, w_gate_e, w_up_e, w_down_e, ln_ple, w_ple_gate, w_ple_proj, ln_final):
    B, S, D = x.shape
    T = B * S
    depth = w_in.shape[0]
    assert D == D_MODEL and S % ROW_TILE == 0 and S % (WINDOW * ATT_BLOCKS_PER_STEP) == 0 and S % (RWKV_CHUNK * RWKV_CHUNKS_PER_STEP) == 0
    assert T % GATHER_TILE == 0 and (2 * T) % EXPERT_BLOCK == 0
    cos, sin = _rope_tables(positions)
    x2 = x.reshape(T, D)
    out = None
    for i in range(depth):
        q, k, v, zr, gates = _inproj(x2, ln_mix[i], w_in[i].astype(BF16), mu_shift[i], cos, sin, S)
        y_att = _attention(q, k, v, sinks[i], B, S)
        y_rwkv = _rwkv(zr, w0[i], w_decay_up[i], a0[i], w_aaa_up[i], w_gate_up[i], k_k[i], k_a[i], r_k[i],
                       ln_x_w[i], ln_x_b[i], B, S)
        pad = LANES - N_GROUPS - N_EXPERTS
        w_router = jnp.concatenate([w_group[i], w_expert[i], jnp.zeros((D, pad), F32)], axis=1)
        w_router_hi = w_router.astype(BF16)
        w_router = jnp.concatenate([w_router_hi, (w_router - w_router_hi.astype(F32)).astype(BF16)], axis=1)
        b_router = jnp.concatenate([b_group[i], b_expert[i], jnp.zeros((pad,), F32)]).reshape(1, LANES)
        x1, h2, route, cnt = _merge(x2, y_att, y_rwkv, gates, w_branch_att[i].astype(BF16),
                                    w_branch_rwkv[i].astype(BF16), w_out[i].astype(BF16), ln_moe[i],
                                    w_router, b_router)
        counts = cnt[0, :N_EXPERTS].astype(jnp.int32)
        items, starts = _work_items(counts, 2 * T)
        expert = route[:, 0:2].astype(jnp.int32)
        first_row = jnp.sum(jnp.where(expert[..., None] == jnp.arange(N_EXPERTS, dtype=jnp.int32), starts, 0), axis=-1)
        dest = (first_row + route[:, 2:4].astype(jnp.int32)).reshape(-1)
        xrows = _dispatch(h2, dest)
        yrows = _experts(xrows, items, w_gate_e[i], w_up_e[i], w_down_e[i])
        last = i == depth - 1
        assert last, "the final-norm kernel closes the only layer"
        out = _final(dest, yrows, x1, route, p[i].reshape(T, PLE_DIM), ln_ple[i], w_ple_gate[i].astype(BF16),
                     w_ple_proj[i].astype(BF16), ln_final)
    return out.reshape(B, S, D)
```

```python
import functools

import jax
import jax.numpy as jnp
from jax import lax
from jax.experimental import pallas as pl
from jax.experimental.pallas import tpu as pltpu

F32 = jnp.float32
BF16 = jnp.bfloat16
HIGHEST = lax.Precision.HIGHEST

D_MODEL = 1024
PLE_DIM = 256
ATT_HEADS = 8
ATT_KV_HEADS = 2
HEAD_DIM = 64
ATT_WIDTH = ATT_HEADS * HEAD_DIM
KV_WIDTH = ATT_KV_HEADS * HEAD_DIM
WINDOW = 128
ROPE_THETA = 10000.0
RWKV_HEADS = 8
RWKV_HEAD = 64
RWKV_WIDTH = RWKV_HEADS * RWKV_HEAD
DECAY_LORA = 64
AAA_LORA = 64
GATE_LORA = 128
RWKV_GN_EPS = 64e-5
ATT_COLS = ATT_WIDTH + 2 * KV_WIDTH
SHIFT_COLS = 3 * RWKV_WIDTH + DECAY_LORA + AAA_LORA + GATE_LORA
GATE_COLS = 2 * D_MODEL
N_GROUPS = 4
EXPERTS_PER_GROUP = 8
N_EXPERTS = N_GROUPS * EXPERTS_PER_GROUP
D_EXPERT = 512
NORM_EPS = 1e-6
NEG_INF = -1e30

LANES = 128
SLAB_ROWS = D_MODEL // LANES
VMEM_LIMIT = 56 * 1024 * 1024

ROW_TILE = 512
MERGE_TILE = 1024
MERGE_ROW_PARTS = 2
ATT_BLOCKS_PER_STEP = 4
RWKV_CHUNK = 64
RWKV_CHUNKS_PER_STEP = 8
RWKV_PACK = 2
RWKV_ROW_PARTS = 1
RWKV_STAGE_GROUP = 4
EXPERT_BLOCK = 1024
EXPERT_SUB_BLOCK = 1024
FINAL_ROW_PARTS = 1
GATHER_TILE = 256
DISPATCH_TILE = 512


def _params(*sem):
    return pltpu.CompilerParams(dimension_semantics=sem, vmem_limit_bytes=VMEM_LIMIT)


def _bdot(a, b):
    return jnp.dot(a.astype(BF16), b.astype(BF16), preferred_element_type=F32)


def _bdot_nt(a, b):
    return lax.dot_general(a.astype(BF16), b.astype(BF16), (((1,), (1,)), ((), ())),
                           preferred_element_type=F32)


def _fdot(a, b):
    return jnp.dot(a, b, preferred_element_type=F32, precision=HIGHEST)


def _rms(x, g):
    return x * lax.rsqrt(jnp.mean(x * x, axis=-1, keepdims=True) + NORM_EPS) * g


def _sigmoid(x):
    return 1.0 / (1.0 + jnp.exp(-x))


def _store_slabs(ref, val, first=0):
    m = val.shape[0]
    for j in range(SLAB_ROWS):
        ref[pl.ds(first * SLAB_ROWS + j, m, stride=SLAB_ROWS), :] = val[:, j * LANES:(j + 1) * LANES]


def _load_slabs(ref, m, first=0):
    return jnp.concatenate([ref[pl.ds(first * SLAB_ROWS + j, m, stride=SLAB_ROWS), :] for j in range(SLAB_ROWS)],
                           axis=1)


def _slab(ref, index):
    return ref.at[pl.ds(pl.multiple_of(index * SLAB_ROWS, SLAB_ROWS), SLAB_ROWS)]


def _rope_table_kernel(pos_ref, invf_ref, cos_ref, sin_ref):
    half = HEAD_DIM // 2
    per_row = LANES // half
    rows = pos_ref.shape[0]
    ang = pos_ref[...].astype(F32) * invf_ref[...]
    group = lax.broadcasted_iota(jnp.int32, ang.shape, 1) // half
    sign = jnp.where(group % 2 == 0, -1.0, 1.0)
    for table, out_ref, scale in ((jnp.cos(ang), cos_ref, None), (jnp.sin(ang), sin_ref, sign)):
        rolled = [table] + [pltpu.roll(table, half * j, 1) for j in range(1, per_row)]
        for m in range(per_row):
            out = rolled[(per_row - 1 - m) % per_row]
            for g in range(per_row - 1):
                out = jnp.where(group == g, rolled[(g - m) % per_row], out)
            out_ref[pl.ds(m, rows, stride=per_row), :] = out if scale is None else out * scale


def _rope_tables(positions):
    T = positions.size
    half = HEAD_DIM // 2
    per_row = LANES // half
    inv_freq = ROPE_THETA ** (-jnp.arange(half, dtype=F32) / half)
    invf = jnp.tile(inv_freq, per_row).reshape(1, LANES)
    pos = jnp.repeat(positions.reshape(T // per_row, per_row), half, axis=1)
    rows = T // per_row
    tr = min(rows, 1024)
    return pl.pallas_call(
        _rope_table_kernel,
        out_shape=(jax.ShapeDtypeStruct((T, LANES), F32),) * 2,
        grid=(rows // tr,),
        in_specs=[pl.BlockSpec((tr, LANES), lambda i: (i, 0)),
                  pl.BlockSpec((1, LANES), lambda i: (0, 0))],
        out_specs=(pl.BlockSpec((tr * per_row, LANES), lambda i: (i, 0)),) * 2,
        compiler_params=_params("arbitrary"),
    )(pos, invf)


def _rope(t, cos, sin):
    n = t.shape[1]
    reps = n // LANES
    c = jnp.tile(cos, (1, reps)) if reps > 1 else cos
    s = jnp.tile(sin, (1, reps)) if reps > 1 else sin
    lane = lax.broadcasted_iota(jnp.int32, t.shape, 1)
    first_half = (lane % HEAD_DIM) < (HEAD_DIM // 2)
    partner = jnp.where(first_half, pltpu.roll(t, n - HEAD_DIM // 2, 1), pltpu.roll(t, HEAD_DIM // 2, 1))
    return t * c + partner * s


def _inproj_kernel(x_ref, g_ref, w_ref, mu_ref, cos_ref, sin_ref,
                   q_ref, k_ref, v_ref, zr_ref, gate_ref, carry_ref, *, tiles_per_seq):
    i = pl.program_id(0)
    tm = x_ref.shape[0]
    h = _rms(x_ref[...], g_ref[...]).astype(BF16)
    cos = cos_ref[...]
    sin = sin_ref[...]

    za = jnp.dot(h, w_ref[:, :ATT_COLS], preferred_element_type=F32)
    q_ref[...] = (_rope(za[:, :ATT_WIDTH], cos, sin) * (HEAD_DIM ** -0.5)).astype(BF16)

    def per_head_doubled(t):
        lane = lax.broadcasted_iota(jnp.int32, t.shape, 1)
        swapped = pltpu.roll(t, HEAD_DIM, 1)
        low = lane < HEAD_DIM
        return jnp.concatenate([jnp.where(low, t, swapped), jnp.where(low, swapped, t)], axis=1).astype(BF16)

    k_ref[...] = per_head_doubled(_rope(za[:, ATT_WIDTH:ATT_WIDTH + KV_WIDTH], cos, sin))
    v_ref[...] = per_head_doubled(za[:, ATT_WIDTH + KV_WIDTH:])

    zs = jnp.dot(h, w_ref[:, ATT_COLS:ATT_COLS + SHIFT_COLS], preferred_element_type=F32)
    row = lax.broadcasted_iota(jnp.int32, zs.shape, 0)
    seq_start = (i % tiles_per_seq) == 0
    before = jnp.where(seq_start, 0.0, carry_ref[0:1, :])
    prev = jnp.where(row == 0, before, pltpu.roll(zs, 1, 0))
    carry_ref[0:1, :] = zs[tm - 1:tm, :]
    zr_ref[...] = zs + (prev - zs) * mu_ref[...]

    zg = jnp.dot(h, w_ref[:, ATT_COLS + SHIFT_COLS:], preferred_element_type=F32)
    gate_ref[...] = _sigmoid(zg).astype(BF16)


def _inproj(x2, ln, w_in, mu, cos, sin, seq):
    T, D = x2.shape
    tm = ROW_TILE
    row = lambda n: pl.BlockSpec((tm, n), lambda i: (i, 0))
    full = lambda a: pl.BlockSpec(a.shape, lambda i: (0,) * a.ndim)
    resident = lambda a: pl.BlockSpec(a.shape, lambda i: (0,) * a.ndim, pipeline_mode=pl.Buffered(1))
    ln = ln.reshape(1, D)
    mu = mu.reshape(1, SHIFT_COLS)
    return pl.pallas_call(
        functools.partial(_inproj_kernel, tiles_per_seq=seq // tm),
        out_shape=(jax.ShapeDtypeStruct((T, ATT_WIDTH), BF16),
                   jax.ShapeDtypeStruct((T, 2 * KV_WIDTH), BF16),
                   jax.ShapeDtypeStruct((T, 2 * KV_WIDTH), BF16),
                   jax.ShapeDtypeStruct((T, SHIFT_COLS), F32),
                   jax.ShapeDtypeStruct((T, GATE_COLS), BF16)),
        grid=(T // tm,),
        in_specs=[row(D), full(ln), resident(w_in), full(mu), row(LANES), row(LANES)],
        out_specs=(row(ATT_WIDTH), row(2 * KV_WIDTH), row(2 * KV_WIDTH), row(SHIFT_COLS), row(GATE_COLS)),
        scratch_shapes=[pltpu.VMEM((8, SHIFT_COLS), F32)],
        compiler_params=_params("arbitrary"),
    )(x2, ln, w_in, mu, cos, sin)


def _attn_kernel(sink_ref, q_ref, kp_ref, kc_ref, vp_ref, vc_ref, o_ref):
    n = pl.program_id(1)
    bq = WINDOW
    grp = ATT_HEADS // ATT_KV_HEADS
    gw = grp * HEAD_DIM
    kall = jnp.concatenate([kp_ref[...], kc_ref[...]], axis=0)
    vall = jnp.concatenate([vp_ref[...], vc_ref[...]], axis=0)
    qi = lax.broadcasted_iota(jnp.int32, (grp * bq, 2 * bq), 0) % bq
    si = lax.broadcasted_iota(jnp.int32, (grp * bq, 2 * bq), 1)
    diff = qi + bq - si
    band = (diff >= 0) & (diff < WINDOW)
    head_row = lax.broadcasted_iota(jnp.int32, (grp * bq, 1), 0) // bq
    lane_head = lax.broadcasted_iota(jnp.int32, (bq, gw), 1) // HEAD_DIM
    zero = jnp.zeros((), BF16)
    for blk in range(q_ref.shape[0] // bq):
        valid = band & ((si >= bq) | (n > 0)) if blk == 0 else band
        outs = []
        for j in range(ATT_KV_HEADS):
            qg = q_ref[blk * bq:(blk + 1) * bq, j * gw:(j + 1) * gw]
            lhs = jnp.concatenate([jnp.where(lane_head == g, qg, zero) for g in range(grp)], axis=0)
            k_half = kall[blk * bq:(blk + 2) * bq, j * 2 * HEAD_DIM:(j + 1) * 2 * HEAD_DIM]
            v_half = vall[blk * bq:(blk + 2) * bq, j * 2 * HEAD_DIM:(j + 1) * 2 * HEAD_DIM]
            k_rep = jnp.concatenate([k_half, k_half], axis=1)
            v_rep = jnp.concatenate([v_half, v_half], axis=1)
            s = lax.dot_general(lhs, k_rep, (((1,), (1,)), ((), ())), preferred_element_type=F32)
            s = jnp.where(valid, s, NEG_INF)
            sink = jnp.zeros((grp * bq, 1), F32)
            for g in range(grp):
                sink = jnp.where(head_row == g, sink_ref[j * grp + g], sink)
            m = jnp.maximum(jnp.max(s, axis=-1, keepdims=True), sink)
            e = jnp.exp(s - m)
            denom = jnp.sum(e, axis=-1, keepdims=True) + jnp.exp(sink - m)
            pv = jnp.dot(e.astype(BF16), v_rep, preferred_element_type=F32) / denom
            out = pv[:bq]
            for g in range(1, grp):
                out = jnp.where(lane_head == g, pv[g * bq:(g + 1) * bq], out)
            outs.append(out)
        o_ref[blk * bq:(blk + 1) * bq, :] = jnp.concatenate(outs, axis=1).astype(o_ref.dtype)


def _attention(q, k, v, sinks, batch, seq):
    T = q.shape[0]
    bq = WINDOW
    per_step = ATT_BLOCKS_PER_STEP
    ns = seq // (bq * per_step)
    cur = lambda b, n: (b * ns + n, 0)
    prev = lambda b, n: (jnp.maximum((b * ns + n) * per_step - 1, 0), 0)
    return pl.pallas_call(
        _attn_kernel,
        out_shape=jax.ShapeDtypeStruct((T, ATT_WIDTH), BF16),
        grid=(batch, ns),
        in_specs=[pl.BlockSpec(memory_space=pltpu.SMEM),
                  pl.BlockSpec((bq * per_step, ATT_WIDTH), cur),
                  pl.BlockSpec((bq, 2 * KV_WIDTH), prev), pl.BlockSpec((bq * per_step, 2 * KV_WIDTH), cur),
                  pl.BlockSpec((bq, 2 * KV_WIDTH), prev), pl.BlockSpec((bq * per_step, 2 * KV_WIDTH), cur)],
        out_specs=pl.BlockSpec((bq * per_step, ATT_WIDTH), cur),
        compiler_params=_params("arbitrary", "arbitrary"),
    )(sinks, q, k, k, v, v)


def _rwkv_kernel(z_ref, w0_ref, wd_ref, a0_ref, wa_ref, wg_ref, kk_ref, ka_ref, rk_ref, lnw_ref, lnb_ref,
                 y_ref, s_ref):
    c = pl.program_id(1)

    @pl.when(c == 0)
    def _():
        s_ref[...] = jnp.zeros_like(s_ref)

    part = z_ref.shape[0] // RWKV_ROW_PARTS
    states = [s_ref[pj] for pj in range(s_ref.shape[0])]
    for i in range(RWKV_ROW_PARTS):
        states = _rwkv_rows(z_ref.at[pl.ds(i * part, part)], w0_ref, wd_ref, a0_ref, wa_ref, wg_ref, kk_ref, ka_ref,
                            rk_ref, lnw_ref, lnb_ref, y_ref.at[pl.ds(i * part, part)], states)
    for pj, state in enumerate(states):
        s_ref[pj] = state


def _rwkv_rows(z_ref, w0_ref, wd_ref, a0_ref, wa_ref, wg_ref, kk_ref, ka_ref, rk_ref, lnw_ref, lnb_ref, y_ref, states):
    L = RWKV_CHUNK
    rows = z_ref.shape[0]
    n_chunks = rows // L
    C = RWKV_WIDTH
    N = RWKV_HEAD
    r = z_ref[:, 0:C]
    k = z_ref[:, C:2 * C]
    v = z_ref[:, 2 * C:3 * C]
    xw = z_ref[:, 3 * C:3 * C + DECAY_LORA]
    xa = z_ref[:, 3 * C + DECAY_LORA:3 * C + DECAY_LORA + AAA_LORA]
    xg = z_ref[:, 3 * C + DECAY_LORA + AAA_LORA:]

    wlin = w0_ref[...] + _bdot(jnp.tanh(xw), wd_ref[...])
    softplus = jnp.maximum(-wlin, 0.0) + jnp.log(1.0 + jnp.exp(-jnp.abs(wlin)))
    logdecay = -jnp.exp(-softplus - 0.5)
    a = _sigmoid(a0_ref[...] + _bdot(xa, wa_ref[...]))
    g = _bdot(_sigmoid(xg), wg_ref[...])
    kk = k * kk_ref[...]
    k2 = k * (1.0 + (a - 1.0) * ka_ref[...])

    hr = lax.broadcasted_iota(jnp.int32, (C, C), 0) // N
    hc = lax.broadcasted_iota(jnp.int32, (C, C), 1) // N
    head_ones = jnp.where(hr == hc, 1.0, 0.0).astype(BF16)
    head_sum = lambda t: jnp.dot(t.astype(BF16), head_ones, preferred_element_type=F32)

    kkn = kk / jnp.maximum(jnp.sqrt(head_sum(kk * kk)), 1e-12)

    assert L == N and (RWKV_PACK * N) % LANES == 0, "the packed block-diagonal products need chunk == head size"
    row = lax.broadcasted_iota(jnp.int32, (L, L), 0)
    col = lax.broadcasted_iota(jnp.int32, (L, L), 1)
    tri = jnp.where(row >= col, 1.0, 0.0).astype(BF16)
    ld_1 = logdecay.astype(BF16)
    rest = logdecay - ld_1.astype(F32)
    ld_2 = rest.astype(BF16)
    ld_3 = (rest - ld_2.astype(F32)).astype(BF16)
    cums = []
    for ci in range(n_chunks):
        rs = slice(ci * L, (ci + 1) * L)
        cums.append(jnp.dot(tri, ld_1[rs], preferred_element_type=F32)
                    + jnp.dot(tri, ld_2[rs], preferred_element_type=F32)
                    + jnp.dot(tri, ld_3[rs], preferred_element_type=F32))
    cum = jnp.concatenate(cums, axis=0) if n_chunks > 1 else cums[0]
    last = [cums[ci][L - 1:L, :] for ci in range(n_chunks)]
    cum_last = jnp.concatenate([jnp.broadcast_to(t, (L, C)) for t in last], axis=0) if n_chunks > 1 \
        else jnp.broadcast_to(last[0], (L, C))
    p_in = jnp.exp(cum)
    p_inv = jnp.exp(-cum)
    p_rem = jnp.exp(cum_last - cum)
    b = kkn * a
    at_f = (-kkn * jnp.exp(cum - logdecay)).astype(BF16)
    rt_f = r * p_in
    rt_b = rt_f.astype(BF16)
    bt_b = (b * p_inv).astype(BF16)
    kt_b = (k2 * p_inv).astype(BF16)
    br_b = (b * p_rem).astype(BF16)
    kr_b = (k2 * p_rem).astype(BF16)
    v_b = v.astype(BF16)

    dot = lambda x, y: jnp.dot(x.astype(BF16), y.astype(BF16), preferred_element_type=F32)
    dot_nt = lambda x, y: lax.dot_general(x.astype(BF16), y.astype(BF16), (((1,), (1,)), ((), ())),
                                          preferred_element_type=F32)
    dot_tn = lambda x, y: lax.dot_general(x.astype(BF16), y.astype(BF16), (((0,), (0,)), ((), ())),
                                          preferred_element_type=F32)

    PW = RWKV_PACK * N
    n_groups = C // PW
    lane_head = lax.broadcasted_iota(jnp.int32, (L, PW), 1) // N
    pcol = lax.broadcasted_iota(jnp.int32, (L, PW), 1) - lane_head * N
    prow = lax.broadcasted_iota(jnp.int32, (L, PW), 0)
    p_lower = prow > pcol
    p_lower_eq = prow >= pcol
    p_eye = jnp.where(prow == pcol, 1.0, 0.0)
    zero = jnp.zeros((), BF16)

    def bdiag(t):
        t = t.astype(BF16)
        return jnp.concatenate([jnp.where(lane_head == h, t, zero) for h in range(RWKV_PACK)], axis=0)

    def pack_rows(t):
        out = t[:N]
        for h in range(1, RWKV_PACK):
            out = jnp.where(lane_head == h, t[h * N:(h + 1) * N], out)
        return out

    cut = lambda t, ci, pj: t[ci * L:(ci + 1) * L, pj * PW:(pj + 1) * PW]
    same_head = (lax.broadcasted_iota(jnp.int32, (PW, PW), 0) // N) == \
                (lax.broadcasted_iota(jnp.int32, (PW, PW), 1) // N)
    rq, y0, gg, hh = {}, {}, {}, {}
    for g0 in range(0, n_chunks, RWKV_STAGE_GROUP):
        pieces = [(ci, pj) for ci in range(g0, min(g0 + RWKV_STAGE_GROUP, n_chunks)) for pj in range(n_groups)]
        _rwkv_chunk_algebra(pieces, cut, bdiag, dot, dot_nt, dot_tn, L, PW, at_f, rt_f, rt_b, bt_b, kt_b, br_b, kr_b,
                            v_b, p_lower, p_lower_eq, p_eye, same_head, pack_rows, rq, y0, gg, hh)

    ys = {}
    new_states = []
    for pj in range(n_groups):
        state = states[pj]
        for ci in range(n_chunks):
            p = (ci, pj)
            sb = state.astype(BF16)
            ys[p] = y0[p] + dot_nt(rq[p], bdiag(sb))
            p_tot = p_in[(ci + 1) * L - 1:(ci + 1) * L, pj * PW:(pj + 1) * PW]
            state = state * p_tot + dot(sb, gg[p]) + hh[p]
        new_states.append(state)
    y = jnp.concatenate([jnp.concatenate([ys[(ci, pj)] for pj in range(n_groups)], axis=1)
                         for ci in range(n_chunks)], axis=0)

    mu = head_sum(y) * (1.0 / N)
    dev = y - mu
    var = head_sum(dev * dev) * (1.0 / N)
    yn = dev * lax.rsqrt(var + RWKV_GN_EPS)
    bonus = head_sum(r * k2 * rk_ref[...]) * v
    y_ref[...] = ((yn * lnw_ref[...] + lnb_ref[...] + bonus) * g).astype(y_ref.dtype)
    return new_states


def _rwkv_chunk_algebra(pieces, cut, bdiag, dot, dot_nt, dot_tn, L, PW, at_f, rt_f, rt_b, bt_b, kt_b, br_b, kr_b, v_b,
                        p_lower, p_lower_eq, p_eye, same_head, pack_rows, rq_out, y0_out, gg_out, hh_out):
    at = {p: cut(at_f, *p) for p in pieces}
    vv = {p: cut(v_b, *p) for p in pieces}
    m = {p: dot_nt(jnp.concatenate([at[p], cut(rt_b, *p)], axis=0),
                   jnp.concatenate([bdiag(cut(bt_b, *p)), bdiag(cut(kt_b, *p))], axis=0)) for p in pieces}
    m_ab = {p: jnp.where(p_lower, m[p][:L, :PW], 0.0) for p in pieces}
    m_ak = {p: jnp.where(p_lower, m[p][:L, PW:], 0.0).astype(BF16) for p in pieces}
    m_rb = {p: jnp.where(p_lower_eq, m[p][L:, :PW], 0.0).astype(BF16) for p in pieces}
    m_rk = {p: jnp.where(p_lower_eq, m[p][L:, PW:], 0.0).astype(BF16) for p in pieces}
    inv = {p: p_eye + m_ab[p] for p in pieces}
    pw = {p: m_ab[p].astype(BF16) for p in pieces}
    pw = {p: dot(pw[p], bdiag(pw[p])).astype(BF16) for p in pieces}
    for _ in range(max(2, (L - 1).bit_length()) - 2):
        both = {p: dot(jnp.concatenate([pw[p], inv[p].astype(BF16)], axis=0), bdiag(pw[p])) for p in pieces}
        pw = {p: both[p][:L].astype(BF16) for p in pieces}
        inv = {p: inv[p] + both[p][L:] for p in pieces}
    inv = {p: (inv[p] + dot(inv[p], bdiag(pw[p]))).astype(BF16) for p in pieces}
    mv = {p: dot(jnp.concatenate([m_ak[p], m_rk[p]], axis=0), bdiag(vv[p])) for p in pieces}
    wu = {p: dot(inv[p], jnp.concatenate([bdiag(at[p]), bdiag(mv[p][:L])], axis=1)) for p in pieces}
    w = {p: wu[p][:, :PW].astype(BF16) for p in pieces}
    u0 = {p: wu[p][:, PW:].astype(BF16) for p in pieces}
    ry = {p: dot(m_rb[p], jnp.concatenate([bdiag(w[p]), bdiag(u0[p])], axis=1)) for p in pieces}
    rq_out.update({p: (cut(rt_f, *p) + ry[p][:, :PW]).astype(BF16) for p in pieces})
    y0_out.update({p: ry[p][:, PW:] + mv[p][L:] for p in pieces})
    gg_out.update({p: jnp.where(same_head, dot_tn(w[p], cut(br_b, *p)), 0.0).astype(BF16) for p in pieces})
    hh_full = {p: dot_tn(jnp.concatenate([u0[p], vv[p]], axis=0),
                         jnp.concatenate([cut(br_b, *p), cut(kr_b, *p)], axis=0)) for p in pieces}
    hh_out.update({p: pack_rows(hh_full[p]) for p in pieces})


def _rwkv(zr, w0, wd, a0, wa, wg, k_k, k_a, r_k, ln_w, ln_b, batch, seq):
    T = zr.shape[0]
    L = RWKV_CHUNK * RWKV_CHUNKS_PER_STEP
    nc = seq // L
    vec = lambda a: a.reshape(1, RWKV_WIDTH)
    full = lambda a: pl.BlockSpec(a.shape, lambda b, c: (0,) * a.ndim)
    args = (vec(w0), wd, vec(a0), wa, wg, vec(k_k), vec(k_a), vec(r_k), vec(ln_w), vec(ln_b))
    return pl.pallas_call(
        _rwkv_kernel,
        out_shape=jax.ShapeDtypeStruct((T, RWKV_WIDTH), BF16),
        grid=(batch, nc),
        in_specs=[pl.BlockSpec((L, SHIFT_COLS), lambda b, c: (b * nc + c, 0))] + [full(a) for a in args],
        out_specs=pl.BlockSpec((L, RWKV_WIDTH), lambda b, c: (b * nc + c, 0)),
        scratch_shapes=[pltpu.VMEM((RWKV_HEADS // RWKV_PACK, RWKV_HEAD, RWKV_PACK * RWKV_HEAD), F32)],
        compiler_params=_params("arbitrary", "arbitrary"),
    )(zr, *args)


def _merge_kernel(x_ref, ya_ref, yr_ref, gate_ref, wba_ref, wbr_ref, wo_ref, lnm_ref, wr_ref, br_ref,
                  x1_ref, h2_ref, route_ref, cnt_ref, cnt_scr):
    i = pl.program_id(0)

    @pl.when(i == 0)
    def _():
        cnt_scr[...] = jnp.zeros_like(cnt_scr)

    part = x_ref.shape[0] // MERGE_ROW_PARTS
    chains = [_merge_rows(x_ref, ya_ref, yr_ref, gate_ref, wba_ref, wbr_ref, wo_ref, lnm_ref, wr_ref, br_ref,
                          x1_ref, h2_ref, route_ref, cnt_ref, cnt_scr, j * part, part)
              for j in range(MERGE_ROW_PARTS)]
    for _ in zip(*chains):
        pass


def _merge_rows(x_ref, ya_ref, yr_ref, gate_ref, wba_ref, wbr_ref, wo_ref, lnm_ref, wr_ref, br_ref,
                x1_ref, h2_ref, route_ref, cnt_ref, cnt_scr, r0, tm):
    D = D_MODEL
    rows = slice(r0, r0 + tm)
    gates = gate_ref[rows, :].astype(F32)
    ya = jnp.dot(ya_ref[rows, :], wba_ref[...], preferred_element_type=F32)
    yr = jnp.dot(yr_ref[rows, :], wbr_ref[...], preferred_element_type=F32)
    yield
    merged = gates[:, :D] * ya + gates[:, D:] * yr
    x1 = x_ref[rows, :] + jnp.dot(merged.astype(BF16), wo_ref[...], preferred_element_type=F32)
    x1_ref[rows, :] = x1
    yield
    h2 = _rms(x1, lnm_ref[...])
    _store_slabs(h2_ref, h2, r0)

    h_hi = h2.astype(BF16)
    h_lo = (h2 - h_hi.astype(F32)).astype(BF16)
    parts = (jnp.dot(h_hi, wr_ref[...], preferred_element_type=F32)
             + jnp.dot(h_lo, wr_ref[...], preferred_element_type=F32))
    yield
    logits = parts[:, :LANES] + parts[:, LANES:] + br_ref[...]
    lane = lax.broadcasted_iota(jnp.int32, logits.shape, 1)
    big = jnp.int32(1 << 20)
    gl = jnp.where(lane < N_GROUPS, logits, NEG_INF)
    gmax = jnp.max(gl, axis=-1, keepdims=True)
    yield
    gidx = jnp.min(jnp.where(gl == gmax, lane, big), axis=-1, keepdims=True)
    gsum = jnp.sum(jnp.where(lane < N_GROUPS, jnp.exp(logits - gmax), 0.0), axis=-1, keepdims=True)
    yield
    g_w = 1.0 / gsum
    lo = N_GROUPS + EXPERTS_PER_GROUP * gidx
    el = jnp.where((lane >= lo) & (lane < lo + EXPERTS_PER_GROUP), logits, NEG_INF)
    m1 = jnp.max(el, axis=-1, keepdims=True)
    yield
    i1 = jnp.min(jnp.where(el == m1, lane, big), axis=-1, keepdims=True)
    yield
    el2 = jnp.where(lane == i1, NEG_INF, el)
    m2 = jnp.max(el2, axis=-1, keepdims=True)
    yield
    i2 = jnp.min(jnp.where(el2 == m2, lane, big), axis=-1, keepdims=True)
    yield
    d = jnp.exp(m2 - m1)
    w1 = g_w / (1.0 + d)
    w2 = g_w * d / (1.0 + d)
    e1 = i1 - N_GROUPS
    e2 = i2 - N_GROUPS

    hit1 = lane == e1
    hit2 = lane == e2
    onehot = jnp.where(hit1 | hit2, 1.0, 0.0).astype(BF16)
    r_i = lax.broadcasted_iota(jnp.int32, (tm, tm), 0)
    c_i = lax.broadcasted_iota(jnp.int32, (tm, tm), 1)
    before = jnp.dot(jnp.where(r_i > c_i, 1.0, 0.0).astype(BF16), onehot, preferred_element_type=F32)
    yield
    before = before + cnt_scr[...]
    rank1 = jnp.sum(jnp.where(hit1, before, 0.0), axis=-1, keepdims=True)
    rank2 = jnp.sum(jnp.where(hit2, before, 0.0), axis=-1, keepdims=True)
    cnt_scr[...] = cnt_scr[...] + jnp.sum(onehot.astype(F32), axis=0, keepdims=True)
    cnt_ref[...] = cnt_scr[...]

    route = jnp.where(lane == 0, e1.astype(F32), 0.0)
    route = jnp.where(lane == 1, e2.astype(F32), route)
    route = jnp.where(lane == 2, rank1, route)
    route = jnp.where(lane == 3, rank2, route)
    route = jnp.where(lane == 4, w1, route)
    route = jnp.where(lane == 5, w2, route)
    route_ref[rows, :] = route
    yield


def _merge(x2, y_att, y_rwkv, gates, wba, wbr, wo, ln_moe, w_router, b_router):
    T, D = x2.shape
    tm = MERGE_TILE
    row = lambda n: pl.BlockSpec((tm, n), lambda i: (i, 0))
    full = lambda a: pl.BlockSpec(a.shape, lambda i: (0,) * a.ndim)
    ln_moe = ln_moe.reshape(1, D)
    return pl.pallas_call(
        _merge_kernel,
        out_shape=(jax.ShapeDtypeStruct((T, D), F32), jax.ShapeDtypeStruct((T * SLAB_ROWS, LANES), F32),
                   jax.ShapeDtypeStruct((T, LANES), F32), jax.ShapeDtypeStruct((1, LANES), F32)),
        grid=(T // tm,),
        in_specs=[row(D), row(ATT_WIDTH), row(RWKV_WIDTH), row(GATE_COLS),
                  full(wba), full(wbr), full(wo), full(ln_moe), full(w_router), full(b_router)],
        out_specs=(row(D), pl.BlockSpec((tm * SLAB_ROWS, LANES), lambda i: (i, 0)), row(LANES),
                   pl.BlockSpec((1, LANES), lambda i: (0, 0))),
        scratch_shapes=[pltpu.VMEM((1, LANES), F32)],
        compiler_params=_params("arbitrary"),
    )(x2, y_att, y_rwkv, gates, wba, wbr, wo, ln_moe, w_router, b_router)


def _dispatch_kernel(dest_ref, h_ref, xrows_hbm, row_sem, *, tile):
    base = pl.program_id(0) * tile

    def row_copy(j, slot):
        return pltpu.make_async_copy(_slab(h_ref, j), _slab(xrows_hbm, dest_ref[2 * (base + j) + slot]), row_sem)

    def issue(j, carry):
        row_copy(j, 0).start(priority=0)
        row_copy(j, 1).start(priority=1)
        return carry

    lax.fori_loop(0, tile, issue, 0, unroll=8)

    def drain(j, carry):
        row_copy(j, 0).wait()
        row_copy(j, 1).wait()
        return carry

    lax.fori_loop(0, tile, drain, 0, unroll=8)


def _dispatch(h2_slabs, dest):
    T = h2_slabs.shape[0] // SLAB_ROWS
    tile = min(DISPATCH_TILE, T)
    assert T % tile == 0
    return pl.pallas_call(
        functools.partial(_dispatch_kernel, tile=tile),
        out_shape=jax.ShapeDtypeStruct((2 * T * SLAB_ROWS, LANES), F32),
        grid_spec=pltpu.PrefetchScalarGridSpec(
            num_scalar_prefetch=1,
            grid=(T // tile,),
            in_specs=[pl.BlockSpec((tile * SLAB_ROWS, LANES), lambda i, d: (i, 0))],
            out_specs=pl.BlockSpec(memory_space=pl.ANY),
            scratch_shapes=[pltpu.SemaphoreType.DMA],
        ),
        compiler_params=_params("arbitrary"),
    )(dest, h2_slabs)


def _expert_kernel(item_e, item_b, item_lo, item_hi, x_ref, wg_ref, wu_ref, wd_ref, y_ref, wgu_bf, wd_bf):
    w = pl.program_id(0)
    lo = item_lo[w]
    hi = item_hi[w]

    @pl.when((w == 0) | (item_e[w] != item_e[jnp.maximum(w - 1, 0)]))
    def _():
        wgu_bf[:, :D_EXPERT] = wg_ref[0].astype(BF16)
        wgu_bf[:, D_EXPERT:] = wu_ref[0].astype(BF16)
        wd_bf[...] = wd_ref[0].astype(BF16)

    def run(keep_other_rows):
        sub = EXPERT_SUB_BLOCK
        for s in range(x_ref.shape[0] // SLAB_ROWS // sub):
            xb = _load_slabs(x_ref, sub, s * sub).astype(BF16)
            hgu = jnp.dot(xb, wgu_bf[...], preferred_element_type=F32)
            hg = hgu[:, :D_EXPERT]
            hid = hg * _sigmoid(hg) * hgu[:, D_EXPERT:]
            y = jnp.dot(hid.astype(BF16), wd_bf[...], preferred_element_type=F32)
            if keep_other_rows:
                rows = lax.broadcasted_iota(jnp.int32, y.shape, 0) + s * sub
                y = jnp.where((rows >= lo) & (rows < hi), y, _load_slabs(y_ref, sub, s * sub))
            _store_slabs(y_ref, y, s * sub)

    pl.when((hi > lo) & (lo == 0))(functools.partial(run, False))
    pl.when((hi > lo) & (lo > 0))(functools.partial(run, True))


def _experts(xrows, items, wg, wu, wd):
    bm = EXPERT_BLOCK
    D = D_MODEL
    n_items = items[0].shape[0]
    block = pl.BlockSpec((bm * SLAB_ROWS, LANES), lambda w, ie, ib, lo, hi: (ib[w], 0))
    return pl.pallas_call(
        _expert_kernel,
        out_shape=jax.ShapeDtypeStruct(xrows.shape, F32),
        grid_spec=pltpu.PrefetchScalarGridSpec(
            num_scalar_prefetch=4,
            grid=(n_items,),
            in_specs=[block,
                      pl.BlockSpec((1, D, D_EXPERT), lambda w, ie, ib, lo, hi: (ie[w], 0, 0)),
                      pl.BlockSpec((1, D, D_EXPERT), lambda w, ie, ib, lo, hi: (ie[w], 0, 0)),
                      pl.BlockSpec((1, D_EXPERT, D), lambda w, ie, ib, lo, hi: (ie[w], 0, 0))],
            out_specs=block,
            scratch_shapes=[pltpu.VMEM((D, 2 * D_EXPERT), BF16), pltpu.VMEM((D_EXPERT, D), BF16)],
        ),
        compiler_params=_params("arbitrary"),
    )(*items, xrows, wg, wu, wd)


def _work_items(counts, n_rows):
    bm = EXPERT_BLOCK
    nb = n_rows // bm
    n_items = nb + N_EXPERTS
    ends = jnp.cumsum(counts)
    starts = ends - counts
    first_blk = starts // bm
    last_blk = (ends - 1) // bm
    per_e = jnp.where(counts > 0, last_blk - first_blk + 1, 0)
    item_end = jnp.cumsum(per_e)
    total = item_end[-1]
    w = jnp.arange(n_items, dtype=jnp.int32)
    wc = jnp.minimum(w, total - 1)
    e = jnp.minimum(jnp.sum((item_end[None, :] <= wc[:, None]).astype(jnp.int32), axis=1), N_EXPERTS - 1)
    b = first_blk[e] + (wc - (item_end[e] - per_e[e]))
    lo = jnp.maximum(starts[e], b * bm) - b * bm
    hi = jnp.minimum(ends[e], (b + 1) * bm) - b * bm
    live = w < total
    lo = jnp.where(live, lo, 0)
    hi = jnp.where(live, hi, 0)
    return (e.astype(jnp.int32), b.astype(jnp.int32), lo.astype(jnp.int32), hi.astype(jnp.int32)), starts


def _final_kernel(dest_ref, yrows_hbm, x1_ref, route_ref, p_ref, lnp_ref, wpg_ref, wpp_ref, lnf_ref,
                  o_ref, rows_ref, row_sems):
    i = pl.program_id(0)
    n = pl.num_programs(0)
    tm = x1_ref.shape[0]

    def row_copy(tile, j, slot, par):
        return pltpu.make_async_copy(_slab(yrows_hbm, dest_ref[2 * (tile * tm + j) + slot]),
                                     _slab(rows_ref.at[par, slot], j), row_sems.at[par])

    def wait_tile(par):
        pltpu.make_async_copy(rows_ref.at[par], rows_ref.at[par], row_sems.at[par]).wait()

    @pl.when(i == 0)
    def _():
        def body(j, carry):
            row_copy(0, j, 0, 0).start(priority=0)
            row_copy(0, j, 1, 0).start(priority=1)
            return carry
        lax.fori_loop(0, tm, body, 0, unroll=8)

    def step(par):
        wait_tile(par)
        nxt = jnp.minimum(i + 1, n - 1)
        for j in range(tm):
            row_copy(nxt, j, 0, 1 - par).start(priority=1)
            row_copy(nxt, j, 1, 1 - par).start(priority=1)
        def rows_chain(r0, nr):
            rows = slice(r0, r0 + nr)
            pp = jnp.dot(p_ref[rows, :].astype(BF16), wpp_ref[...], preferred_element_type=F32)
            route = route_ref[rows, :]
            x2 = (x1_ref[rows, :] + route[:, 4:5] * _load_slabs(rows_ref.at[par, 0], nr, r0)
                  + route[:, 5:6] * _load_slabs(rows_ref.at[par, 1], nr, r0))
            yield
            h = _rms(x2, lnp_ref[...]).astype(BF16)
            yield
            gate = _sigmoid(jnp.dot(h, wpg_ref[...], preferred_element_type=F32))
            yield
            x3 = x2 + gate * pp
            o_ref[rows, :] = _rms(x3, lnf_ref[...])
            yield

        part = tm // FINAL_ROW_PARTS
        for _ in zip(*[rows_chain(j * part, part) for j in range(FINAL_ROW_PARTS)]):
            pass

        @pl.when(i == n - 1)
        def _():
            wait_tile(1 - par)

    for par in range(2):
        pl.when(i % 2 == par)(functools.partial(step, par))


def _final(dest, yrows, x1, route, p2, ln_ple, wpg, wpp, ln_final):
    T, D = x1.shape
    tm = GATHER_TILE
    row = lambda n: pl.BlockSpec((tm, n), lambda i, d: (i, 0))
    full = lambda a: pl.BlockSpec(a.shape, lambda i, d: (0,) * a.ndim)
    ln_ple = ln_ple.reshape(1, D)
    ln_final = ln_final.reshape(1, D)
    return pl.pallas_call(
        _final_kernel,
        out_shape=jax.ShapeDtypeStruct((T, D), F32),
        grid_spec=pltpu.PrefetchScalarGridSpec(
            num_scalar_prefetch=1,
            grid=(T // tm,),
            in_specs=[pl.BlockSpec(memory_space=pl.ANY),
                      row(D), row(LANES), row(PLE_DIM), full(ln_ple), full(wpg), full(wpp), full(ln_final)],
            out_specs=row(D),
            scratch_shapes=[pltpu.VMEM((2, 2, tm * SLAB_ROWS, LANES), F32), pltpu.SemaphoreType.DMA((2,))],
        ),
        compiler_params=_params("arbitrary"),
    )(dest, yrows, x1, route, p2, ln_ple, wpg, wpp, ln_final)


def kernel(x, p, positions, ln_mix, w_in, mu_shift, w0, w_decay_up, a0, w_aaa_up, w_gate_up, k_k, k_a, r_k, ln_x_w, ln_x_b, sinks, w_branch_att, w_branch_rwkv, w_out, ln_moe, w_group, b_group, w_expert, b_expert, w_gate_e, w_up_e, w_down_e, ln_ple, w_ple_gate, w_ple_proj, ln_final):
    B, S, D = x.shape
    T = B * S
    depth = w_in.shape[0]
    assert D == D_MODEL and S % ROW_TILE == 0 and S % (WINDOW * ATT_BLOCKS_PER_STEP) == 0 and S % (RWKV_CHUNK * RWKV_CHUNKS_PER_STEP) == 0
    assert T % GATHER_TILE == 0 and (2 * T) % EXPERT_BLOCK == 0
    cos, sin = _rope_tables(positions)
    x2 = x.reshape(T, D)
    out = None
    for i in range(depth):
        q, k, v, zr, gates = _inproj(x2, ln_mix[i], w_in[i].astype(BF16), mu_shift[i], cos, sin, S)
        y_att = _attention(q, k, v, sinks[i], B, S)
        y_rwkv = _rwkv(zr, w0[i], w_decay_up[i], a0[i], w_aaa_up[i], w_gate_up[i], k_k[i], k_a[i], r_k[i],
                       ln_x_w[i], ln_x_b[i], B, S)
        pad = LANES - N_GROUPS - N_EXPERTS
        w_router = jnp.concatenate([w_group[i], w_expert[i], jnp.zeros((D, pad), F32)], axis=1)
        w_router_hi = w_router.astype(BF16)
        w_router = jnp.concatenate([w_router_hi, (w_router - w_router_hi.astype(F32)).astype(BF16)], axis=1)
        b_router = jnp.concatenate([b_group[i], b_expert[i], jnp.zeros((pad,), F32)]).reshape(1, LANES)
        x1, h2, route, cnt = _merge(x2, y_att, y_rwkv, gates, w_branch_att[i].astype(BF16),
                                    w_branch_rwkv[i].astype(BF16), w_out[i].astype(BF16), ln_moe[i],
                                    w_router, b_router)
        counts = cnt[0, :N_EXPERTS].astype(jnp.int32)
        items, starts = _work_items(counts, 2 * T)
        expert = route[:, 0:2].astype(jnp.int32)
        first_row = jnp.sum(jnp.where(expert[..., None] == jnp.arange(N_EXPERTS, dtype=jnp.int32), starts, 0), axis=-1)
        dest = (first_row + route[:, 2:4].astype(jnp.int32)).reshape(-1)
        xrows = _dispatch(h2, dest)
        yrows = _experts(xrows, items, w_gate_e[i], w_up_e[i], w_down_e[i])
        last = i == depth - 1
        assert last, "the final-norm kernel closes the only layer"
        out = _final(dest, yrows, x1, route, p[i].reshape(T, PLE_DIM), ln_ple[i], w_ple_gate[i].astype(BF16),
                     w_ple_proj[i].astype(BF16), ln_final)
    return out.reshape(B, S, D)
```

```python
import functools
import math

import jax
import jax.numpy as jnp
from jax import lax
from jax.experimental import pallas as pl
from jax.experimental.pallas import tpu as pltpu

F32 = jnp.float32
BF16 = jnp.bfloat16
HIGHEST = lax.Precision.HIGHEST

D_MODEL = 1024
PLE_DIM = 256
ATT_HEADS = 8
ATT_KV_HEADS = 2
HEAD_DIM = 64
ATT_WIDTH = ATT_HEADS * HEAD_DIM
KV_WIDTH = ATT_KV_HEADS * HEAD_DIM
WINDOW = 128
ROPE_THETA = 10000.0
RWKV_HEADS = 8
RWKV_HEAD = 64
RWKV_WIDTH = RWKV_HEADS * RWKV_HEAD
DECAY_LORA = 64
AAA_LORA = 64
GATE_LORA = 128
RWKV_GN_EPS = 64e-5
ATT_COLS = ATT_WIDTH + 2 * KV_WIDTH
SHIFT_COLS = 3 * RWKV_WIDTH + DECAY_LORA + AAA_LORA + GATE_LORA
GATE_COLS = 2 * D_MODEL
N_GROUPS = 4
EXPERTS_PER_GROUP = 8
N_EXPERTS = N_GROUPS * EXPERTS_PER_GROUP
D_EXPERT = 512
NORM_EPS = 1e-6
NEG_INF = -1e30

LANES = 128
SLAB_ROWS = D_MODEL // LANES
VMEM_LIMIT = 56 * 1024 * 1024

ROW_TILE = 512
MERGE_TILE = 1024
MERGE_ROW_PARTS = 2
ATT_BLOCKS_PER_STEP = 4
RWKV_CHUNK = 64
RWKV_CHUNKS_PER_STEP = 8
RWKV_PACK = 2
RWKV_ROW_PARTS = 1
RWKV_STAGE_GROUP = 4
EXPERT_BLOCK = 1024
EXPERT_SUB_BLOCK = 1024
FINAL_ROW_PARTS = 1
GATHER_TILE = 256
DISPATCH_TILE = 512


def _params(*sem):
    return pltpu.CompilerParams(dimension_semantics=sem, vmem_limit_bytes=VMEM_LIMIT)


def _bdot(a, b):
    return jnp.dot(a.astype(BF16), b.astype(BF16), preferred_element_type=F32)


def _bdot_nt(a, b):
    return lax.dot_general(a.astype(BF16), b.astype(BF16), (((1,), (1,)), ((), ())),
                           preferred_element_type=F32)


def _fdot(a, b):
    return jnp.dot(a, b, preferred_element_type=F32, precision=HIGHEST)


def _rms(x, g):
    return x * lax.rsqrt(jnp.mean(x * x, axis=-1, keepdims=True) + NORM_EPS) * g


def _sigmoid(x):
    return 1.0 / (1.0 + jnp.exp(-x))


def _store_slabs(ref, val, first=0):
    m = val.shape[0]
    for j in range(SLAB_ROWS):
        ref[pl.ds(first * SLAB_ROWS + j, m, stride=SLAB_ROWS), :] = val[:, j * LANES:(j + 1) * LANES]


def _load_slabs(ref, m, first=0):
    return jnp.concatenate([ref[pl.ds(first * SLAB_ROWS + j, m, stride=SLAB_ROWS), :] for j in range(SLAB_ROWS)],
                           axis=1)


def _slab(ref, index):
    return ref.at[pl.ds(pl.multiple_of(index * SLAB_ROWS, SLAB_ROWS), SLAB_ROWS)]


def _rope_table_kernel(pos_ref, invf_ref, cos_ref, sin_ref):
    half = HEAD_DIM // 2
    per_row = LANES // half
    rows = pos_ref.shape[0]
    ang = pos_ref[...].astype(F32) * invf_ref[...]
    group = lax.broadcasted_iota(jnp.int32, ang.shape, 1) // half
    sign = jnp.where(group % 2 == 0, -1.0, 1.0)
    for table, out_ref, scale in ((jnp.cos(ang), cos_ref, None), (jnp.sin(ang), sin_ref, sign)):
        rolled = [table] + [pltpu.roll(table, half * j, 1) for j in range(1, per_row)]
        for m in range(per_row):
            out = rolled[(per_row - 1 - m) % per_row]
            for g in range(per_row - 1):
                out = jnp.where(group == g, rolled[(g - m) % per_row], out)
            out_ref[pl.ds(m, rows, stride=per_row), :] = out if scale is None else out * scale


def _rope_tables(positions):
    T = positions.size
    half = HEAD_DIM // 2
    per_row = LANES // half
    inv_freq = ROPE_THETA ** (-jnp.arange(half, dtype=F32) / half)
    invf = jnp.tile(inv_freq, per_row).reshape(1, LANES)
    pos = jnp.repeat(positions.reshape(T // per_row, per_row), half, axis=1)
    rows = T // per_row
    tr = min(rows, 1024)
    return pl.pallas_call(
        _rope_table_kernel,
        out_shape=(jax.ShapeDtypeStruct((T, LANES), F32),) * 2,
        grid=(rows // tr,),
        in_specs=[pl.BlockSpec((tr, LANES), lambda i: (i, 0)),
                  pl.BlockSpec((1, LANES), lambda i: (0, 0))],
        out_specs=(pl.BlockSpec((tr * per_row, LANES), lambda i: (i, 0)),) * 2,
        compiler_params=_params("arbitrary"),
    )(pos, invf)


def _rope(t, cos, sin):
    n = t.shape[1]
    reps = n // LANES
    c = jnp.tile(cos, (1, reps)) if reps > 1 else cos
    s = jnp.tile(sin, (1, reps)) if reps > 1 else sin
    lane = lax.broadcasted_iota(jnp.int32, t.shape, 1)
    first_half = (lane % HEAD_DIM) < (HEAD_DIM // 2)
    partner = jnp.where(first_half, pltpu.roll(t, n - HEAD_DIM // 2, 1), pltpu.roll(t, HEAD_DIM // 2, 1))
    return t * c + partner * s


def _inproj_kernel(x_ref, g_ref, w_ref, mu_ref, cos_ref, sin_ref,
                   q_ref, k_ref, v_ref, zr_ref, gate_ref, carry_ref, *, tiles_per_seq):
    i = pl.program_id(0)
    tm = x_ref.shape[0]
    h = _rms(x_ref[...], g_ref[...]).astype(BF16)
    cos = cos_ref[...]
    sin = sin_ref[...]

    za = jnp.dot(h, w_ref[:, :ATT_COLS], preferred_element_type=F32)
    q_ref[...] = (_rope(za[:, :ATT_WIDTH], cos, sin) * (HEAD_DIM ** -0.5)).astype(BF16)

    def per_head_doubled(t):
        lane = lax.broadcasted_iota(jnp.int32, t.shape, 1)
        swapped = pltpu.roll(t, HEAD_DIM, 1)
        low = lane < HEAD_DIM
        return jnp.concatenate([jnp.where(low, t, swapped), jnp.where(low, swapped, t)], axis=1).astype(BF16)

    k_ref[...] = per_head_doubled(_rope(za[:, ATT_WIDTH:ATT_WIDTH + KV_WIDTH], cos, sin))
    v_ref[...] = per_head_doubled(za[:, ATT_WIDTH + KV_WIDTH:])

    zs = jnp.dot(h, w_ref[:, ATT_COLS:ATT_COLS + SHIFT_COLS], preferred_element_type=F32)
    row = lax.broadcasted_iota(jnp.int32, zs.shape, 0)
    seq_start = (i % tiles_per_seq) == 0
    before = jnp.where(seq_start, 0.0, carry_ref[0:1, :])
    prev = jnp.where(row == 0, before, pltpu.roll(zs, 1, 0))
    carry_ref[0:1, :] = zs[tm - 1:tm, :]
    zr_ref[...] = zs + (prev - zs) * mu_ref[...]

    zg = jnp.dot(h, w_ref[:, ATT_COLS + SHIFT_COLS:], preferred_element_type=F32)
    gate_ref[...] = _sigmoid(zg).astype(BF16)


def _inproj(x2, ln, w_in, mu, cos, sin, seq):
    T, D = x2.shape
    tm = ROW_TILE
    row = lambda n: pl.BlockSpec((tm, n), lambda i: (i, 0))
    full = lambda a: pl.BlockSpec(a.shape, lambda i: (0,) * a.ndim)
    resident = lambda a: pl.BlockSpec(a.shape, lambda i: (0,) * a.ndim, pipeline_mode=pl.Buffered(1))
    ln = ln.reshape(1, D)
    mu = mu.reshape(1, SHIFT_COLS)
    return pl.pallas_call(
        functools.partial(_inproj_kernel, tiles_per_seq=seq // tm),
        out_shape=(jax.ShapeDtypeStruct((T, ATT_WIDTH), BF16),
                   jax.ShapeDtypeStruct((T, 2 * KV_WIDTH), BF16),
                   jax.ShapeDtypeStruct((T, 2 * KV_WIDTH), BF16),
                   jax.ShapeDtypeStruct((T, SHIFT_COLS), F32),
                   jax.ShapeDtypeStruct((T, GATE_COLS), BF16)),
        grid=(T // tm,),
        in_specs=[row(D), full(ln), resident(w_in), full(mu), row(LANES), row(LANES)],
        out_specs=(row(ATT_WIDTH), row(2 * KV_WIDTH), row(2 * KV_WIDTH), row(SHIFT_COLS), row(GATE_COLS)),
        scratch_shapes=[pltpu.VMEM((8, SHIFT_COLS), F32)],
        compiler_params=_params("arbitrary"),
    )(x2, ln, w_in, mu, cos, sin)


def _attn_kernel(sink_ref, q_ref, kp_ref, kc_ref, vp_ref, vc_ref, o_ref):
    n = pl.program_id(1)
    bq = WINDOW
    grp = ATT_HEADS // ATT_KV_HEADS
    gw = grp * HEAD_DIM
    kall = jnp.concatenate([kp_ref[...], kc_ref[...]], axis=0)
    vall = jnp.concatenate([vp_ref[...], vc_ref[...]], axis=0)
    qi = lax.broadcasted_iota(jnp.int32, (grp * bq, 2 * bq), 0) % bq
    si = lax.broadcasted_iota(jnp.int32, (grp * bq, 2 * bq), 1)
    diff = qi + bq - si
    band = (diff >= 0) & (diff < WINDOW)
    head_row = lax.broadcasted_iota(jnp.int32, (grp * bq, 1), 0) // bq
    lane_head = lax.broadcasted_iota(jnp.int32, (bq, gw), 1) // HEAD_DIM
    zero = jnp.zeros((), BF16)
    for blk in range(q_ref.shape[0] // bq):
        valid = band & ((si >= bq) | (n > 0)) if blk == 0 else band
        outs = []
        for j in range(ATT_KV_HEADS):
            qg = q_ref[blk * bq:(blk + 1) * bq, j * gw:(j + 1) * gw]
            lhs = jnp.concatenate([jnp.where(lane_head == g, qg, zero) for g in range(grp)], axis=0)
            k_half = kall[blk * bq:(blk + 2) * bq, j * 2 * HEAD_DIM:(j + 1) * 2 * HEAD_DIM]
            v_half = vall[blk * bq:(blk + 2) * bq, j * 2 * HEAD_DIM:(j + 1) * 2 * HEAD_DIM]
            k_rep = jnp.concatenate([k_half, k_half], axis=1)
            v_rep = jnp.concatenate([v_half, v_half], axis=1)
            s = lax.dot_general(lhs, k_rep, (((1,), (1,)), ((), ())), preferred_element_type=F32)
            s = jnp.where(valid, s, NEG_INF)
            sink = jnp.zeros((grp * bq, 1), F32)
            for g in range(grp):
                sink = jnp.where(head_row == g, sink_ref[j * grp + g], sink)
            m = jnp.maximum(jnp.max(s, axis=-1, keepdims=True), sink)
            e = jnp.exp(s - m)
            denom = jnp.sum(e, axis=-1, keepdims=True) + jnp.exp(sink - m)
            pv = jnp.dot(e.astype(BF16), v_rep, preferred_element_type=F32) / denom
            out = pv[:bq]
            for g in range(1, grp):
                out = jnp.where(lane_head == g, pv[g * bq:(g + 1) * bq], out)
            outs.append(out)
        o_ref[blk * bq:(blk + 1) * bq, :] = jnp.concatenate(outs, axis=1).astype(o_ref.dtype)


def _attention(q, k, v, sinks, batch, seq):
    T = q.shape[0]
    bq = WINDOW
    per_step = ATT_BLOCKS_PER_STEP
    ns = seq // (bq * per_step)
    cur = lambda b, n: (b * ns + n, 0)
    prev = lambda b, n: (jnp.maximum((b * ns + n) * per_step - 1, 0), 0)
    return pl.pallas_call(
        _attn_kernel,
        out_shape=jax.ShapeDtypeStruct((T, ATT_WIDTH), BF16),
        grid=(batch, ns),
        in_specs=[pl.BlockSpec(memory_space=pltpu.SMEM),
                  pl.BlockSpec((bq * per_step, ATT_WIDTH), cur),
                  pl.BlockSpec((bq, 2 * KV_WIDTH), prev), pl.BlockSpec((bq * per_step, 2 * KV_WIDTH), cur),
                  pl.BlockSpec((bq, 2 * KV_WIDTH), prev), pl.BlockSpec((bq * per_step, 2 * KV_WIDTH), cur)],
        out_specs=pl.BlockSpec((bq * per_step, ATT_WIDTH), cur),
        compiler_params=_params("arbitrary", "arbitrary"),
    )(sinks, q, k, k, v, v)


def _rwkv_kernel(z_ref, w0_ref, wd_ref, a0_ref, wa_ref, wg_ref, kk_ref, ka_ref, rk_ref, lnw_ref, lnb_ref,
                 y_ref, s_ref):
    c = pl.program_id(1)

    @pl.when(c == 0)
    def _():
        s_ref[...] = jnp.zeros_like(s_ref)

    part = z_ref.shape[0] // RWKV_ROW_PARTS
    states = [s_ref[pj] for pj in range(s_ref.shape[0])]
    for i in range(RWKV_ROW_PARTS):
        states = _rwkv_rows(z_ref.at[pl.ds(i * part, part)], w0_ref, wd_ref, a0_ref, wa_ref, wg_ref, kk_ref, ka_ref,
                            rk_ref, lnw_ref, lnb_ref, y_ref.at[pl.ds(i * part, part)], states)
    for pj, state in enumerate(states):
        s_ref[pj] = state


def _rwkv_rows(z_ref, w0_ref, wd_ref, a0_ref, wa_ref, wg_ref, kk_ref, ka_ref, rk_ref, lnw_ref, lnb_ref, y_ref, states):
    L = RWKV_CHUNK
    rows = z_ref.shape[0]
    n_chunks = rows // L
    C = RWKV_WIDTH
    N = RWKV_HEAD
    r = z_ref[:, 0:C]
    k = z_ref[:, C:2 * C]
    v = z_ref[:, 2 * C:3 * C]
    xw = z_ref[:, 3 * C:3 * C + DECAY_LORA]
    xa = z_ref[:, 3 * C + DECAY_LORA:3 * C + DECAY_LORA + AAA_LORA]
    xg = z_ref[:, 3 * C + DECAY_LORA + AAA_LORA:]

    wlin = w0_ref[...] + _bdot(jnp.tanh(xw), wd_ref[...])
    logdecay = -math.exp(-0.5) * _sigmoid(wlin)
    a = _sigmoid(a0_ref[...] + _bdot(xa, wa_ref[...]))
    g = _bdot(_sigmoid(xg), wg_ref[...])
    kk = k * kk_ref[...]
    k2 = k * (1.0 + (a - 1.0) * ka_ref[...])

    slab = 2 * LANES
    hr = lax.broadcasted_iota(jnp.int32, (slab, slab), 0) // N
    hc = lax.broadcasted_iota(jnp.int32, (slab, slab), 1) // N
    head_ones = jnp.where(hr == hc, 1.0, 0.0).astype(BF16)

    def head_sum(t):
        t = t.astype(BF16)
        return jnp.concatenate([jnp.dot(t[:, j * slab:(j + 1) * slab], head_ones, preferred_element_type=F32)
                                for j in range(C // slab)], axis=1)

    kkn = kk / jnp.maximum(jnp.sqrt(head_sum(kk * kk)), 1e-12)

    assert L == N and (RWKV_PACK * N) % LANES == 0, "the packed block-diagonal products need chunk == head size"
    row = lax.broadcasted_iota(jnp.int32, (L, L), 0)
    col = lax.broadcasted_iota(jnp.int32, (L, L), 1)
    tri = jnp.where(row >= col, 1.0, 0.0).astype(BF16)
    ld_1 = logdecay.astype(BF16)
    rest = logdecay - ld_1.astype(F32)
    ld_2 = rest.astype(BF16)
    ld_3 = (rest - ld_2.astype(F32)).astype(BF16)
    cums = []
    for ci in range(n_chunks):
        rs = slice(ci * L, (ci + 1) * L)
        cums.append(jnp.dot(tri, ld_1[rs], preferred_element_type=F32)
                    + jnp.dot(tri, ld_2[rs], preferred_element_type=F32)
                    + jnp.dot(tri, ld_3[rs], preferred_element_type=F32))
    cum = jnp.concatenate(cums, axis=0) if n_chunks > 1 else cums[0]
    last = [cums[ci][L - 1:L, :] for ci in range(n_chunks)]
    cum_last = jnp.concatenate([jnp.broadcast_to(t, (L, C)) for t in last], axis=0) if n_chunks > 1 \
        else jnp.broadcast_to(last[0], (L, C))
    p_in = jnp.exp(cum)
    p_inv = jnp.exp(-cum)
    p_rem = jnp.exp(cum_last - cum)
    b = kkn * a
    at_f = (-kkn * jnp.exp(cum - logdecay)).astype(BF16)
    rt_f = r * p_in
    rt_b = rt_f.astype(BF16)
    bt_b = (b * p_inv).astype(BF16)
    kt_b = (k2 * p_inv).astype(BF16)
    br_b = (b * p_rem).astype(BF16)
    kr_b = (k2 * p_rem).astype(BF16)
    v_b = v.astype(BF16)

    dot = lambda x, y: jnp.dot(x.astype(BF16), y.astype(BF16), preferred_element_type=F32)
    dot_nt = lambda x, y: lax.dot_general(x.astype(BF16), y.astype(BF16), (((1,), (1,)), ((), ())),
                                          preferred_element_type=F32)
    dot_tn = lambda x, y: lax.dot_general(x.astype(BF16), y.astype(BF16), (((0,), (0,)), ((), ())),
                                          preferred_element_type=F32)

    PW = RWKV_PACK * N
    n_groups = C // PW
    lane_head = lax.broadcasted_iota(jnp.int32, (L, PW), 1) // N
    pcol = lax.broadcasted_iota(jnp.int32, (L, PW), 1) - lane_head * N
    prow = lax.broadcasted_iota(jnp.int32, (L, PW), 0)
    p_lower = prow > pcol
    p_lower_eq = prow >= pcol
    p_eye = jnp.where(prow == pcol, 1.0, 0.0)
    zero = jnp.zeros((), BF16)

    def bdiag(t):
        t = t.astype(BF16)
        return jnp.concatenate([jnp.where(lane_head == h, t, zero) for h in range(RWKV_PACK)], axis=0)

    def pack_rows(t):
        out = t[:N]
        for h in range(1, RWKV_PACK):
            out = jnp.where(lane_head == h, t[h * N:(h + 1) * N], out)
        return out

    cut = lambda t, ci, pj: t[ci * L:(ci + 1) * L, pj * PW:(pj + 1) * PW]
    same_head = (lax.broadcasted_iota(jnp.int32, (PW, PW), 0) // N) == \
                (lax.broadcasted_iota(jnp.int32, (PW, PW), 1) // N)
    rq, y0, gg, hh = {}, {}, {}, {}
    for g0 in range(0, n_chunks, RWKV_STAGE_GROUP):
        pieces = [(ci, pj) for ci in range(g0, min(g0 + RWKV_STAGE_GROUP, n_chunks)) for pj in range(n_groups)]
        _rwkv_chunk_algebra(pieces, cut, bdiag, dot, dot_nt, dot_tn, L, PW, at_f, rt_f, rt_b, bt_b, kt_b, br_b, kr_b,
                            v_b, p_lower, p_lower_eq, p_eye, same_head, pack_rows, rq, y0, gg, hh)

    ys = {}
    new_states = []
    for pj in range(n_groups):
        state = states[pj]
        for ci in range(n_chunks):
            p = (ci, pj)
            sb = state.astype(BF16)
            ys[p] = y0[p] + dot_nt(rq[p], bdiag(sb))
            p_tot = p_in[(ci + 1) * L - 1:(ci + 1) * L, pj * PW:(pj + 1) * PW]
            state = state * p_tot + dot(sb, gg[p]) + hh[p]
        new_states.append(state)
    y = jnp.concatenate([jnp.concatenate([ys[(ci, pj)] for pj in range(n_groups)], axis=1)
                         for ci in range(n_chunks)], axis=0)

    mu = head_sum(y) * (1.0 / N)
    dev = y - mu
    var = head_sum(dev * dev) * (1.0 / N)
    yn = dev * lax.rsqrt(var + RWKV_GN_EPS)
    bonus = head_sum(r * k2 * rk_ref[...]) * v
    y_ref[...] = ((yn * lnw_ref[...] + lnb_ref[...] + bonus) * g).astype(y_ref.dtype)
    return new_states


def _rwkv_chunk_algebra(pieces, cut, bdiag, dot, dot_nt, dot_tn, L, PW, at_f, rt_f, rt_b, bt_b, kt_b, br_b, kr_b, v_b,
                        p_lower, p_lower_eq, p_eye, same_head, pack_rows, rq_out, y0_out, gg_out, hh_out):
    at = {p: cut(at_f, *p) for p in pieces}
    vv = {p: cut(v_b, *p) for p in pieces}
    m = {p: dot_nt(jnp.concatenate([at[p], cut(rt_b, *p)], axis=0),
                   jnp.concatenate([bdiag(cut(bt_b, *p)), bdiag(cut(kt_b, *p))], axis=0)) for p in pieces}
    m_ab = {p: jnp.where(p_lower, m[p][:L, :PW], 0.0) for p in pieces}
    m_ak = {p: jnp.where(p_lower, m[p][:L, PW:], 0.0).astype(BF16) for p in pieces}
    m_rb = {p: jnp.where(p_lower_eq, m[p][L:, :PW], 0.0).astype(BF16) for p in pieces}
    m_rk = {p: jnp.where(p_lower_eq, m[p][L:, PW:], 0.0).astype(BF16) for p in pieces}
    inv = {p: p_eye + m_ab[p] for p in pieces}
    pw = {p: m_ab[p].astype(BF16) for p in pieces}
    pw = {p: dot(pw[p], bdiag(pw[p])).astype(BF16) for p in pieces}
    for _ in range(max(2, (L - 1).bit_length()) - 2):
        both = {p: dot(jnp.concatenate([pw[p], inv[p].astype(BF16)], axis=0), bdiag(pw[p])) for p in pieces}
        pw = {p: both[p][:L].astype(BF16) for p in pieces}
        inv = {p: inv[p] + both[p][L:] for p in pieces}
    inv = {p: (inv[p] + dot(inv[p], bdiag(pw[p]))).astype(BF16) for p in pieces}
    mv = {p: dot(jnp.concatenate([m_ak[p], m_rk[p]], axis=0), bdiag(vv[p])) for p in pieces}
    wu = {p: dot(inv[p], jnp.concatenate([bdiag(at[p]), bdiag(mv[p][:L])], axis=1)) for p in pieces}
    w = {p: wu[p][:, :PW].astype(BF16) for p in pieces}
    u0 = {p: wu[p][:, PW:].astype(BF16) for p in pieces}
    ry = {p: dot(m_rb[p], jnp.concatenate([bdiag(w[p]), bdiag(u0[p])], axis=1)) for p in pieces}
    rq_out.update({p: (cut(rt_f, *p) + ry[p][:, :PW]).astype(BF16) for p in pieces})
    y0_out.update({p: ry[p][:, PW:] + mv[p][L:] for p in pieces})
    gg_out.update({p: jnp.where(same_head, dot_tn(w[p], cut(br_b, *p)), 0.0).astype(BF16) for p in pieces})
    hh_full = {p: dot_tn(jnp.concatenate([u0[p], vv[p]], axis=0),
                         jnp.concatenate([cut(br_b, *p), cut(kr_b, *p)], axis=0)) for p in pieces}
    hh_out.update({p: pack_rows(hh_full[p]) for p in pieces})


def _rwkv(zr, w0, wd, a0, wa, wg, k_k, k_a, r_k, ln_w, ln_b, batch, seq):
    T = zr.shape[0]
    L = RWKV_CHUNK * RWKV_CHUNKS_PER_STEP
    nc = seq // L
    vec = lambda a: a.reshape(1, RWKV_WIDTH)
    full = lambda a: pl.BlockSpec(a.shape, lambda b, c: (0,) * a.ndim)
    args = (vec(w0), wd, vec(a0), wa, wg, vec(k_k), vec(k_a), vec(r_k), vec(ln_w), vec(ln_b))
    return pl.pallas_call(
        _rwkv_kernel,
        out_shape=jax.ShapeDtypeStruct((T, RWKV_WIDTH), BF16),
        grid=(batch, nc),
        in_specs=[pl.BlockSpec((L, SHIFT_COLS), lambda b, c: (b * nc + c, 0))] + [full(a) for a in args],
        out_specs=pl.BlockSpec((L, RWKV_WIDTH), lambda b, c: (b * nc + c, 0)),
        scratch_shapes=[pltpu.VMEM((RWKV_HEADS // RWKV_PACK, RWKV_HEAD, RWKV_PACK * RWKV_HEAD), F32)],
        compiler_params=_params("arbitrary", "arbitrary"),
    )(zr, *args)


def _merge_kernel(x_ref, ya_ref, yr_ref, gate_ref, wba_ref, wbr_ref, wo_ref, lnm_ref, wr_ref, br_ref,
                  x1_ref, h2_ref, route_ref, cnt_ref, cnt_scr):
    i = pl.program_id(0)

    @pl.when(i == 0)
    def _():
        cnt_scr[...] = jnp.zeros_like(cnt_scr)

    part = x_ref.shape[0] // MERGE_ROW_PARTS
    chains = [_merge_rows(x_ref, ya_ref, yr_ref, gate_ref, wba_ref, wbr_ref, wo_ref, lnm_ref, wr_ref, br_ref,
                          x1_ref, h2_ref, route_ref, cnt_ref, cnt_scr, j * part, part)
              for j in range(MERGE_ROW_PARTS)]
    for _ in zip(*chains):
        pass


def _merge_rows(x_ref, ya_ref, yr_ref, gate_ref, wba_ref, wbr_ref, wo_ref, lnm_ref, wr_ref, br_ref,
                x1_ref, h2_ref, route_ref, cnt_ref, cnt_scr, r0, tm):
    D = D_MODEL
    rows = slice(r0, r0 + tm)
    gates = gate_ref[rows, :].astype(F32)
    ya = jnp.dot(ya_ref[rows, :], wba_ref[...], preferred_element_type=F32)
    yr = jnp.dot(yr_ref[rows, :], wbr_ref[...], preferred_element_type=F32)
    yield
    merged = gates[:, :D] * ya + gates[:, D:] * yr
    x1 = x_ref[rows, :] + jnp.dot(merged.astype(BF16), wo_ref[...], preferred_element_type=F32)
    x1_ref[rows, :] = x1
    yield
    h2 = _rms(x1, lnm_ref[...])
    _store_slabs(h2_ref, h2, r0)

    h_hi = h2.astype(BF16)
    h_lo = (h2 - h_hi.astype(F32)).astype(BF16)
    parts = (jnp.dot(h_hi, wr_ref[...], preferred_element_type=F32)
             + jnp.dot(h_lo, wr_ref[...], preferred_element_type=F32))
    yield
    logits = parts[:, :LANES] + parts[:, LANES:] + br_ref[...]
    lane = lax.broadcasted_iota(jnp.int32, logits.shape, 1)
    big = jnp.int32(1 << 20)
    gl = jnp.where(lane < N_GROUPS, logits, NEG_INF)
    gmax = jnp.max(gl, axis=-1, keepdims=True)
    yield
    gidx = jnp.min(jnp.where(gl == gmax, lane, big), axis=-1, keepdims=True)
    gsum = jnp.sum(jnp.where(lane < N_GROUPS, jnp.exp(logits - gmax), 0.0), axis=-1, keepdims=True)
    yield
    g_w = 1.0 / gsum
    lo = N_GROUPS + EXPERTS_PER_GROUP * gidx
    el = jnp.where((lane >= lo) & (lane < lo + EXPERTS_PER_GROUP), logits, NEG_INF)
    m1 = jnp.max(el, axis=-1, keepdims=True)
    yield
    i1 = jnp.min(jnp.where(el == m1, lane, big), axis=-1, keepdims=True)
    yield
    el2 = jnp.where(lane == i1, NEG_INF, el)
    m2 = jnp.max(el2, axis=-1, keepdims=True)
    yield
    i2 = jnp.min(jnp.where(el2 == m2, lane, big), axis=-1, keepdims=True)
    yield
    d = jnp.exp(m2 - m1)
    w1 = g_w / (1.0 + d)
    w2 = g_w * d / (1.0 + d)
    e1 = i1 - N_GROUPS
    e2 = i2 - N_GROUPS

    hit1 = lane == e1
    hit2 = lane == e2
    onehot = jnp.where(hit1 | hit2, 1.0, 0.0).astype(BF16)
    r_i = lax.broadcasted_iota(jnp.int32, (tm, tm), 0)
    c_i = lax.broadcasted_iota(jnp.int32, (tm, tm), 1)
    before = jnp.dot(jnp.where(r_i > c_i, 1.0, 0.0).astype(BF16), onehot, preferred_element_type=F32)
    yield
    before = before + cnt_scr[...]
    rank1 = jnp.sum(jnp.where(hit1, before, 0.0), axis=-1, keepdims=True)
    rank2 = jnp.sum(jnp.where(hit2, before, 0.0), axis=-1, keepdims=True)
    cnt_scr[...] = cnt_scr[...] + jnp.sum(onehot.astype(F32), axis=0, keepdims=True)
    cnt_ref[...] = cnt_scr[...]

    route = jnp.where(lane == 0, e1.astype(F32), 0.0)
    route = jnp.where(lane == 1, e2.astype(F32), route)
    route = jnp.where(lane == 2, rank1, route)
    route = jnp.where(lane == 3, rank2, route)
    route = jnp.where(lane == 4, w1, route)
    route = jnp.where(lane == 5, w2, route)
    route_ref[rows, :] = route
    yield


def _merge(x2, y_att, y_rwkv, gates, wba, wbr, wo, ln_moe, w_router, b_router):
    T, D = x2.shape
    tm = MERGE_TILE
    row = lambda n: pl.BlockSpec((tm, n), lambda i: (i, 0))
    full = lambda a: pl.BlockSpec(a.shape, lambda i: (0,) * a.ndim)
    ln_moe = ln_moe.reshape(1, D)
    return pl.pallas_call(
        _merge_kernel,
        out_shape=(jax.ShapeDtypeStruct((T, D), F32), jax.ShapeDtypeStruct((T * SLAB_ROWS, LANES), F32),
                   jax.ShapeDtypeStruct((T, LANES), F32), jax.ShapeDtypeStruct((1, LANES), F32)),
        grid=(T // tm,),
        in_specs=[row(D), row(ATT_WIDTH), row(RWKV_WIDTH), row(GATE_COLS),
                  full(wba), full(wbr), full(wo), full(ln_moe), full(w_router), full(b_router)],
        out_specs=(row(D), pl.BlockSpec((tm * SLAB_ROWS, LANES), lambda i: (i, 0)), row(LANES),
                   pl.BlockSpec((1, LANES), lambda i: (0, 0))),
        scratch_shapes=[pltpu.VMEM((1, LANES), F32)],
        compiler_params=_params("arbitrary"),
    )(x2, y_att, y_rwkv, gates, wba, wbr, wo, ln_moe, w_router, b_router)


def _dispatch_kernel(dest_ref, h_ref, xrows_hbm, row_sem, *, tile):
    base = pl.program_id(0) * tile

    def row_copy(j, slot):
        return pltpu.make_async_copy(_slab(h_ref, j), _slab(xrows_hbm, dest_ref[2 * (base + j) + slot]), row_sem)

    def issue(j, carry):
        row_copy(j, 0).start(priority=0)
        row_copy(j, 1).start(priority=1)
        return carry

    lax.fori_loop(0, tile, issue, 0, unroll=8)

    def drain(j, carry):
        row_copy(j, 0).wait()
        row_copy(j, 1).wait()
        return carry

    lax.fori_loop(0, tile, drain, 0, unroll=8)


def _dispatch(h2_slabs, dest):
    T = h2_slabs.shape[0] // SLAB_ROWS
    tile = min(DISPATCH_TILE, T)
    assert T % tile == 0
    return pl.pallas_call(
        functools.partial(_dispatch_kernel, tile=tile),
        out_shape=jax.ShapeDtypeStruct((2 * T * SLAB_ROWS, LANES), F32),
        grid_spec=pltpu.PrefetchScalarGridSpec(
            num_scalar_prefetch=1,
            grid=(T // tile,),
            in_specs=[pl.BlockSpec((tile * SLAB_ROWS, LANES), lambda i, d: (i, 0))],
            out_specs=pl.BlockSpec(memory_space=pl.ANY),
            scratch_shapes=[pltpu.SemaphoreType.DMA],
        ),
        compiler_params=_params("arbitrary"),
    )(dest, h2_slabs)


def _expert_kernel(item_e, item_b, item_lo, item_hi, x_ref, wg_ref, wu_ref, wd_ref, y_ref, wgu_bf, wd_bf):
    w = pl.program_id(0)
    lo = item_lo[w]
    hi = item_hi[w]

    @pl.when((w == 0) | (item_e[w] != item_e[jnp.maximum(w - 1, 0)]))
    def _():
        wgu_bf[:, :D_EXPERT] = wg_ref[0].astype(BF16)
        wgu_bf[:, D_EXPERT:] = wu_ref[0].astype(BF16)
        wd_bf[...] = wd_ref[0].astype(BF16)

    def run(keep_other_rows):
        sub = EXPERT_SUB_BLOCK
        for s in range(x_ref.shape[0] // SLAB_ROWS // sub):
            xb = _load_slabs(x_ref, sub, s * sub).astype(BF16)
            hgu = jnp.dot(xb, wgu_bf[...], preferred_element_type=F32)
            hg = hgu[:, :D_EXPERT]
            hid = hg * _sigmoid(hg) * hgu[:, D_EXPERT:]
            y = jnp.dot(hid.astype(BF16), wd_bf[...], preferred_element_type=F32)
            if keep_other_rows:
                rows = lax.broadcasted_iota(jnp.int32, y.shape, 0) + s * sub
                y = jnp.where((rows >= lo) & (rows < hi), y, _load_slabs(y_ref, sub, s * sub))
            _store_slabs(y_ref, y, s * sub)

    pl.when((hi > lo) & (lo == 0))(functools.partial(run, False))
    pl.when((hi > lo) & (lo > 0))(functools.partial(run, True))


def _experts(xrows, items, wg, wu, wd):
    bm = EXPERT_BLOCK
    D = D_MODEL
    n_items = items[0].shape[0]
    block = pl.BlockSpec((bm * SLAB_ROWS, LANES), lambda w, ie, ib, lo, hi: (ib[w], 0))
    return pl.pallas_call(
        _expert_kernel,
        out_shape=jax.ShapeDtypeStruct(xrows.shape, F32),
        grid_spec=pltpu.PrefetchScalarGridSpec(
            num_scalar_prefetch=4,
            grid=(n_items,),
            in_specs=[block,
                      pl.BlockSpec((1, D, D_EXPERT), lambda w, ie, ib, lo, hi: (ie[w], 0, 0)),
                      pl.BlockSpec((1, D, D_EXPERT), lambda w, ie, ib, lo, hi: (ie[w], 0, 0)),
                      pl.BlockSpec((1, D_EXPERT, D), lambda w, ie, ib, lo, hi: (ie[w], 0, 0))],
            out_specs=block,
            scratch_shapes=[pltpu.VMEM((D, 2 * D_EXPERT), BF16), pltpu.VMEM((D_EXPERT, D), BF16)],
        ),
        compiler_params=_params("arbitrary"),
    )(*items, xrows, wg, wu, wd)


def _work_items(counts, n_rows):
    bm = EXPERT_BLOCK
    nb = n_rows // bm
    n_items = nb + N_EXPERTS
    ends = jnp.cumsum(counts)
    starts = ends - counts
    first_blk = starts // bm
    last_blk = (ends - 1) // bm
    per_e = jnp.where(counts > 0, last_blk - first_blk + 1, 0)
    item_end = jnp.cumsum(per_e)
    total = item_end[-1]
    w = jnp.arange(n_items, dtype=jnp.int32)
    wc = jnp.minimum(w, total - 1)
    e = jnp.minimum(jnp.sum((item_end[None, :] <= wc[:, None]).astype(jnp.int32), axis=1), N_EXPERTS - 1)
    b = first_blk[e] + (wc - (item_end[e] - per_e[e]))
    lo = jnp.maximum(starts[e], b * bm) - b * bm
    hi = jnp.minimum(ends[e], (b + 1) * bm) - b * bm
    live = w < total
    lo = jnp.where(live, lo, 0)
    hi = jnp.where(live, hi, 0)
    return (e.astype(jnp.int32), b.astype(jnp.int32), lo.astype(jnp.int32), hi.astype(jnp.int32)), starts


def _final_kernel(dest_ref, yrows_hbm, x1_ref, route_ref, p_ref, lnp_ref, wpg_ref, wpp_ref, lnf_ref,
                  o_ref, rows_ref, row_sems):
    i = pl.program_id(0)
    n = pl.num_programs(0)
    tm = x1_ref.shape[0]

    def row_copy(tile, j, slot, par):
        return pltpu.make_async_copy(_slab(yrows_hbm, dest_ref[2 * (tile * tm + j) + slot]),
                                     _slab(rows_ref.at[par, slot], j), row_sems.at[par])

    def wait_tile(par):
        pltpu.make_async_copy(rows_ref.at[par], rows_ref.at[par], row_sems.at[par]).wait()

    @pl.when(i == 0)
    def _():
        def body(j, carry):
            row_copy(0, j, 0, 0).start(priority=0)
            row_copy(0, j, 1, 0).start(priority=1)
            return carry
        lax.fori_loop(0, tm, body, 0, unroll=8)

    def step(par):
        wait_tile(par)
        nxt = jnp.minimum(i + 1, n - 1)
        for j in range(tm):
            row_copy(nxt, j, 0, 1 - par).start(priority=0)
            row_copy(nxt, j, 1, 1 - par).start(priority=1)
        def rows_chain(r0, nr):
            rows = slice(r0, r0 + nr)
            pp = jnp.dot(p_ref[rows, :].astype(BF16), wpp_ref[...], preferred_element_type=F32)
            route = route_ref[rows, :]
            x2 = (x1_ref[rows, :] + route[:, 4:5] * _load_slabs(rows_ref.at[par, 0], nr, r0)
                  + route[:, 5:6] * _load_slabs(rows_ref.at[par, 1], nr, r0))
            yield
            h = _rms(x2, lnp_ref[...]).astype(BF16)
            yield
            gate = _sigmoid(jnp.dot(h, wpg_ref[...], preferred_element_type=F32))
            yield
            x3 = x2 + gate * pp
            o_ref[rows, :] = _rms(x3, lnf_ref[...])
            yield

        part = tm // FINAL_ROW_PARTS
        for _ in zip(*[rows_chain(j * part, part) for j in range(FINAL_ROW_PARTS)]):
            pass

        @pl.when(i == n - 1)
        def _():
            wait_tile(1 - par)

    for par in range(2):
        pl.when(i % 2 == par)(functools.partial(step, par))


def _final(dest, yrows, x1, route, p2, ln_ple, wpg, wpp, ln_final):
    T, D = x1.shape
    tm = GATHER_TILE
    row = lambda n: pl.BlockSpec((tm, n), lambda i, d: (i, 0))
    full = lambda a: pl.BlockSpec(a.shape, lambda i, d: (0,) * a.ndim)
    ln_ple = ln_ple.reshape(1, D)
    ln_final = ln_final.reshape(1, D)
    return pl.pallas_call(
        _final_kernel,
        out_shape=jax.ShapeDtypeStruct((T, D), F32),
        grid_spec=pltpu.PrefetchScalarGridSpec(
            num_scalar_prefetch=1,
            grid=(T // tm,),
            in_specs=[pl.BlockSpec(memory_space=pl.ANY),
                      row(D), row(LANES), row(PLE_DIM), full(ln_ple), full(wpg), full(wpp), full(ln_final)],
            out_specs=row(D),
            scratch_shapes=[pltpu.VMEM((2, 2, tm * SLAB_ROWS, LANES), F32), pltpu.SemaphoreType.DMA((2,))],
        ),
        compiler_params=_params("arbitrary"),
    )(dest, yrows, x1, route, p2, ln_ple, wpg, wpp, ln_final)


def kernel(x, p, positions, ln_mix, w_in, mu_shift, w0, w_decay_up, a0, w_aaa_up, w_gate_up, k_k, k_a, r_k, ln_x_w, ln_x_b, sinks, w_branch_att, w_branch_rwkv, w_out, ln_moe, w_group, b_group, w_expert, b_expert, w_gate_e, w_up_e, w_down_e, ln_ple, w_ple_gate, w_ple_proj, ln_final):
    B, S, D = x.shape
    T = B * S
    depth = w_in.shape[0]
    assert D == D_MODEL and S % ROW_TILE == 0 and S % (WINDOW * ATT_BLOCKS_PER_STEP) == 0 and S % (RWKV_CHUNK * RWKV_CHUNKS_PER_STEP) == 0
    assert T % GATHER_TILE == 0 and (2 * T) % EXPERT_BLOCK == 0
    cos, sin = _rope_tables(positions)
    x2 = x.reshape(T, D)
    out = None
    for i in range(depth):
        q, k, v, zr, gates = _inproj(x2, ln_mix[i], w_in[i].astype(BF16), mu_shift[i], cos, sin, S)
        y_att = _attention(q, k, v, sinks[i], B, S)
        y_rwkv = _rwkv(zr, w0[i], w_decay_up[i], a0[i], w_aaa_up[i], w_gate_up[i], k_k[i], k_a[i], r_k[i],
                       ln_x_w[i], ln_x_b[i], B, S)
        pad = LANES - N_GROUPS - N_EXPERTS
        w_router = jnp.concatenate([w_group[i], w_expert[i], jnp.zeros((D, pad), F32)], axis=1)
        w_router_hi = w_router.astype(BF16)
        w_router = jnp.concatenate([w_router_hi, (w_router - w_router_hi.astype(F32)).astype(BF16)], axis=1)
        b_router = jnp.concatenate([b_group[i], b_expert[i], jnp.zeros((pad,), F32)]).reshape(1, LANES)
        x1, h2, route, cnt = _merge(x2, y_att, y_rwkv, gates, w_branch_att[i].astype(BF16),
                                    w_branch_rwkv[i].astype(BF16), w_out[i].astype(BF16), ln_moe[i],
                                    w_router, b_router)
        counts = cnt[0, :N_EXPERTS].astype(jnp.int32)
        items, starts = _work_items(counts, 2 * T)
        expert = route[:, 0:2].astype(jnp.int32)
        first_row = jnp.sum(jnp.where(expert[..., None] == jnp.arange(N_EXPERTS, dtype=jnp.int32), starts, 0), axis=-1)
        dest = (first_row + route[:, 2:4].astype(jnp.int32)).reshape(-1)
        xrows = _dispatch(h2, dest)
        yrows = _experts(xrows, items, w_gate_e[i], w_up_e[i], w_down_e[i])
        last = i == depth - 1
        assert last, "the final-norm kernel closes the only layer"
        out = _final(dest, yrows, x1, route, p[i].reshape(T, PLE_DIM), ln_ple[i], w_ple_gate[i].astype(BF16),
                     w_ple_proj[i].astype(BF16), ln_final)
    return out.reshape(B, S, D)
```

```python
import functools
import math

import jax
import jax.numpy as jnp
from jax import lax
from jax.experimental import pallas as pl
from jax.experimental.pallas import tpu as pltpu

F32 = jnp.float32
BF16 = jnp.bfloat16
HIGHEST = lax.Precision.HIGHEST

D_MODEL = 1024
PLE_DIM = 256
ATT_HEADS = 8
ATT_KV_HEADS = 2
HEAD_DIM = 64
ATT_WIDTH = ATT_HEADS * HEAD_DIM
KV_WIDTH = ATT_KV_HEADS * HEAD_DIM
WINDOW = 128
ROPE_THETA = 10000.0
RWKV_HEADS = 8
RWKV_HEAD = 64
RWKV_WIDTH = RWKV_HEADS * RWKV_HEAD
DECAY_LORA = 64
AAA_LORA = 64
GATE_LORA = 128
RWKV_GN_EPS = 64e-5
ATT_COLS = ATT_WIDTH + 2 * KV_WIDTH
SHIFT_COLS = 3 * RWKV_WIDTH + DECAY_LORA + AAA_LORA + GATE_LORA
GATE_COLS = 2 * D_MODEL
N_GROUPS = 4
EXPERTS_PER_GROUP = 8
N_EXPERTS = N_GROUPS * EXPERTS_PER_GROUP
D_EXPERT = 512
NORM_EPS = 1e-6
NEG_INF = -1e30

LANES = 128
ROUTE_COLS = 8
SLAB_ROWS = D_MODEL // LANES
VMEM_LIMIT = 56 * 1024 * 1024

ROW_TILE = 512
MERGE_TILE = 1024
MERGE_ROW_PARTS = 2
ATT_BLOCKS_PER_STEP = 4
RWKV_CHUNK = 64
RWKV_CHUNKS_PER_STEP = 8
RWKV_PACK = 2
RWKV_STAGE_GROUP = 4
EXPERT_BLOCK = 1024
GATHER_TILE = 256
DISPATCH_TILE = 512


def _params(*sem):
    return pltpu.CompilerParams(dimension_semantics=sem, vmem_limit_bytes=VMEM_LIMIT)


def _bdot(a, b):
    return jnp.dot(a.astype(BF16), b.astype(BF16), preferred_element_type=F32)


def _bdot_nt(a, b):
    return lax.dot_general(a.astype(BF16), b.astype(BF16), (((1,), (1,)), ((), ())),
                           preferred_element_type=F32)


def _fdot(a, b):
    return jnp.dot(a, b, preferred_element_type=F32, precision=HIGHEST)


def _rms(x, g):
    return x * lax.rsqrt(jnp.mean(x * x, axis=-1, keepdims=True) + NORM_EPS) * g


def _sigmoid(x):
    return 1.0 / (1.0 + jnp.exp(-x))


def _store_slabs(ref, val, first=0):
    m = val.shape[0]
    for j in range(SLAB_ROWS):
        ref[pl.ds(first * SLAB_ROWS + j, m, stride=SLAB_ROWS), :] = val[:, j * LANES:(j + 1) * LANES]


def _load_slabs(ref, m, first=0):
    return jnp.concatenate([ref[pl.ds(first * SLAB_ROWS + j, m, stride=SLAB_ROWS), :] for j in range(SLAB_ROWS)],
                           axis=1)


def _slab(ref, index):
    return ref.at[pl.ds(pl.multiple_of(index * SLAB_ROWS, SLAB_ROWS), SLAB_ROWS)]


def _rope_table_kernel(pos_ref, invf_ref, cos_ref, sin_ref):
    half = HEAD_DIM // 2
    per_row = LANES // half
    rows = pos_ref.shape[0]
    ang = pos_ref[...].astype(F32) * invf_ref[...]
    group = lax.broadcasted_iota(jnp.int32, ang.shape, 1) // half
    sign = jnp.where(group % 2 == 0, -1.0, 1.0)
    for table, out_ref, scale in ((jnp.cos(ang), cos_ref, None), (jnp.sin(ang), sin_ref, sign)):
        rolled = [table] + [pltpu.roll(table, half * j, 1) for j in range(1, per_row)]
        for m in range(per_row):
            out = rolled[(per_row - 1 - m) % per_row]
            for g in range(per_row - 1):
                out = jnp.where(group == g, rolled[(g - m) % per_row], out)
            out_ref[pl.ds(m, rows, stride=per_row), :] = out if scale is None else out * scale


def _rope_tables(positions):
    T = positions.size
    half = HEAD_DIM // 2
    per_row = LANES // half
    inv_freq = ROPE_THETA ** (-jnp.arange(half, dtype=F32) / half)
    invf = jnp.tile(inv_freq, per_row).reshape(1, LANES)
    pos = jnp.repeat(positions.reshape(T // per_row, per_row), half, axis=1)
    rows = T // per_row
    tr = min(rows, 1024)
    return pl.pallas_call(
        _rope_table_kernel,
        out_shape=(jax.ShapeDtypeStruct((T, LANES), F32),) * 2,
        grid=(rows // tr,),
        in_specs=[pl.BlockSpec((tr, LANES), lambda i: (i, 0)),
                  pl.BlockSpec((1, LANES), lambda i: (0, 0))],
        out_specs=(pl.BlockSpec((tr * per_row, LANES), lambda i: (i, 0)),) * 2,
        compiler_params=_params("arbitrary"),
    )(pos, invf)


def _rope(t, cos, sin):
    n = t.shape[1]
    reps = n // LANES
    c = jnp.tile(cos, (1, reps)) if reps > 1 else cos
    s = jnp.tile(sin, (1, reps)) if reps > 1 else sin
    lane = lax.broadcasted_iota(jnp.int32, t.shape, 1)
    first_half = (lane % HEAD_DIM) < (HEAD_DIM // 2)
    partner = jnp.where(first_half, pltpu.roll(t, n - HEAD_DIM // 2, 1), pltpu.roll(t, HEAD_DIM // 2, 1))
    return t * c + partner * s


def _inproj_kernel(x_ref, g_ref, w_ref, mu_ref, cos_ref, sin_ref,
                   q_ref, k_ref, v_ref, zr_ref, gate_ref, carry_ref, *, tiles_per_seq):
    i = pl.program_id(0)
    tm = x_ref.shape[0]
    h = _rms(x_ref[...], g_ref[...]).astype(BF16)
    cos = cos_ref[...]
    sin = sin_ref[...]

    za = jnp.dot(h, w_ref[:, :ATT_COLS], preferred_element_type=F32)
    q_ref[...] = (_rope(za[:, :ATT_WIDTH], cos, sin) * (HEAD_DIM ** -0.5)).astype(BF16)

    def per_head_doubled(t):
        lane = lax.broadcasted_iota(jnp.int32, t.shape, 1)
        swapped = pltpu.roll(t, HEAD_DIM, 1)
        low = lane < HEAD_DIM
        return jnp.concatenate([jnp.where(low, t, swapped), jnp.where(low, swapped, t)], axis=1).astype(BF16)

    k_ref[...] = per_head_doubled(_rope(za[:, ATT_WIDTH:ATT_WIDTH + KV_WIDTH], cos, sin))
    v_ref[...] = per_head_doubled(za[:, ATT_WIDTH + KV_WIDTH:])

    zs = jnp.dot(h, w_ref[:, ATT_COLS:ATT_COLS + SHIFT_COLS], preferred_element_type=F32)
    row = lax.broadcasted_iota(jnp.int32, zs.shape, 0)
    seq_start = (i % tiles_per_seq) == 0
    before = jnp.where(seq_start, 0.0, carry_ref[0:1, :])
    prev = jnp.where(row == 0, before, pltpu.roll(zs, 1, 0))
    carry_ref[0:1, :] = zs[tm - 1:tm, :]
    zr_ref[...] = zs + (prev - zs) * mu_ref[...]

    zg = jnp.dot(h, w_ref[:, ATT_COLS + SHIFT_COLS:], preferred_element_type=F32)
    gate_ref[...] = _sigmoid(zg).astype(BF16)


def _inproj(x2, ln, w_in, mu, cos, sin, seq):
    T, D = x2.shape
    tm = ROW_TILE
    row = lambda n: pl.BlockSpec((tm, n), lambda i: (i, 0))
    full = lambda a: pl.BlockSpec(a.shape, lambda i: (0,) * a.ndim)
    resident = lambda a: pl.BlockSpec(a.shape, lambda i: (0,) * a.ndim, pipeline_mode=pl.Buffered(1))
    ln = ln.reshape(1, D)
    mu = mu.reshape(1, SHIFT_COLS)
    return pl.pallas_call(
        functools.partial(_inproj_kernel, tiles_per_seq=seq // tm),
        out_shape=(jax.ShapeDtypeStruct((T, ATT_WIDTH), BF16),
                   jax.ShapeDtypeStruct((T, 2 * KV_WIDTH), BF16),
                   jax.ShapeDtypeStruct((T, 2 * KV_WIDTH), BF16),
                   jax.ShapeDtypeStruct((T, SHIFT_COLS), F32),
                   jax.ShapeDtypeStruct((T, GATE_COLS), BF16)),
        grid=(T // tm,),
        in_specs=[row(D), full(ln), resident(w_in), full(mu), row(LANES), row(LANES)],
        out_specs=(row(ATT_WIDTH), row(2 * KV_WIDTH), row(2 * KV_WIDTH), row(SHIFT_COLS), row(GATE_COLS)),
        scratch_shapes=[pltpu.VMEM((8, SHIFT_COLS), F32)],
        compiler_params=_params("arbitrary"),
    )(x2, ln, w_in, mu, cos, sin)


def _attn_kernel(sink_ref, q_ref, kp_ref, kc_ref, vp_ref, vc_ref, o_ref):
    n = pl.program_id(1)
    bq = WINDOW
    grp = ATT_HEADS // ATT_KV_HEADS
    gw = grp * HEAD_DIM
    kall = jnp.concatenate([kp_ref[...], kc_ref[...]], axis=0)
    vall = jnp.concatenate([vp_ref[...], vc_ref[...]], axis=0)
    qi = lax.broadcasted_iota(jnp.int32, (grp * bq, 2 * bq), 0) % bq
    si = lax.broadcasted_iota(jnp.int32, (grp * bq, 2 * bq), 1)
    diff = qi + bq - si
    band = (diff >= 0) & (diff < WINDOW)
    head_row = lax.broadcasted_iota(jnp.int32, (grp * bq, 1), 0) // bq
    lane_head = lax.broadcasted_iota(jnp.int32, (bq, gw), 1) // HEAD_DIM
    zero = jnp.zeros((), BF16)
    for blk in range(q_ref.shape[0] // bq):
        valid = band & ((si >= bq) | (n > 0)) if blk == 0 else band
        outs = []
        for j in range(ATT_KV_HEADS):
            qg = q_ref[blk * bq:(blk + 1) * bq, j * gw:(j + 1) * gw]
            lhs = jnp.concatenate([jnp.where(lane_head == g, qg, zero) for g in range(grp)], axis=0)
            k_half = kall[blk * bq:(blk + 2) * bq, j * 2 * HEAD_DIM:(j + 1) * 2 * HEAD_DIM]
            v_half = vall[blk * bq:(blk + 2) * bq, j * 2 * HEAD_DIM:(j + 1) * 2 * HEAD_DIM]
            k_rep = jnp.concatenate([k_half, k_half], axis=1)
            v_rep = jnp.concatenate([v_half, v_half], axis=1)
            s = lax.dot_general(lhs, k_rep, (((1,), (1,)), ((), ())), preferred_element_type=F32)
            s = jnp.where(valid, s, NEG_INF)
            sink = jnp.zeros((grp * bq, 1), F32)
            for g in range(grp):
                sink = jnp.where(head_row == g, sink_ref[j * grp + g], sink)
            m = jnp.maximum(jnp.max(s, axis=-1, keepdims=True), sink)
            e = jnp.exp(s - m)
            denom = jnp.sum(e, axis=-1, keepdims=True) + jnp.exp(sink - m)
            pv = jnp.dot(e.astype(BF16), v_rep, preferred_element_type=F32) / denom
            out = pv[:bq]
            for g in range(1, grp):
                out = jnp.where(lane_head == g, pv[g * bq:(g + 1) * bq], out)
            outs.append(out)
        o_ref[blk * bq:(blk + 1) * bq, :] = jnp.concatenate(outs, axis=1).astype(o_ref.dtype)


def _attention(q, k, v, sinks, batch, seq):
    T = q.shape[0]
    bq = WINDOW
    per_step = ATT_BLOCKS_PER_STEP
    ns = seq // (bq * per_step)
    cur = lambda b, n: (b * ns + n, 0)
    prev = lambda b, n: (jnp.maximum((b * ns + n) * per_step - 1, 0), 0)
    return pl.pallas_call(
        _attn_kernel,
        out_shape=jax.ShapeDtypeStruct((T, ATT_WIDTH), BF16),
        grid=(batch, ns),
        in_specs=[pl.BlockSpec(memory_space=pltpu.SMEM),
                  pl.BlockSpec((bq * per_step, ATT_WIDTH), cur),
                  pl.BlockSpec((bq, 2 * KV_WIDTH), prev), pl.BlockSpec((bq * per_step, 2 * KV_WIDTH), cur),
                  pl.BlockSpec((bq, 2 * KV_WIDTH), prev), pl.BlockSpec((bq * per_step, 2 * KV_WIDTH), cur)],
        out_specs=pl.BlockSpec((bq * per_step, ATT_WIDTH), cur),
        compiler_params=_params("arbitrary", "arbitrary"),
    )(sinks, q, k, k, v, v)


def _rwkv_kernel(z_ref, w0_ref, wd_ref, a0_ref, wa_ref, wg_ref, kk_ref, ka_ref, rk_ref, lnw_ref, lnb_ref,
                 y_ref, s_ref):
    c = pl.program_id(1)

    @pl.when(c == 0)
    def _():
        s_ref[...] = jnp.zeros_like(s_ref)

    states = _rwkv_rows(z_ref, w0_ref, wd_ref, a0_ref, wa_ref, wg_ref, kk_ref, ka_ref, rk_ref, lnw_ref, lnb_ref,
                        y_ref, [s_ref[pj] for pj in range(s_ref.shape[0])])
    for pj, state in enumerate(states):
        s_ref[pj] = state


def _rwkv_rows(z_ref, w0_ref, wd_ref, a0_ref, wa_ref, wg_ref, kk_ref, ka_ref, rk_ref, lnw_ref, lnb_ref, y_ref, states):
    L = RWKV_CHUNK
    rows = z_ref.shape[0]
    n_chunks = rows // L
    C = RWKV_WIDTH
    N = RWKV_HEAD
    r = z_ref[:, 0:C]
    k = z_ref[:, C:2 * C]
    v = z_ref[:, 2 * C:3 * C]
    xw = z_ref[:, 3 * C:3 * C + DECAY_LORA]
    xa = z_ref[:, 3 * C + DECAY_LORA:3 * C + DECAY_LORA + AAA_LORA]
    xg = z_ref[:, 3 * C + DECAY_LORA + AAA_LORA:]

    wlin = w0_ref[...] + _bdot(jnp.tanh(xw), wd_ref[...])
    logdecay = -math.exp(-0.5) * _sigmoid(wlin)
    a = _sigmoid(a0_ref[...] + _bdot(xa, wa_ref[...]))
    g = _bdot(_sigmoid(xg), wg_ref[...])
    kk = k * kk_ref[...]
    k2 = k * (1.0 + (a - 1.0) * ka_ref[...])

    slab = 2 * LANES
    hr = lax.broadcasted_iota(jnp.int32, (slab, slab), 0) // N
    hc = lax.broadcasted_iota(jnp.int32, (slab, slab), 1) // N
    head_ones = jnp.where(hr == hc, 1.0, 0.0).astype(BF16)

    def head_sum(t):
        t = t.astype(BF16)
        return jnp.concatenate([jnp.dot(t[:, j * slab:(j + 1) * slab], head_ones, preferred_element_type=F32)
                                for j in range(C // slab)], axis=1)

    kkn = kk / jnp.maximum(jnp.sqrt(head_sum(kk * kk)), 1e-12)

    assert L == N and (RWKV_PACK * N) % LANES == 0, "the packed block-diagonal products need chunk == head size"
    row = lax.broadcasted_iota(jnp.int32, (L, L), 0)
    col = lax.broadcasted_iota(jnp.int32, (L, L), 1)
    tri = jnp.where(row >= col, 1.0, 0.0).astype(BF16)
    ld_1 = logdecay.astype(BF16)
    ld_2 = (logdecay - ld_1.astype(F32)).astype(BF16)
    cums = []
    for ci in range(n_chunks):
        rs = slice(ci * L, (ci + 1) * L)
        cums.append(jnp.dot(tri, ld_1[rs], preferred_element_type=F32)
                    + jnp.dot(tri, ld_2[rs], preferred_element_type=F32))
    cum = jnp.concatenate(cums, axis=0) if n_chunks > 1 else cums[0]
    last = [cums[ci][L - 1:L, :] for ci in range(n_chunks)]
    cum_last = jnp.concatenate([jnp.broadcast_to(t, (L, C)) for t in last], axis=0) if n_chunks > 1 \
        else jnp.broadcast_to(last[0], (L, C))
    p_in = jnp.exp(cum)
    p_inv = jnp.exp(-cum)
    p_rem = jnp.exp(cum_last - cum)
    b = kkn * a
    at_f = (-kkn * jnp.exp(cum - logdecay)).astype(BF16)
    rt_f = r * p_in
    rt_b = rt_f.astype(BF16)
    bt_b = (b * p_inv).astype(BF16)
    kt_b = (k2 * p_inv).astype(BF16)
    br_b = (b * p_rem).astype(BF16)
    kr_b = (k2 * p_rem).astype(BF16)
    v_b = v.astype(BF16)

    dot = lambda x, y: jnp.dot(x.astype(BF16), y.astype(BF16), preferred_element_type=F32)
    dot_nt = lambda x, y: lax.dot_general(x.astype(BF16), y.astype(BF16), (((1,), (1,)), ((), ())),
                                          preferred_element_type=F32)
    dot_tn = lambda x, y: lax.dot_general(x.astype(BF16), y.astype(BF16), (((0,), (0,)), ((), ())),
                                          preferred_element_type=F32)

    PW = RWKV_PACK * N
    n_groups = C // PW
    lane_head = lax.broadcasted_iota(jnp.int32, (L, PW), 1) // N
    pcol = lax.broadcasted_iota(jnp.int32, (L, PW), 1) - lane_head * N
    prow = lax.broadcasted_iota(jnp.int32, (L, PW), 0)
    p_lower = prow > pcol
    p_lower_eq = prow >= pcol
    p_eye = jnp.where(prow == pcol, 1.0, 0.0)
    zero = jnp.zeros((), BF16)

    def bdiag(t):
        t = t.astype(BF16)
        return jnp.concatenate([jnp.where(lane_head == h, t, zero) for h in range(RWKV_PACK)], axis=0)

    def pack_rows(t):
        out = t[:N]
        for h in range(1, RWKV_PACK):
            out = jnp.where(lane_head == h, t[h * N:(h + 1) * N], out)
        return out

    cut = lambda t, ci, pj: t[ci * L:(ci + 1) * L, pj * PW:(pj + 1) * PW]
    same_head = (lax.broadcasted_iota(jnp.int32, (PW, PW), 0) // N) == \
                (lax.broadcasted_iota(jnp.int32, (PW, PW), 1) // N)
    rq, y0, gg, hh = {}, {}, {}, {}
    for g0 in range(0, n_chunks, RWKV_STAGE_GROUP):
        pieces = [(ci, pj) for ci in range(g0, min(g0 + RWKV_STAGE_GROUP, n_chunks)) for pj in range(n_groups)]
        _rwkv_chunk_algebra(pieces, cut, bdiag, dot, dot_nt, dot_tn, L, PW, at_f, rt_f, rt_b, bt_b, kt_b, br_b, kr_b,
                            v_b, p_lower, p_lower_eq, p_eye, same_head, pack_rows, rq, y0, gg, hh)

    ys = {}
    new_states = []
    for pj in range(n_groups):
        state = states[pj]
        for ci in range(n_chunks):
            p = (ci, pj)
            sb = state.astype(BF16)
            ys[p] = y0[p] + dot_nt(rq[p], bdiag(sb))
            p_tot = p_in[(ci + 1) * L - 1:(ci + 1) * L, pj * PW:(pj + 1) * PW]
            state = state * p_tot + dot(sb, gg[p]) + hh[p]
        new_states.append(state)
    y = jnp.concatenate([jnp.concatenate([ys[(ci, pj)] for pj in range(n_groups)], axis=1)
                         for ci in range(n_chunks)], axis=0)

    mu = head_sum(y) * (1.0 / N)
    dev = y - mu
    var = head_sum(dev * dev) * (1.0 / N)
    yn = dev * lax.rsqrt(var + RWKV_GN_EPS)
    bonus = head_sum(r * k2 * rk_ref[...]) * v
    y_ref[...] = ((yn * lnw_ref[...] + lnb_ref[...] + bonus) * g).astype(y_ref.dtype)
    return new_states


def _rwkv_chunk_algebra(pieces, cut, bdiag, dot, dot_nt, dot_tn, L, PW, at_f, rt_f, rt_b, bt_b, kt_b, br_b, kr_b, v_b,
                        p_lower, p_lower_eq, p_eye, same_head, pack_rows, rq_out, y0_out, gg_out, hh_out):
    at = {p: cut(at_f, *p) for p in pieces}
    vv = {p: cut(v_b, *p) for p in pieces}
    m = {p: dot_nt(jnp.concatenate([at[p], cut(rt_b, *p)], axis=0),
                   jnp.concatenate([bdiag(cut(bt_b, *p)), bdiag(cut(kt_b, *p))], axis=0)) for p in pieces}
    m_ab = {p: jnp.where(p_lower, m[p][:L, :PW], 0.0) for p in pieces}
    m_ak = {p: jnp.where(p_lower, m[p][:L, PW:], 0.0).astype(BF16) for p in pieces}
    m_rb = {p: jnp.where(p_lower_eq, m[p][L:, :PW], 0.0).astype(BF16) for p in pieces}
    m_rk = {p: jnp.where(p_lower_eq, m[p][L:, PW:], 0.0).astype(BF16) for p in pieces}
    inv = {p: p_eye + m_ab[p] for p in pieces}
    pw = {p: m_ab[p].astype(BF16) for p in pieces}
    pw = {p: dot(pw[p], bdiag(pw[p])).astype(BF16) for p in pieces}
    for _ in range(max(2, (L - 1).bit_length()) - 2):
        both = {p: dot(jnp.concatenate([pw[p], inv[p].astype(BF16)], axis=0), bdiag(pw[p])) for p in pieces}
        pw = {p: both[p][:L].astype(BF16) for p in pieces}
        inv = {p: inv[p] + both[p][L:] for p in pieces}
    inv = {p: (inv[p] + dot(inv[p], bdiag(pw[p]))).astype(BF16) for p in pieces}
    mv = {p: dot(jnp.concatenate([m_ak[p], m_rk[p]], axis=0), bdiag(vv[p])) for p in pieces}
    wu = {p: dot(inv[p], jnp.concatenate([bdiag(at[p]), bdiag(mv[p][:L])], axis=1)) for p in pieces}
    w = {p: wu[p][:, :PW].astype(BF16) for p in pieces}
    u0 = {p: wu[p][:, PW:].astype(BF16) for p in pieces}
    ry = {p: dot(m_rb[p], jnp.concatenate([bdiag(w[p]), bdiag(u0[p])], axis=1)) for p in pieces}
    rq_out.update({p: (cut(rt_f, *p) + ry[p][:, :PW]).astype(BF16) for p in pieces})
    y0_out.update({p: ry[p][:, PW:] + mv[p][L:] for p in pieces})
    gg_out.update({p: jnp.where(same_head, dot_tn(w[p], cut(br_b, *p)), 0.0).astype(BF16) for p in pieces})
    hh_full = {p: dot_tn(jnp.concatenate([u0[p], vv[p]], axis=0),
                         jnp.concatenate([cut(br_b, *p), cut(kr_b, *p)], axis=0)) for p in pieces}
    hh_out.update({p: pack_rows(hh_full[p]) for p in pieces})


def _rwkv(zr, w0, wd, a0, wa, wg, k_k, k_a, r_k, ln_w, ln_b, batch, seq):
    T = zr.shape[0]
    L = RWKV_CHUNK * RWKV_CHUNKS_PER_STEP
    nc = seq // L
    vec = lambda a: a.reshape(1, RWKV_WIDTH)
    full = lambda a: pl.BlockSpec(a.shape, lambda b, c: (0,) * a.ndim)
    args = (vec(w0), wd, vec(a0), wa, wg, vec(k_k), vec(k_a), vec(r_k), vec(ln_w), vec(ln_b))
    return pl.pallas_call(
        _rwkv_kernel,
        out_shape=jax.ShapeDtypeStruct((T, RWKV_WIDTH), BF16),
        grid=(batch, nc),
        in_specs=[pl.BlockSpec((L, SHIFT_COLS), lambda b, c: (b * nc + c, 0))] + [full(a) for a in args],
        out_specs=pl.BlockSpec((L, RWKV_WIDTH), lambda b, c: (b * nc + c, 0)),
        scratch_shapes=[pltpu.VMEM((RWKV_HEADS // RWKV_PACK, RWKV_HEAD, RWKV_PACK * RWKV_HEAD), F32)],
        compiler_params=_params("arbitrary", "arbitrary"),
    )(zr, *args)


def _merge_kernel(x_ref, ya_ref, yr_ref, gate_ref, wba_ref, wbr_ref, wo_ref, lnm_ref, wr_ref, br_ref,
                  x1_ref, h2_ref, route_ref, cnt_ref, cnt_scr):
    i = pl.program_id(0)

    @pl.when(i == 0)
    def _():
        cnt_scr[...] = jnp.zeros_like(cnt_scr)

    part = x_ref.shape[0] // MERGE_ROW_PARTS
    chains = [_merge_rows(x_ref, ya_ref, yr_ref, gate_ref, wba_ref, wbr_ref, wo_ref, lnm_ref, wr_ref, br_ref,
                          x1_ref, h2_ref, route_ref, cnt_ref, cnt_scr, j * part, part)
              for j in range(MERGE_ROW_PARTS)]
    for _ in zip(*chains):
        pass


def _merge_rows(x_ref, ya_ref, yr_ref, gate_ref, wba_ref, wbr_ref, wo_ref, lnm_ref, wr_ref, br_ref,
                x1_ref, h2_ref, route_ref, cnt_ref, cnt_scr, r0, tm):
    D = D_MODEL
    rows = slice(r0, r0 + tm)
    gates = gate_ref[rows, :].astype(F32)
    ya = jnp.dot(ya_ref[rows, :], wba_ref[...], preferred_element_type=F32)
    yr = jnp.dot(yr_ref[rows, :], wbr_ref[...], preferred_element_type=F32)
    yield
    merged = gates[:, :D] * ya + gates[:, D:] * yr
    x1 = x_ref[rows, :] + jnp.dot(merged.astype(BF16), wo_ref[...], preferred_element_type=F32)
    x1_ref[rows, :] = x1
    yield
    h2 = _rms(x1, lnm_ref[...])
    _store_slabs(h2_ref, h2, r0)

    h_hi = h2.astype(BF16)
    h_lo = (h2 - h_hi.astype(F32)).astype(BF16)
    parts = (jnp.dot(h_hi, wr_ref[...], preferred_element_type=F32)
             + jnp.dot(h_lo, wr_ref[...], preferred_element_type=F32))
    yield
    logits = parts[:, :LANES] + parts[:, LANES:] + br_ref[...]
    lane = lax.broadcasted_iota(jnp.int32, logits.shape, 1)
    big = jnp.int32(1 << 20)
    gl = jnp.where(lane < N_GROUPS, logits, NEG_INF)
    gmax = jnp.max(gl, axis=-1, keepdims=True)
    yield
    gidx = jnp.min(jnp.where(gl == gmax, lane, big), axis=-1, keepdims=True)
    gsum = jnp.sum(jnp.where(lane < N_GROUPS, jnp.exp(logits - gmax), 0.0), axis=-1, keepdims=True)
    yield
    g_w = 1.0 / gsum
    lo = N_GROUPS + EXPERTS_PER_GROUP * gidx
    el = jnp.where((lane >= lo) & (lane < lo + EXPERTS_PER_GROUP), logits, NEG_INF)
    m1 = jnp.max(el, axis=-1, keepdims=True)
    yield
    i1 = jnp.min(jnp.where(el == m1, lane, big), axis=-1, keepdims=True)
    yield
    el2 = jnp.where(lane == i1, NEG_INF, el)
    m2 = jnp.max(el2, axis=-1, keepdims=True)
    yield
    i2 = jnp.min(jnp.where(el2 == m2, lane, big), axis=-1, keepdims=True)
    yield
    d = jnp.exp(m2 - m1)
    w1 = g_w / (1.0 + d)
    w2 = g_w * d / (1.0 + d)
    e1 = i1 - N_GROUPS
    e2 = i2 - N_GROUPS

    hit1 = lane == e1
    hit2 = lane == e2
    onehot = jnp.where(hit1 | hit2, 1.0, 0.0).astype(BF16)
    r_i = lax.broadcasted_iota(jnp.int32, (tm, tm), 0)
    c_i = lax.broadcasted_iota(jnp.int32, (tm, tm), 1)
    before = jnp.dot(jnp.where(r_i > c_i, 1.0, 0.0).astype(BF16), onehot, preferred_element_type=F32)
    yield
    before = before + cnt_scr[...]
    rank1 = jnp.sum(jnp.where(hit1, before, 0.0), axis=-1, keepdims=True)
    rank2 = jnp.sum(jnp.where(hit2, before, 0.0), axis=-1, keepdims=True)
    cnt_scr[...] = cnt_scr[...] + jnp.sum(onehot.astype(F32), axis=0, keepdims=True)
    cnt_ref[...] = cnt_scr[...]

    route = jnp.where(lane == 0, e1.astype(F32), 0.0)
    route = jnp.where(lane == 1, e2.astype(F32), route)
    route = jnp.where(lane == 2, rank1, route)
    route = jnp.where(lane == 3, rank2, route)
    route = jnp.where(lane == 4, w1, route)
    route = jnp.where(lane == 5, w2, route)
    route_ref[rows, :] = route[:, :ROUTE_COLS]
    yield


def _merge(x2, y_att, y_rwkv, gates, wba, wbr, wo, ln_moe, w_router, b_router):
    T, D = x2.shape
    tm = MERGE_TILE
    row = lambda n: pl.BlockSpec((tm, n), lambda i: (i, 0))
    full = lambda a: pl.BlockSpec(a.shape, lambda i: (0,) * a.ndim)
    ln_moe = ln_moe.reshape(1, D)
    return pl.pallas_call(
        _merge_kernel,
        out_shape=(jax.ShapeDtypeStruct((T, D), F32), jax.ShapeDtypeStruct((T * SLAB_ROWS, LANES), F32),
                   jax.ShapeDtypeStruct((T, ROUTE_COLS), F32), jax.ShapeDtypeStruct((1, LANES), F32)),
        grid=(T // tm,),
        in_specs=[row(D), row(ATT_WIDTH), row(RWKV_WIDTH), row(GATE_COLS),
                  full(wba), full(wbr), full(wo), full(ln_moe), full(w_router), full(b_router)],
        out_specs=(row(D), pl.BlockSpec((tm * SLAB_ROWS, LANES), lambda i: (i, 0)), row(ROUTE_COLS),
                   pl.BlockSpec((1, LANES), lambda i: (0, 0))),
        scratch_shapes=[pltpu.VMEM((1, LANES), F32)],
        compiler_params=_params("arbitrary"),
    )(x2, y_att, y_rwkv, gates, wba, wbr, wo, ln_moe, w_router, b_router)


def _dispatch_kernel(dest_ref, h_ref, xrows_hbm, row_sem, *, tile):
    base = pl.program_id(0) * tile

    def row_copy(j, slot):
        return pltpu.make_async_copy(_slab(h_ref, j), _slab(xrows_hbm, dest_ref[2 * (base + j) + slot]), row_sem)

    def issue(j, carry):
        row_copy(j, 0).start(priority=0)
        row_copy(j, 1).start(priority=1)
        return carry

    lax.fori_loop(0, tile, issue, 0, unroll=8)

    def drain(j, carry):
        row_copy(j, 0).wait()
        row_copy(j, 1).wait()
        return carry

    lax.fori_loop(0, tile, drain, 0, unroll=8)


def _dispatch(h2_slabs, dest):
    T = h2_slabs.shape[0] // SLAB_ROWS
    tile = min(DISPATCH_TILE, T)
    assert T % tile == 0
    return pl.pallas_call(
        functools.partial(_dispatch_kernel, tile=tile),
        out_shape=jax.ShapeDtypeStruct((2 * T * SLAB_ROWS, LANES), F32),
        grid_spec=pltpu.PrefetchScalarGridSpec(
            num_scalar_prefetch=1,
            grid=(T // tile,),
            in_specs=[pl.BlockSpec((tile * SLAB_ROWS, LANES), lambda i, d: (i, 0))],
            out_specs=pl.BlockSpec(memory_space=pl.ANY),
            scratch_shapes=[pltpu.SemaphoreType.DMA],
        ),
        compiler_params=_params("arbitrary"),
    )(dest, h2_slabs)


def _expert_kernel(item_e, item_b, item_lo, item_hi, x_ref, wg_ref, wu_ref, wd_ref, y_ref, wgu_bf, wd_bf):
    w = pl.program_id(0)
    lo = item_lo[w]
    hi = item_hi[w]

    @pl.when((w == 0) | (item_e[w] != item_e[jnp.maximum(w - 1, 0)]))
    def _():
        wgu_bf[:, :D_EXPERT] = wg_ref[0].astype(BF16)
        wgu_bf[:, D_EXPERT:] = wu_ref[0].astype(BF16)
        wd_bf[...] = wd_ref[0].astype(BF16)

    def run(keep_other_rows):
        bm = x_ref.shape[0] // SLAB_ROWS
        xb = _load_slabs(x_ref, bm).astype(BF16)
        hgu = jnp.dot(xb, wgu_bf[...], preferred_element_type=F32)
        hg = hgu[:, :D_EXPERT]
        hid = hg * _sigmoid(hg) * hgu[:, D_EXPERT:]
        y = jnp.dot(hid.astype(BF16), wd_bf[...], preferred_element_type=F32)
        if keep_other_rows:
            rows = lax.broadcasted_iota(jnp.int32, y.shape, 0)
            y = jnp.where((rows >= lo) & (rows < hi), y, _load_slabs(y_ref, bm))
        _store_slabs(y_ref, y)

    pl.when((hi > lo) & (lo == 0))(functools.partial(run, False))
    pl.when((hi > lo) & (lo > 0))(functools.partial(run, True))


def _experts(xrows, items, wg, wu, wd):
    bm = EXPERT_BLOCK
    D = D_MODEL
    n_items = items[0].shape[0]
    block = pl.BlockSpec((bm * SLAB_ROWS, LANES), lambda w, ie, ib, lo, hi: (ib[w], 0))
    return pl.pallas_call(
        _expert_kernel,
        out_shape=jax.ShapeDtypeStruct(xrows.shape, F32),
        grid_spec=pltpu.PrefetchScalarGridSpec(
            num_scalar_prefetch=4,
            grid=(n_items,),
            in_specs=[block,
                      pl.BlockSpec((1, D, D_EXPERT), lambda w, ie, ib, lo, hi: (ie[w], 0, 0)),
                      pl.BlockSpec((1, D, D_EXPERT), lambda w, ie, ib, lo, hi: (ie[w], 0, 0)),
                      pl.BlockSpec((1, D_EXPERT, D), lambda w, ie, ib, lo, hi: (ie[w], 0, 0))],
            out_specs=block,
            scratch_shapes=[pltpu.VMEM((D, 2 * D_EXPERT), BF16), pltpu.VMEM((D_EXPERT, D), BF16)],
        ),
        compiler_params=_params("arbitrary"),
    )(*items, xrows, wg, wu, wd)


def _work_items(counts, n_rows):
    bm = EXPERT_BLOCK
    nb = n_rows // bm
    n_items = nb + N_EXPERTS
    ends = jnp.cumsum(counts)
    starts = ends - counts
    first_blk = starts // bm
    last_blk = (ends - 1) // bm
    per_e = jnp.where(counts > 0, last_blk - first_blk + 1, 0)
    item_end = jnp.cumsum(per_e)
    total = item_end[-1]
    w = jnp.arange(n_items, dtype=jnp.int32)
    wc = jnp.minimum(w, total - 1)
    e = jnp.minimum(jnp.sum((item_end[None, :] <= wc[:, None]).astype(jnp.int32), axis=1), N_EXPERTS - 1)
    b = first_blk[e] + (wc - (item_end[e] - per_e[e]))
    lo = jnp.maximum(starts[e], b * bm) - b * bm
    hi = jnp.minimum(ends[e], (b + 1) * bm) - b * bm
    live = w < total
    lo = jnp.where(live, lo, 0)
    hi = jnp.where(live, hi, 0)
    return (e.astype(jnp.int32), b.astype(jnp.int32), lo.astype(jnp.int32), hi.astype(jnp.int32)), starts


def _final_kernel(dest_ref, yrows_hbm, x1_ref, route_ref, p_ref, lnp_ref, wpg_ref, wpp_ref, lnf_ref,
                  o_ref, rows_ref, row_sems):
    i = pl.program_id(0)
    n = pl.num_programs(0)
    tm = x1_ref.shape[0]

    def row_copy(tile, j, slot, par):
        return pltpu.make_async_copy(_slab(yrows_hbm, dest_ref[2 * (tile * tm + j) + slot]),
                                     _slab(rows_ref.at[par, slot], j), row_sems.at[par])

    def wait_tile(par):
        pltpu.make_async_copy(rows_ref.at[par], rows_ref.at[par], row_sems.at[par]).wait()

    @pl.when(i == 0)
    def _():
        def body(j, carry):
            row_copy(0, j, 0, 0).start(priority=0)
            row_copy(0, j, 1, 0).start(priority=1)
            return carry
        lax.fori_loop(0, tm, body, 0, unroll=8)

    def step(par):
        wait_tile(par)
        nxt = jnp.minimum(i + 1, n - 1)
        for j in range(tm):
            row_copy(nxt, j, 0, 1 - par).start(priority=0)
            row_copy(nxt, j, 1, 1 - par).start(priority=1)
        pp = jnp.dot(p_ref[...].astype(BF16), wpp_ref[...], preferred_element_type=F32)
        route = route_ref[...]
        x2 = (x1_ref[...] + route[:, 4:5] * _load_slabs(rows_ref.at[par, 0], tm)
              + route[:, 5:6] * _load_slabs(rows_ref.at[par, 1], tm))
        gate = _sigmoid(jnp.dot(_rms(x2, lnp_ref[...]).astype(BF16), wpg_ref[...], preferred_element_type=F32))
        x3 = x2 + gate * pp
        o_ref[...] = _rms(x3, lnf_ref[...])

        @pl.when(i == n - 1)
        def _():
            wait_tile(1 - par)

    for par in range(2):
        pl.when(i % 2 == par)(functools.partial(step, par))


def _final(dest, yrows, x1, route, p2, ln_ple, wpg, wpp, ln_final):
    T, D = x1.shape
    tm = GATHER_TILE
    row = lambda n: pl.BlockSpec((tm, n), lambda i, d: (i, 0))
    full = lambda a: pl.BlockSpec(a.shape, lambda i, d: (0,) * a.ndim)
    ln_ple = ln_ple.reshape(1, D)
    ln_final = ln_final.reshape(1, D)
    return pl.pallas_call(
        _final_kernel,
        out_shape=jax.ShapeDtypeStruct((T, D), F32),
        grid_spec=pltpu.PrefetchScalarGridSpec(
            num_scalar_prefetch=1,
            grid=(T // tm,),
            in_specs=[pl.BlockSpec(memory_space=pl.ANY),
                      row(D), row(ROUTE_COLS), row(PLE_DIM), full(ln_ple), full(wpg), full(wpp), full(ln_final)],
            out_specs=row(D),
            scratch_shapes=[pltpu.VMEM((2, 2, tm * SLAB_ROWS, LANES), F32), pltpu.SemaphoreType.DMA((2,))],
        ),
        compiler_params=_params("arbitrary"),
    )(dest, yrows, x1, route, p2, ln_ple, wpg, wpp, ln_final)


def kernel(x, p, positions, ln_mix, w_in, mu_shift, w0, w_decay_up, a0, w_aaa_up, w_gate_up, k_k, k_a, r_k, ln_x_w, ln_x_b, sinks, w_branch_att, w_branch_rwkv, w_out, ln_moe, w_group, b_group, w_expert, b_expert, w_gate_e, w_up_e, w_down_e, ln_ple, w_ple_gate, w_ple_proj, ln_final):
    B, S, D = x.shape
    T = B * S
    depth = w_in.shape[0]
    assert D == D_MODEL and S % ROW_TILE == 0 and S % (WINDOW * ATT_BLOCKS_PER_STEP) == 0 and S % (RWKV_CHUNK * RWKV_CHUNKS_PER_STEP) == 0
    assert T % GATHER_TILE == 0 and (2 * T) % EXPERT_BLOCK == 0
    cos, sin = _rope_tables(positions)
    x2 = x.reshape(T, D)
    out = None
    for i in range(depth):
        q, k, v, zr, gates = _inproj(x2, ln_mix[i], w_in[i].astype(BF16), mu_shift[i], cos, sin, S)
        y_att = _attention(q, k, v, sinks[i], B, S)
        y_rwkv = _rwkv(zr, w0[i], w_decay_up[i], a0[i], w_aaa_up[i], w_gate_up[i], k_k[i], k_a[i], r_k[i],
                       ln_x_w[i], ln_x_b[i], B, S)
        pad = LANES - N_GROUPS - N_EXPERTS
        w_router = jnp.concatenate([w_group[i], w_expert[i], jnp.zeros((D, pad), F32)], axis=1)
        w_router_hi = w_router.astype(BF16)
        w_router = jnp.concatenate([w_router_hi, (w_router - w_router_hi.astype(F32)).astype(BF16)], axis=1)
        b_router = jnp.concatenate([b_group[i], b_expert[i], jnp.zeros((pad,), F32)]).reshape(1, LANES)
        x1, h2, route, cnt = _merge(x2, y_att, y_rwkv, gates, w_branch_att[i].astype(BF16),
                                    w_branch_rwkv[i].astype(BF16), w_out[i].astype(BF16), ln_moe[i],
                                    w_router, b_router)
        counts = cnt[0, :N_EXPERTS].astype(jnp.int32)
        items, starts = _work_items(counts, 2 * T)
        expert = route[:, 0:2].astype(jnp.int32)
        first_row = jnp.sum(jnp.where(expert[..., None] == jnp.arange(N_EXPERTS, dtype=jnp.int32), starts, 0), axis=-1)
        dest = (first_row + route[:, 2:4].astype(jnp.int32)).reshape(-1)
        xrows = _dispatch(h2, dest)
        yrows = _experts(xrows, items, w_gate_e[i], w_up_e[i], w_down_e[i])
        last = i == depth - 1
        assert last, "the final-norm kernel closes the only layer"
        out = _final(dest, yrows, x1, route, p[i].reshape(T, PLE_DIM), ln_ple[i], w_ple_gate[i].astype(BF16),
                     w_ple_proj[i].astype(BF16), ln_final)
    return out.reshape(B, S, D)
```

```python
import functools
import math

import jax
import jax.numpy as jnp
from jax import lax
from jax.experimental import pallas as pl
from jax.experimental.pallas import tpu as pltpu

F32 = jnp.float32
BF16 = jnp.bfloat16
HIGHEST = lax.Precision.HIGHEST

D_MODEL = 1024
PLE_DIM = 256
ATT_HEADS = 8
ATT_KV_HEADS = 2
HEAD_DIM = 64
ATT_WIDTH = ATT_HEADS * HEAD_DIM
KV_WIDTH = ATT_KV_HEADS * HEAD_DIM
WINDOW = 128
ROPE_THETA = 10000.0
RWKV_HEADS = 8
RWKV_HEAD = 64
RWKV_WIDTH = RWKV_HEADS * RWKV_HEAD
DECAY_LORA = 64
AAA_LORA = 64
GATE_LORA = 128
RWKV_GN_EPS = 64e-5
ATT_COLS = ATT_WIDTH + 2 * KV_WIDTH
SHIFT_COLS = 3 * RWKV_WIDTH + DECAY_LORA + AAA_LORA + GATE_LORA
GATE_COLS = 2 * D_MODEL
N_GROUPS = 4
EXPERTS_PER_GROUP = 8
N_EXPERTS = N_GROUPS * EXPERTS_PER_GROUP
D_EXPERT = 512
NORM_EPS = 1e-6
NEG_INF = -1e30

LANES = 128
ROUTE_COLS = 8
SLAB_ROWS = D_MODEL // LANES
VMEM_LIMIT = 56 * 1024 * 1024

ROW_TILE = 512
MERGE_TILE = 1024
MERGE_ROW_PARTS = 2
ATT_BLOCKS_PER_STEP = 4
RWKV_CHUNK = 64
RWKV_CHUNKS_PER_STEP = 8
RWKV_PACK = 2
RWKV_STAGE_GROUP = 4
EXPERT_BLOCK = 1024
GATHER_TILE = 256
DISPATCH_TILE = 512


def _params(*sem):
    return pltpu.CompilerParams(dimension_semantics=sem, vmem_limit_bytes=VMEM_LIMIT)


def _bdot(a, b):
    return jnp.dot(a.astype(BF16), b.astype(BF16), preferred_element_type=F32)


def _bdot_nt(a, b):
    return lax.dot_general(a.astype(BF16), b.astype(BF16), (((1,), (1,)), ((), ())),
                           preferred_element_type=F32)


def _fdot(a, b):
    return jnp.dot(a, b, preferred_element_type=F32, precision=HIGHEST)


def _rms(x, g):
    return x * lax.rsqrt(jnp.mean(x * x, axis=-1, keepdims=True) + NORM_EPS) * g


def _sigmoid(x):
    return 1.0 / (1.0 + jnp.exp(-x))


def _store_slabs(ref, val, first=0):
    m = val.shape[0]
    for j in range(SLAB_ROWS):
        ref[pl.ds(first * SLAB_ROWS + j, m, stride=SLAB_ROWS), :] = val[:, j * LANES:(j + 1) * LANES]


def _load_slabs(ref, m, first=0):
    return jnp.concatenate([ref[pl.ds(first * SLAB_ROWS + j, m, stride=SLAB_ROWS), :] for j in range(SLAB_ROWS)],
                           axis=1)


def _slab(ref, index):
    return ref.at[pl.ds(pl.multiple_of(index * SLAB_ROWS, SLAB_ROWS), SLAB_ROWS)]


def _rope_table_kernel(pos_ref, invf_ref, cos_ref, sin_ref):
    half = HEAD_DIM // 2
    per_row = LANES // half
    rows = pos_ref.shape[0]
    ang = pos_ref[...].astype(F32) * invf_ref[...]
    group = lax.broadcasted_iota(jnp.int32, ang.shape, 1) // half
    sign = jnp.where(group % 2 == 0, -1.0, 1.0)
    for table, out_ref, scale in ((jnp.cos(ang), cos_ref, None), (jnp.sin(ang), sin_ref, sign)):
        rolled = [table] + [pltpu.roll(table, half * j, 1) for j in range(1, per_row)]
        for m in range(per_row):
            out = rolled[(per_row - 1 - m) % per_row]
            for g in range(per_row - 1):
                out = jnp.where(group == g, rolled[(g - m) % per_row], out)
            out_ref[pl.ds(m, rows, stride=per_row), :] = out if scale is None else out * scale


def _rope_tables(positions):
    T = positions.size
    half = HEAD_DIM // 2
    per_row = LANES // half
    inv_freq = ROPE_THETA ** (-jnp.arange(half, dtype=F32) / half)
    invf = jnp.tile(inv_freq, per_row).reshape(1, LANES)
    pos = jnp.repeat(positions.reshape(T // per_row, per_row), half, axis=1)
    rows = T // per_row
    tr = min(rows, 1024)
    return pl.pallas_call(
        _rope_table_kernel,
        out_shape=(jax.ShapeDtypeStruct((T, LANES), F32),) * 2,
        grid=(rows // tr,),
        in_specs=[pl.BlockSpec((tr, LANES), lambda i: (i, 0)),
                  pl.BlockSpec((1, LANES), lambda i: (0, 0))],
        out_specs=(pl.BlockSpec((tr * per_row, LANES), lambda i: (i, 0)),) * 2,
        compiler_params=_params("arbitrary"),
    )(pos, invf)


def _rope(t, cos, sin):
    n = t.shape[1]
    reps = n // LANES
    c = jnp.tile(cos, (1, reps)) if reps > 1 else cos
    s = jnp.tile(sin, (1, reps)) if reps > 1 else sin
    lane = lax.broadcasted_iota(jnp.int32, t.shape, 1)
    first_half = (lane % HEAD_DIM) < (HEAD_DIM // 2)
    partner = jnp.where(first_half, pltpu.roll(t, n - HEAD_DIM // 2, 1), pltpu.roll(t, HEAD_DIM // 2, 1))
    return t * c + partner * s


def _inproj_kernel(x_ref, g_ref, w_ref, mu_ref, cos_ref, sin_ref,
                   q_ref, k_ref, v_ref, zr_ref, gate_ref, carry_ref, *, tiles_per_seq):
    i = pl.program_id(0)
    tm = x_ref.shape[0]
    h = _rms(x_ref[...], g_ref[...]).astype(BF16)
    cos = cos_ref[...]
    sin = sin_ref[...]

    za = jnp.dot(h, w_ref[:, :ATT_COLS], preferred_element_type=F32)
    q_ref[...] = (_rope(za[:, :ATT_WIDTH], cos, sin) * (HEAD_DIM ** -0.5)).astype(BF16)

    def per_head_doubled(t):
        lane = lax.broadcasted_iota(jnp.int32, t.shape, 1)
        swapped = pltpu.roll(t, HEAD_DIM, 1)
        low = lane < HEAD_DIM
        return jnp.concatenate([jnp.where(low, t, swapped), jnp.where(low, swapped, t)], axis=1).astype(BF16)

    k_ref[...] = per_head_doubled(_rope(za[:, ATT_WIDTH:ATT_WIDTH + KV_WIDTH], cos, sin))
    v_ref[...] = per_head_doubled(za[:, ATT_WIDTH + KV_WIDTH:])

    zs = jnp.dot(h, w_ref[:, ATT_COLS:ATT_COLS + SHIFT_COLS], preferred_element_type=F32)
    row = lax.broadcasted_iota(jnp.int32, zs.shape, 0)
    seq_start = (i % tiles_per_seq) == 0
    before = jnp.where(seq_start, 0.0, carry_ref[0:1, :])
    prev = jnp.where(row == 0, before, pltpu.roll(zs, 1, 0))
    carry_ref[0:1, :] = zs[tm - 1:tm, :]
    zr_ref[...] = zs + (prev - zs) * mu_ref[...]

    zg = jnp.dot(h, w_ref[:, ATT_COLS + SHIFT_COLS:], preferred_element_type=F32)
    gate_ref[...] = _sigmoid(zg).astype(BF16)


def _inproj(x2, ln, w_in, mu, cos, sin, seq):
    T, D = x2.shape
    tm = ROW_TILE
    row = lambda n: pl.BlockSpec((tm, n), lambda i: (i, 0))
    full = lambda a: pl.BlockSpec(a.shape, lambda i: (0,) * a.ndim)
    resident = lambda a: pl.BlockSpec(a.shape, lambda i: (0,) * a.ndim, pipeline_mode=pl.Buffered(1))
    ln = ln.reshape(1, D)
    mu = mu.reshape(1, SHIFT_COLS)
    return pl.pallas_call(
        functools.partial(_inproj_kernel, tiles_per_seq=seq // tm),
        out_shape=(jax.ShapeDtypeStruct((T, ATT_WIDTH), BF16),
                   jax.ShapeDtypeStruct((T, 2 * KV_WIDTH), BF16),
                   jax.ShapeDtypeStruct((T, 2 * KV_WIDTH), BF16),
                   jax.ShapeDtypeStruct((T, SHIFT_COLS), F32),
                   jax.ShapeDtypeStruct((T, GATE_COLS), BF16)),
        grid=(T // tm,),
        in_specs=[row(D), full(ln), resident(w_in), full(mu), row(LANES), row(LANES)],
        out_specs=(row(ATT_WIDTH), row(2 * KV_WIDTH), row(2 * KV_WIDTH), row(SHIFT_COLS), row(GATE_COLS)),
        scratch_shapes=[pltpu.VMEM((8, SHIFT_COLS), F32)],
        compiler_params=_params("arbitrary"),
    )(x2, ln, w_in, mu, cos, sin)


def _attn_kernel(sink_ref, q_ref, kp_ref, kc_ref, vp_ref, vc_ref, o_ref):
    n = pl.program_id(1)
    bq = WINDOW
    grp = ATT_HEADS // ATT_KV_HEADS
    gw = grp * HEAD_DIM
    kall = jnp.concatenate([kp_ref[...], kc_ref[...]], axis=0)
    vall = jnp.concatenate([vp_ref[...], vc_ref[...]], axis=0)
    qi = lax.broadcasted_iota(jnp.int32, (grp * bq, 2 * bq), 0) % bq
    si = lax.broadcasted_iota(jnp.int32, (grp * bq, 2 * bq), 1)
    diff = qi + bq - si
    band = (diff >= 0) & (diff < WINDOW)
    head_row = lax.broadcasted_iota(jnp.int32, (grp * bq, 1), 0) // bq
    lane_head = lax.broadcasted_iota(jnp.int32, (bq, gw), 1) // HEAD_DIM
    zero = jnp.zeros((), BF16)
    for blk in range(q_ref.shape[0] // bq):
        valid = band & ((si >= bq) | (n > 0)) if blk == 0 else band
        outs = []
        for j in range(ATT_KV_HEADS):
            qg = q_ref[blk * bq:(blk + 1) * bq, j * gw:(j + 1) * gw]
            lhs = jnp.concatenate([jnp.where(lane_head == g, qg, zero) for g in range(grp)], axis=0)
            k_half = kall[blk * bq:(blk + 2) * bq, j * 2 * HEAD_DIM:(j + 1) * 2 * HEAD_DIM]
            v_half = vall[blk * bq:(blk + 2) * bq, j * 2 * HEAD_DIM:(j + 1) * 2 * HEAD_DIM]
            k_rep = jnp.concatenate([k_half, k_half], axis=1)
            v_rep = jnp.concatenate([v_half, v_half], axis=1)
            s = lax.dot_general(lhs, k_rep, (((1,), (1,)), ((), ())), preferred_element_type=F32)
            s = jnp.where(valid, s, NEG_INF)
            sink = jnp.zeros((grp * bq, 1), F32)
            for g in range(grp):
                sink = jnp.where(head_row == g, sink_ref[j * grp + g], sink)
            m = jnp.maximum(jnp.max(s, axis=-1, keepdims=True), sink)
            e = jnp.exp(s - m)
            denom = jnp.sum(e, axis=-1, keepdims=True) + jnp.exp(sink - m)
            pv = jnp.dot(e.astype(BF16), v_rep, preferred_element_type=F32) / denom
            out = pv[:bq]
            for g in range(1, grp):
                out = jnp.where(lane_head == g, pv[g * bq:(g + 1) * bq], out)
            outs.append(out)
        o_ref[blk * bq:(blk + 1) * bq, :] = jnp.concatenate(outs, axis=1).astype(o_ref.dtype)


def _attention(q, k, v, sinks, batch, seq):
    T = q.shape[0]
    bq = WINDOW
    per_step = ATT_BLOCKS_PER_STEP
    ns = seq // (bq * per_step)
    cur = lambda b, n: (b * ns + n, 0)
    prev = lambda b, n: (jnp.maximum((b * ns + n) * per_step - 1, 0), 0)
    return pl.pallas_call(
        _attn_kernel,
        out_shape=jax.ShapeDtypeStruct((T, ATT_WIDTH), BF16),
        grid=(batch, ns),
        in_specs=[pl.BlockSpec(memory_space=pltpu.SMEM),
                  pl.BlockSpec((bq * per_step, ATT_WIDTH), cur),
                  pl.BlockSpec((bq, 2 * KV_WIDTH), prev), pl.BlockSpec((bq * per_step, 2 * KV_WIDTH), cur),
                  pl.BlockSpec((bq, 2 * KV_WIDTH), prev), pl.BlockSpec((bq * per_step, 2 * KV_WIDTH), cur)],
        out_specs=pl.BlockSpec((bq * per_step, ATT_WIDTH), cur),
        compiler_params=_params("arbitrary", "arbitrary"),
    )(sinks, q, k, k, v, v)


def _rwkv_kernel(z_ref, w0_ref, wd_ref, a0_ref, wa_ref, wg_ref, kk_ref, ka_ref, rk_ref, lnw_ref, lnb_ref,
                 y_ref, s_ref):
    c = pl.program_id(1)

    @pl.when(c == 0)
    def _():
        s_ref[...] = jnp.zeros_like(s_ref)

    states = _rwkv_rows(z_ref, w0_ref, wd_ref, a0_ref, wa_ref, wg_ref, kk_ref, ka_ref, rk_ref, lnw_ref, lnb_ref,
                        y_ref, [s_ref[pj] for pj in range(s_ref.shape[0])])
    for pj, state in enumerate(states):
        s_ref[pj] = state


def _rwkv_rows(z_ref, w0_ref, wd_ref, a0_ref, wa_ref, wg_ref, kk_ref, ka_ref, rk_ref, lnw_ref, lnb_ref, y_ref, states):
    L = RWKV_CHUNK
    rows = z_ref.shape[0]
    n_chunks = rows // L
    C = RWKV_WIDTH
    N = RWKV_HEAD
    r = z_ref[:, 0:C]
    k = z_ref[:, C:2 * C]
    v = z_ref[:, 2 * C:3 * C]
    xw = z_ref[:, 3 * C:3 * C + DECAY_LORA]
    xa = z_ref[:, 3 * C + DECAY_LORA:3 * C + DECAY_LORA + AAA_LORA]
    xg = z_ref[:, 3 * C + DECAY_LORA + AAA_LORA:]

    wlin = w0_ref[...] + _bdot(jnp.tanh(xw), wd_ref[...])
    logdecay = -math.exp(-0.5) * _sigmoid(wlin)
    a = _sigmoid(a0_ref[...] + _bdot(xa, wa_ref[...]))
    g = _bdot(_sigmoid(xg), wg_ref[...])
    kk = k * kk_ref[...]
    k2 = k * (1.0 + (a - 1.0) * ka_ref[...])

    slab = 2 * LANES
    hr = lax.broadcasted_iota(jnp.int32, (slab, slab), 0) // N
    hc = lax.broadcasted_iota(jnp.int32, (slab, slab), 1) // N
    head_ones = jnp.where(hr == hc, 1.0, 0.0).astype(BF16)

    def head_sum(t):
        t = t.astype(BF16)
        return jnp.concatenate([jnp.dot(t[:, j * slab:(j + 1) * slab], head_ones, preferred_element_type=F32)
                                for j in range(C // slab)], axis=1)

    kkn = kk / jnp.maximum(jnp.sqrt(head_sum(kk * kk)), 1e-12)

    assert L == N and (RWKV_PACK * N) % LANES == 0, "the packed block-diagonal products need chunk == head size"
    row = lax.broadcasted_iota(jnp.int32, (L, L), 0)
    col = lax.broadcasted_iota(jnp.int32, (L, L), 1)
    tri = jnp.where(row >= col, 1.0, 0.0).astype(BF16)
    ld_1 = logdecay.astype(BF16)
    ld_2 = (logdecay - ld_1.astype(F32)).astype(BF16)
    cums = []
    for ci in range(n_chunks):
        rs = slice(ci * L, (ci + 1) * L)
        cums.append(jnp.dot(tri, ld_1[rs], preferred_element_type=F32)
                    + jnp.dot(tri, ld_2[rs], preferred_element_type=F32))
    cum = jnp.concatenate(cums, axis=0) if n_chunks > 1 else cums[0]
    last = [cums[ci][L - 1:L, :] for ci in range(n_chunks)]
    cum_last = jnp.concatenate([jnp.broadcast_to(t, (L, C)) for t in last], axis=0) if n_chunks > 1 \
        else jnp.broadcast_to(last[0], (L, C))
    p_in = jnp.exp(cum)
    p_inv = jnp.exp(-cum)
    p_rem = jnp.exp(cum_last - cum)
    b = kkn * a
    at_f = (-kkn * jnp.exp(cum - logdecay)).astype(BF16)
    rt_f = r * p_in
    rt_b = rt_f.astype(BF16)
    bt_b = (b * p_inv).astype(BF16)
    kt_b = (k2 * p_inv).astype(BF16)
    br_b = (b * p_rem).astype(BF16)
    kr_b = (k2 * p_rem).astype(BF16)
    v_b = v.astype(BF16)

    dot = lambda x, y: jnp.dot(x.astype(BF16), y.astype(BF16), preferred_element_type=F32)
    dot_nt = lambda x, y: lax.dot_general(x.astype(BF16), y.astype(BF16), (((1,), (1,)), ((), ())),
                                          preferred_element_type=F32)
    dot_tn = lambda x, y: lax.dot_general(x.astype(BF16), y.astype(BF16), (((0,), (0,)), ((), ())),
                                          preferred_element_type=F32)

    PW = RWKV_PACK * N
    n_groups = C // PW
    lane_head = lax.broadcasted_iota(jnp.int32, (L, PW), 1) // N
    pcol = lax.broadcasted_iota(jnp.int32, (L, PW), 1) - lane_head * N
    prow = lax.broadcasted_iota(jnp.int32, (L, PW), 0)
    p_lower = prow > pcol
    p_lower_eq = prow >= pcol
    p_eye = jnp.where(prow == pcol, 1.0, 0.0)
    zero = jnp.zeros((), BF16)

    def bdiag(t):
        t = t.astype(BF16)
        return jnp.concatenate([jnp.where(lane_head == h, t, zero) for h in range(RWKV_PACK)], axis=0)

    def pack_rows(t):
        out = t[:N]
        for h in range(1, RWKV_PACK):
            out = jnp.where(lane_head == h, t[h * N:(h + 1) * N], out)
        return out

    cut = lambda t, ci, pj: t[ci * L:(ci + 1) * L, pj * PW:(pj + 1) * PW]
    same_head = (lax.broadcasted_iota(jnp.int32, (PW, PW), 0) // N) == \
                (lax.broadcasted_iota(jnp.int32, (PW, PW), 1) // N)
    rq, y0, gg, hh = {}, {}, {}, {}
    for g0 in range(0, n_chunks, RWKV_STAGE_GROUP):
        pieces = [(ci, pj) for ci in range(g0, min(g0 + RWKV_STAGE_GROUP, n_chunks)) for pj in range(n_groups)]
        _rwkv_chunk_algebra(pieces, cut, bdiag, dot, dot_nt, dot_tn, L, PW, at_f, rt_f, rt_b, bt_b, kt_b, br_b, kr_b,
                            v_b, p_lower, p_lower_eq, p_eye, same_head, pack_rows, rq, y0, gg, hh)

    ys = {}
    new_states = []
    for pj in range(n_groups):
        state = states[pj]
        for ci in range(n_chunks):
            p = (ci, pj)
            sb = state.astype(BF16)
            ys[p] = y0[p] + dot_nt(rq[p], bdiag(sb))
            p_tot = p_in[(ci + 1) * L - 1:(ci + 1) * L, pj * PW:(pj + 1) * PW]
            state = state * p_tot + dot(sb, gg[p]) + hh[p]
        new_states.append(state)
    y = jnp.concatenate([jnp.concatenate([ys[(ci, pj)] for pj in range(n_groups)], axis=1)
                         for ci in range(n_chunks)], axis=0)

    mu = head_sum(y) * (1.0 / N)
    dev = y - mu
    var = head_sum(dev * dev) * (1.0 / N)
    yn = dev * lax.rsqrt(var + RWKV_GN_EPS)
    bonus = head_sum(r * k2 * rk_ref[...]) * v
    y_ref[...] = ((yn * lnw_ref[...] + lnb_ref[...] + bonus) * g).astype(y_ref.dtype)
    return new_states


def _rwkv_chunk_algebra(pieces, cut, bdiag, dot, dot_nt, dot_tn, L, PW, at_f, rt_f, rt_b, bt_b, kt_b, br_b, kr_b, v_b,
                        p_lower, p_lower_eq, p_eye, same_head, pack_rows, rq_out, y0_out, gg_out, hh_out):
    at = {p: cut(at_f, *p) for p in pieces}
    vv = {p: cut(v_b, *p) for p in pieces}
    m = {p: dot_nt(jnp.concatenate([at[p], cut(rt_b, *p)], axis=0),
                   jnp.concatenate([bdiag(cut(bt_b, *p)), bdiag(cut(kt_b, *p))], axis=0)) for p in pieces}
    m_ab = {p: jnp.where(p_lower, m[p][:L, :PW], 0.0) for p in pieces}
    m_ak = {p: jnp.where(p_lower, m[p][:L, PW:], 0.0).astype(BF16) for p in pieces}
    m_rb = {p: jnp.where(p_lower_eq, m[p][L:, :PW], 0.0).astype(BF16) for p in pieces}
    m_rk = {p: jnp.where(p_lower_eq, m[p][L:, PW:], 0.0).astype(BF16) for p in pieces}
    inv = {p: p_eye + m_ab[p] for p in pieces}
    pw = {p: m_ab[p].astype(BF16) for p in pieces}
    pw = {p: dot(pw[p], bdiag(pw[p])).astype(BF16) for p in pieces}
    for _ in range(max(2, (L - 1).bit_length()) - 2):
        both = {p: dot(jnp.concatenate([pw[p], inv[p].astype(BF16)], axis=0), bdiag(pw[p])) for p in pieces}
        pw = {p: both[p][:L].astype(BF16) for p in pieces}
        inv = {p: inv[p] + both[p][L:] for p in pieces}
    inv = {p: (inv[p] + dot(inv[p], bdiag(pw[p]))).astype(BF16) for p in pieces}
    mv = {p: dot(jnp.concatenate([m_ak[p], m_rk[p]], axis=0), bdiag(vv[p])) for p in pieces}
    wu = {p: dot(inv[p], jnp.concatenate([bdiag(at[p]), bdiag(mv[p][:L])], axis=1)) for p in pieces}
    w = {p: wu[p][:, :PW].astype(BF16) for p in pieces}
    u0 = {p: wu[p][:, PW:].astype(BF16) for p in pieces}
    ry = {p: dot(m_rb[p], jnp.concatenate([bdiag(w[p]), bdiag(u0[p])], axis=1)) for p in pieces}
    rq_out.update({p: (cut(rt_f, *p) + ry[p][:, :PW]).astype(BF16) for p in pieces})
    y0_out.update({p: ry[p][:, PW:] + mv[p][L:] for p in pieces})
    gg_out.update({p: jnp.where(same_head, dot_tn(w[p], cut(br_b, *p)), 0.0).astype(BF16) for p in pieces})
    hh_full = {p: dot_tn(jnp.concatenate([u0[p], vv[p]], axis=0),
                         jnp.concatenate([cut(br_b, *p), cut(kr_b, *p)], axis=0)) for p in pieces}
    hh_out.update({p: pack_rows(hh_full[p]) for p in pieces})


def _rwkv(zr, w0, wd, a0, wa, wg, k_k, k_a, r_k, ln_w, ln_b, batch, seq):
    T = zr.shape[0]
    L = RWKV_CHUNK * RWKV_CHUNKS_PER_STEP
    nc = seq // L
    vec = lambda a: a.reshape(1, RWKV_WIDTH)
    full = lambda a: pl.BlockSpec(a.shape, lambda b, c: (0,) * a.ndim)
    args = (vec(w0), wd, vec(a0), wa, wg, vec(k_k), vec(k_a), vec(r_k), vec(ln_w), vec(ln_b))
    return pl.pallas_call(
        _rwkv_kernel,
        out_shape=jax.ShapeDtypeStruct((T, RWKV_WIDTH), BF16),
        grid=(batch, nc),
        in_specs=[pl.BlockSpec((L, SHIFT_COLS), lambda b, c: (b * nc + c, 0))] + [full(a) for a in args],
        out_specs=pl.BlockSpec((L, RWKV_WIDTH), lambda b, c: (b * nc + c, 0)),
        scratch_shapes=[pltpu.VMEM((RWKV_HEADS // RWKV_PACK, RWKV_HEAD, RWKV_PACK * RWKV_HEAD), F32)],
        compiler_params=_params("arbitrary", "arbitrary"),
    )(zr, *args)


def _merge_kernel(x_ref, ya_ref, yr_ref, gate_ref, wba_ref, wbr_ref, wo_ref, lnm_ref, wr_ref, br_ref,
                  x1_ref, h2_ref, route_ref, cnt_ref, cnt_scr):
    i = pl.program_id(0)

    @pl.when(i == 0)
    def _():
        cnt_scr[...] = jnp.zeros_like(cnt_scr)

    part = x_ref.shape[0] // MERGE_ROW_PARTS
    chains = [_merge_rows(x_ref, ya_ref, yr_ref, gate_ref, wba_ref, wbr_ref, wo_ref, lnm_ref, wr_ref, br_ref,
                          x1_ref, h2_ref, route_ref, cnt_ref, cnt_scr, j * part, part)
              for j in range(MERGE_ROW_PARTS)]
    for _ in zip(*chains):
        pass


def _merge_rows(x_ref, ya_ref, yr_ref, gate_ref, wba_ref, wbr_ref, wo_ref, lnm_ref, wr_ref, br_ref,
                x1_ref, h2_ref, route_ref, cnt_ref, cnt_scr, r0, tm):
    D = D_MODEL
    rows = slice(r0, r0 + tm)
    gates = gate_ref[rows, :].astype(F32)
    ya = jnp.dot(ya_ref[rows, :], wba_ref[...], preferred_element_type=F32)
    yr = jnp.dot(yr_ref[rows, :], wbr_ref[...], preferred_element_type=F32)
    yield
    merged = gates[:, :D] * ya + gates[:, D:] * yr
    x1 = x_ref[rows, :] + jnp.dot(merged.astype(BF16), wo_ref[...], preferred_element_type=F32)
    x1_ref[rows, :] = x1
    yield
    h2 = _rms(x1, lnm_ref[...])
    _store_slabs(h2_ref, h2, r0)

    h_hi = h2.astype(BF16)
    h_lo = (h2 - h_hi.astype(F32)).astype(BF16)
    parts = (jnp.dot(h_hi, wr_ref[...], preferred_element_type=F32)
             + jnp.dot(h_lo, wr_ref[...], preferred_element_type=F32))
    yield
    logits = parts[:, :LANES] + parts[:, LANES:] + br_ref[...]
    lane = lax.broadcasted_iota(jnp.int32, logits.shape, 1)
    big = jnp.int32(1 << 20)
    gl = jnp.where(lane < N_GROUPS, logits, NEG_INF)
    gmax = jnp.max(gl, axis=-1, keepdims=True)
    yield
    gidx = jnp.min(jnp.where(gl == gmax, lane, big), axis=-1, keepdims=True)
    gsum = jnp.sum(jnp.where(lane < N_GROUPS, jnp.exp(logits - gmax), 0.0), axis=-1, keepdims=True)
    yield
    g_w = 1.0 / gsum
    lo = N_GROUPS + EXPERTS_PER_GROUP * gidx
    el = jnp.where((lane >= lo) & (lane < lo + EXPERTS_PER_GROUP), logits, NEG_INF)
    m1 = jnp.max(el, axis=-1, keepdims=True)
    yield
    i1 = jnp.min(jnp.where(el == m1, lane, big), axis=-1, keepdims=True)
    yield
    el2 = jnp.where(lane == i1, NEG_INF, el)
    m2 = jnp.max(el2, axis=-1, keepdims=True)
    yield
    i2 = jnp.min(jnp.where(el2 == m2, lane, big), axis=-1, keepdims=True)
    yield
    d = jnp.exp(m2 - m1)
    w1 = g_w / (1.0 + d)
    w2 = g_w * d / (1.0 + d)
    e1 = i1 - N_GROUPS
    e2 = i2 - N_GROUPS

    hit1 = lane == e1
    hit2 = lane == e2
    onehot = jnp.where(hit1 | hit2, 1.0, 0.0).astype(BF16)
    r_i = lax.broadcasted_iota(jnp.int32, (tm, tm), 0)
    c_i = lax.broadcasted_iota(jnp.int32, (tm, tm), 1)
    before = jnp.dot(jnp.where(r_i > c_i, 1.0, 0.0).astype(BF16), onehot, preferred_element_type=F32)
    yield
    before = before + cnt_scr[...]
    rank1 = jnp.sum(jnp.where(hit1, before, 0.0), axis=-1, keepdims=True)
    rank2 = jnp.sum(jnp.where(hit2, before, 0.0), axis=-1, keepdims=True)
    cnt_scr[...] = cnt_scr[...] + jnp.sum(onehot.astype(F32), axis=0, keepdims=True)
    cnt_ref[...] = cnt_scr[...]

    route = jnp.where(lane == 0, e1.astype(F32), 0.0)
    route = jnp.where(lane == 1, e2.astype(F32), route)
    route = jnp.where(lane == 2, rank1, route)
    route = jnp.where(lane == 3, rank2, route)
    route = jnp.where(lane == 4, w1, route)
    route = jnp.where(lane == 5, w2, route)
    route_ref[rows, :] = route[:, :ROUTE_COLS]
    yield


def _merge(x2, y_att, y_rwkv, gates, wba, wbr, wo, ln_moe, w_router, b_router):
    T, D = x2.shape
    tm = MERGE_TILE
    row = lambda n: pl.BlockSpec((tm, n), lambda i: (i, 0))
    full = lambda a: pl.BlockSpec(a.shape, lambda i: (0,) * a.ndim)
    ln_moe = ln_moe.reshape(1, D)
    return pl.pallas_call(
        _merge_kernel,
        out_shape=(jax.ShapeDtypeStruct((T, D), F32), jax.ShapeDtypeStruct((T * SLAB_ROWS, LANES), F32),
                   jax.ShapeDtypeStruct((T, ROUTE_COLS), F32), jax.ShapeDtypeStruct((1, LANES), F32)),
        grid=(T // tm,),
        in_specs=[row(D), row(ATT_WIDTH), row(RWKV_WIDTH), row(GATE_COLS),
                  full(wba), full(wbr), full(wo), full(ln_moe), full(w_router), full(b_router)],
        out_specs=(row(D), pl.BlockSpec((tm * SLAB_ROWS, LANES), lambda i: (i, 0)), row(ROUTE_COLS),
                   pl.BlockSpec((1, LANES), lambda i: (0, 0))),
        scratch_shapes=[pltpu.VMEM((1, LANES), F32)],
        compiler_params=_params("arbitrary"),
    )(x2, y_att, y_rwkv, gates, wba, wbr, wo, ln_moe, w_router, b_router)


def _dispatch_kernel(dest_ref, h_ref, xrows_hbm, row_sem, *, tile, tokens):
    base = pl.program_id(0) * tile

    def row_copy(j, slot):
        return pltpu.make_async_copy(_slab(h_ref, j), _slab(xrows_hbm, dest_ref[slot * tokens + base + j]), row_sem)

    def issue(j, carry):
        row_copy(j, 0).start(priority=0)
        row_copy(j, 1).start(priority=1)
        return carry

    lax.fori_loop(0, tile, issue, 0, unroll=8)

    def drain(j, carry):
        row_copy(j, 0).wait()
        row_copy(j, 1).wait()
        return carry

    lax.fori_loop(0, tile, drain, 0, unroll=8)


def _dispatch(h2_slabs, dest):
    T = h2_slabs.shape[0] // SLAB_ROWS
    tile = min(DISPATCH_TILE, T)
    assert T % tile == 0
    return pl.pallas_call(
        functools.partial(_dispatch_kernel, tile=tile, tokens=T),
        out_shape=jax.ShapeDtypeStruct((2 * T * SLAB_ROWS, LANES), F32),
        grid_spec=pltpu.PrefetchScalarGridSpec(
            num_scalar_prefetch=1,
            grid=(T // tile,),
            in_specs=[pl.BlockSpec((tile * SLAB_ROWS, LANES), lambda i, d: (i, 0))],
            out_specs=pl.BlockSpec(memory_space=pl.ANY),
            scratch_shapes=[pltpu.SemaphoreType.DMA],
        ),
        compiler_params=_params("arbitrary"),
    )(dest, h2_slabs)


def _expert_kernel(item_e, item_b, item_lo, item_hi, x_ref, wg_ref, wu_ref, wd_ref, y_ref, wgu_bf, wd_bf):
    w = pl.program_id(0)
    lo = item_lo[w]
    hi = item_hi[w]

    @pl.when((w == 0) | (item_e[w] != item_e[jnp.maximum(w - 1, 0)]))
    def _():
        wgu_bf[:, :D_EXPERT] = wg_ref[0].astype(BF16)
        wgu_bf[:, D_EXPERT:] = wu_ref[0].astype(BF16)
        wd_bf[...] = wd_ref[0].astype(BF16)

    def run(keep_other_rows):
        bm = x_ref.shape[0] // SLAB_ROWS
        xb = _load_slabs(x_ref, bm).astype(BF16)
        hgu = jnp.dot(xb, wgu_bf[...], preferred_element_type=F32)
        hg = hgu[:, :D_EXPERT]
        hid = hg * _sigmoid(hg) * hgu[:, D_EXPERT:]
        y = jnp.dot(hid.astype(BF16), wd_bf[...], preferred_element_type=F32)
        if keep_other_rows:
            rows = lax.broadcasted_iota(jnp.int32, y.shape, 0)
            y = jnp.where((rows >= lo) & (rows < hi), y, _load_slabs(y_ref, bm))
        _store_slabs(y_ref, y)

    pl.when((hi > lo) & (lo == 0))(functools.partial(run, False))
    pl.when((hi > lo) & (lo > 0))(functools.partial(run, True))


def _experts(xrows, items, wg, wu, wd):
    bm = EXPERT_BLOCK
    D = D_MODEL
    n_items = items[0].shape[0]
    block = pl.BlockSpec((bm * SLAB_ROWS, LANES), lambda w, ie, ib, lo, hi: (ib[w], 0))
    return pl.pallas_call(
        _expert_kernel,
        out_shape=jax.ShapeDtypeStruct(xrows.shape, F32),
        grid_spec=pltpu.PrefetchScalarGridSpec(
            num_scalar_prefetch=4,
            grid=(n_items,),
            in_specs=[block,
                      pl.BlockSpec((1, D, D_EXPERT), lambda w, ie, ib, lo, hi: (ie[w], 0, 0)),
                      pl.BlockSpec((1, D, D_EXPERT), lambda w, ie, ib, lo, hi: (ie[w], 0, 0)),
                      pl.BlockSpec((1, D_EXPERT, D), lambda w, ie, ib, lo, hi: (ie[w], 0, 0))],
            out_specs=block,
            scratch_shapes=[pltpu.VMEM((D, 2 * D_EXPERT), BF16), pltpu.VMEM((D_EXPERT, D), BF16)],
        ),
        compiler_params=_params("arbitrary"),
    )(*items, xrows, wg, wu, wd)


def _work_items(counts, n_rows):
    bm = EXPERT_BLOCK
    nb = n_rows // bm
    n_items = nb + N_EXPERTS
    ends = jnp.cumsum(counts)
    starts = ends - counts
    first_blk = starts // bm
    last_blk = (ends - 1) // bm
    per_e = jnp.where(counts > 0, last_blk - first_blk + 1, 0)
    item_end = jnp.cumsum(per_e)
    total = item_end[-1]
    w = jnp.arange(n_items, dtype=jnp.int32)
    wc = jnp.minimum(w, total - 1)
    e = jnp.minimum(jnp.sum((item_end[None, :] <= wc[:, None]).astype(jnp.int32), axis=1), N_EXPERTS - 1)
    b = first_blk[e] + (wc - (item_end[e] - per_e[e]))
    lo = jnp.maximum(starts[e], b * bm) - b * bm
    hi = jnp.minimum(ends[e], (b + 1) * bm) - b * bm
    live = w < total
    lo = jnp.where(live, lo, 0)
    hi = jnp.where(live, hi, 0)
    return (e.astype(jnp.int32), b.astype(jnp.int32), lo.astype(jnp.int32), hi.astype(jnp.int32)), starts


def _final_kernel(dest_ref, yrows_hbm, x1_ref, route_ref, p_ref, lnp_ref, wpg_ref, wpp_ref, lnf_ref,
                  o_ref, rows_ref, row_sems):
    i = pl.program_id(0)
    n = pl.num_programs(0)
    tm = x1_ref.shape[0]
    tokens = n * tm

    def row_copy(tile, j, slot, par):
        return pltpu.make_async_copy(_slab(yrows_hbm, dest_ref[slot * tokens + tile * tm + j]),
                                     _slab(rows_ref.at[par, slot], j), row_sems.at[par])

    def wait_tile(par):
        pltpu.make_async_copy(rows_ref.at[par], rows_ref.at[par], row_sems.at[par]).wait()

    @pl.when(i == 0)
    def _():
        def body(j, carry):
            row_copy(0, j, 0, 0).start(priority=0)
            row_copy(0, j, 1, 0).start(priority=1)
            return carry
        lax.fori_loop(0, tm, body, 0, unroll=8)

    def step(par):
        wait_tile(par)
        nxt = jnp.minimum(i + 1, n - 1)
        for j in range(tm):
            row_copy(nxt, j, 0, 1 - par).start(priority=0)
            row_copy(nxt, j, 1, 1 - par).start(priority=1)
        pp = jnp.dot(p_ref[...].astype(BF16), wpp_ref[...], preferred_element_type=F32)
        route = route_ref[...]
        x2 = (x1_ref[...] + route[:, 4:5] * _load_slabs(rows_ref.at[par, 0], tm)
              + route[:, 5:6] * _load_slabs(rows_ref.at[par, 1], tm))
        gate = _sigmoid(jnp.dot(_rms(x2, lnp_ref[...]).astype(BF16), wpg_ref[...], preferred_element_type=F32))
        x3 = x2 + gate * pp
        o_ref[...] = _rms(x3, lnf_ref[...])

        @pl.when(i == n - 1)
        def _():
            wait_tile(1 - par)

    for par in range(2):
        pl.when(i % 2 == par)(functools.partial(step, par))


def _final(dest, yrows, x1, route, p2, ln_ple, wpg, wpp, ln_final):
    T, D = x1.shape
    tm = GATHER_TILE
    row = lambda n: pl.BlockSpec((tm, n), lambda i, d: (i, 0))
    full = lambda a: pl.BlockSpec(a.shape, lambda i, d: (0,) * a.ndim)
    ln_ple = ln_ple.reshape(1, D)
    ln_final = ln_final.reshape(1, D)
    return pl.pallas_call(
        _final_kernel,
        out_shape=jax.ShapeDtypeStruct((T, D), F32),
        grid_spec=pltpu.PrefetchScalarGridSpec(
            num_scalar_prefetch=1,
            grid=(T // tm,),
            in_specs=[pl.BlockSpec(memory_space=pl.ANY),
                      row(D), row(ROUTE_COLS), row(PLE_DIM), full(ln_ple), full(wpg), full(wpp), full(ln_final)],
            out_specs=row(D),
            scratch_shapes=[pltpu.VMEM((2, 2, tm * SLAB_ROWS, LANES), F32), pltpu.SemaphoreType.DMA((2,))],
        ),
        compiler_params=_params("arbitrary"),
    )(dest, yrows, x1, route, p2, ln_ple, wpg, wpp, ln_final)


def kernel(x, p, positions, ln_mix, w_in, mu_shift, w0, w_decay_up, a0, w_aaa_up, w_gate_up, k_k, k_a, r_k, ln_x_w, ln_x_b, sinks, w_branch_att, w_branch_rwkv, w_out, ln_moe, w_group, b_group, w_expert, b_expert, w_gate_e, w_up_e, w_down_e, ln_ple, w_ple_gate, w_ple_proj, ln_final):
    B, S, D = x.shape
    T = B * S
    depth = w_in.shape[0]
    assert D == D_MODEL and S % ROW_TILE == 0 and S % (WINDOW * ATT_BLOCKS_PER_STEP) == 0 and S % (RWKV_CHUNK * RWKV_CHUNKS_PER_STEP) == 0
    assert T % GATHER_TILE == 0 and (2 * T) % EXPERT_BLOCK == 0
    cos, sin = _rope_tables(positions)
    x2 = x.reshape(T, D)
    out = None
    for i in range(depth):
        q, k, v, zr, gates = _inproj(x2, ln_mix[i], w_in[i].astype(BF16), mu_shift[i], cos, sin, S)
        y_att = _attention(q, k, v, sinks[i], B, S)
        y_rwkv = _rwkv(zr, w0[i], w_decay_up[i], a0[i], w_aaa_up[i], w_gate_up[i], k_k[i], k_a[i], r_k[i],
                       ln_x_w[i], ln_x_b[i], B, S)
        pad = LANES - N_GROUPS - N_EXPERTS
        w_router = jnp.concatenate([w_group[i], w_expert[i], jnp.zeros((D, pad), F32)], axis=1)
        w_router_hi = w_router.astype(BF16)
        w_router = jnp.concatenate([w_router_hi, (w_router - w_router_hi.astype(F32)).astype(BF16)], axis=1)
        b_router = jnp.concatenate([b_group[i], b_expert[i], jnp.zeros((pad,), F32)]).reshape(1, LANES)
        x1, h2, route, cnt = _merge(x2, y_att, y_rwkv, gates, w_branch_att[i].astype(BF16),
                                    w_branch_rwkv[i].astype(BF16), w_out[i].astype(BF16), ln_moe[i],
                                    w_router, b_router)
        counts = cnt[0, :N_EXPERTS].astype(jnp.int32)
        items, starts = _work_items(counts, 2 * T)
        route_t = route.T
        expert = route_t[0:2].astype(jnp.int32)
        first_row = jnp.sum(jnp.where(expert[None] == jnp.arange(N_EXPERTS, dtype=jnp.int32)[:, None, None],
                                      starts[:, None, None], 0), axis=0)
        dest = (first_row + route_t[2:4].astype(jnp.int32)).reshape(-1)
        xrows = _dispatch(h2, dest)
        yrows = _experts(xrows, items, w_gate_e[i], w_up_e[i], w_down_e[i])
        last = i == depth - 1
        assert last, "the final-norm kernel closes the only layer"
        out = _final(dest, yrows, x1, route, p[i].reshape(T, PLE_DIM), ln_ple[i], w_ple_gate[i].astype(BF16),
                     w_ple_proj[i].astype(BF16), ln_final)
    return out.reshape(B, S, D)
```

```python
import functools
import math

import jax
import jax.numpy as jnp
from jax import lax
from jax.experimental import pallas as pl
from jax.experimental.pallas import tpu as pltpu

F32 = jnp.float32
BF16 = jnp.bfloat16

D_MODEL = 1024
PLE_DIM = 256
ATT_HEADS = 8
ATT_KV_HEADS = 2
HEAD_DIM = 64
ATT_WIDTH = ATT_HEADS * HEAD_DIM
KV_WIDTH = ATT_KV_HEADS * HEAD_DIM
WINDOW = 128
ROPE_THETA = 10000.0
RWKV_HEADS = 8
RWKV_HEAD = 64
RWKV_WIDTH = RWKV_HEADS * RWKV_HEAD
DECAY_LORA = 64
AAA_LORA = 64
GATE_LORA = 128
RWKV_GN_EPS = 64e-5
ATT_COLS = ATT_WIDTH + 2 * KV_WIDTH
SHIFT_COLS = 3 * RWKV_WIDTH + DECAY_LORA + AAA_LORA + GATE_LORA
GATE_COLS = 2 * D_MODEL
N_GROUPS = 4
EXPERTS_PER_GROUP = 8
N_EXPERTS = N_GROUPS * EXPERTS_PER_GROUP
D_EXPERT = 512
NORM_EPS = 1e-6
NEG_INF = -1e30

LANES = 128
ROUTE_COLS = 8
SLAB_ROWS = D_MODEL // LANES
VMEM_LIMIT = 56 * 1024 * 1024

ROW_TILE = 512
MERGE_TILE = 1024
MERGE_ROW_PARTS = 2
ATT_BLOCKS_PER_STEP = 4
RWKV_CHUNK = 64
RWKV_CHUNKS_PER_STEP = 8
RWKV_PACK = 2
RWKV_STAGE_GROUP = 4
EXPERT_BLOCK = 1024
GATHER_TILE = 256
DISPATCH_TILE = 512


def _params(*sem):
    return pltpu.CompilerParams(dimension_semantics=sem, vmem_limit_bytes=VMEM_LIMIT)


def _bdot(a, b):
    return jnp.dot(a.astype(BF16), b.astype(BF16), preferred_element_type=F32)


def _rms(x, g):
    return x * lax.rsqrt(jnp.mean(x * x, axis=-1, keepdims=True) + NORM_EPS) * g


def _sigmoid(x):
    return 1.0 / (1.0 + jnp.exp(-x))


def _store_slabs(ref, val, first=0):
    m = val.shape[0]
    for j in range(SLAB_ROWS):
        ref[pl.ds(first * SLAB_ROWS + j, m, stride=SLAB_ROWS), :] = val[:, j * LANES:(j + 1) * LANES]


def _load_slabs(ref, m, first=0):
    return jnp.concatenate([ref[pl.ds(first * SLAB_ROWS + j, m, stride=SLAB_ROWS), :] for j in range(SLAB_ROWS)],
                           axis=1)


def _slab(ref, index):
    return ref.at[pl.ds(pl.multiple_of(index * SLAB_ROWS, SLAB_ROWS), SLAB_ROWS)]


def _rope_table_kernel(pos_ref, invf_ref, cos_ref, sin_ref):
    half = HEAD_DIM // 2
    per_row = LANES // half
    rows = pos_ref.shape[0]
    group = lax.broadcasted_iota(jnp.int32, (rows, LANES), 1) // half
    pos = pos_ref[...].astype(F32)
    pos_lanes = jnp.broadcast_to(pos[:, per_row - 1:per_row], (rows, LANES))
    for m in range(per_row - 1):
        pos_lanes = jnp.where(group == m, pos[:, m:m + 1], pos_lanes)
    ang = pos_lanes * invf_ref[...]
    sign = jnp.where(group % 2 == 0, -1.0, 1.0)
    for table, out_ref, scale in ((jnp.cos(ang), cos_ref, None), (jnp.sin(ang), sin_ref, sign)):
        rolled = [table] + [pltpu.roll(table, half * j, 1) for j in range(1, per_row)]
        for m in range(per_row):
            out = rolled[(per_row - 1 - m) % per_row]
            for g in range(per_row - 1):
                out = jnp.where(group == g, rolled[(g - m) % per_row], out)
            out_ref[pl.ds(m, rows, stride=per_row), :] = out if scale is None else out * scale


def _rope_tables(positions):
    T = positions.size
    half = HEAD_DIM // 2
    per_row = LANES // half
    inv_freq = ROPE_THETA ** (-jnp.arange(half, dtype=F32) / half)
    invf = jnp.tile(inv_freq, per_row).reshape(1, LANES)
    pos = positions.reshape(T // per_row, per_row)
    rows = T // per_row
    tr = min(rows, 1024)
    return pl.pallas_call(
        _rope_table_kernel,
        out_shape=(jax.ShapeDtypeStruct((T, LANES), F32),) * 2,
        grid=(rows // tr,),
        in_specs=[pl.BlockSpec((tr, per_row), lambda i: (i, 0)),
                  pl.BlockSpec((1, LANES), lambda i: (0, 0))],
        out_specs=(pl.BlockSpec((tr * per_row, LANES), lambda i: (i, 0)),) * 2,
        compiler_params=_params("arbitrary"),
    )(pos, invf)


def _rope(t, cos, sin):
    n = t.shape[1]
    reps = n // LANES
    c = jnp.tile(cos, (1, reps)) if reps > 1 else cos
    s = jnp.tile(sin, (1, reps)) if reps > 1 else sin
    lane = lax.broadcasted_iota(jnp.int32, t.shape, 1)
    first_half = (lane % HEAD_DIM) < (HEAD_DIM // 2)
    partner = jnp.where(first_half, pltpu.roll(t, n - HEAD_DIM // 2, 1), pltpu.roll(t, HEAD_DIM // 2, 1))
    return t * c + partner * s


def _inproj_kernel(x_ref, g_ref, w_ref, mu_ref, cos_ref, sin_ref,
                   q_ref, k_ref, v_ref, zr_ref, gate_ref, carry_ref, *, tiles_per_seq):
    i = pl.program_id(0)
    tm = x_ref.shape[0]
    h = _rms(x_ref[...], g_ref[...]).astype(BF16)
    cos = cos_ref[...]
    sin = sin_ref[...]

    za = jnp.dot(h, w_ref[:, :ATT_COLS], preferred_element_type=F32)
    q_ref[...] = (_rope(za[:, :ATT_WIDTH], cos, sin) * (HEAD_DIM ** -0.5)).astype(BF16)

    def per_head_doubled(t):
        lane = lax.broadcasted_iota(jnp.int32, t.shape, 1)
        swapped = pltpu.roll(t, HEAD_DIM, 1)
        low = lane < HEAD_DIM
        return jnp.concatenate([jnp.where(low, t, swapped), jnp.where(low, swapped, t)], axis=1).astype(BF16)

    k_ref[...] = per_head_doubled(_rope(za[:, ATT_WIDTH:ATT_WIDTH + KV_WIDTH], cos, sin))
    v_ref[...] = per_head_doubled(za[:, ATT_WIDTH + KV_WIDTH:])

    zs = jnp.dot(h, w_ref[:, ATT_COLS:ATT_COLS + SHIFT_COLS], preferred_element_type=F32)
    row = lax.broadcasted_iota(jnp.int32, zs.shape, 0)
    seq_start = (i % tiles_per_seq) == 0
    before = jnp.where(seq_start, 0.0, carry_ref[0:1, :])
    prev = jnp.where(row == 0, before, pltpu.roll(zs, 1, 0))
    carry_ref[0:1, :] = zs[tm - 1:tm, :]
    zr_ref[...] = zs + (prev - zs) * mu_ref[...]

    zg = jnp.dot(h, w_ref[:, ATT_COLS + SHIFT_COLS:], preferred_element_type=F32)
    gate_ref[...] = _sigmoid(zg).astype(BF16)


def _inproj(x2, ln, w_in, mu, cos, sin, seq):
    T, D = x2.shape
    tm = ROW_TILE
    row = lambda n: pl.BlockSpec((tm, n), lambda i: (i, 0))
    full = lambda a: pl.BlockSpec(a.shape, lambda i: (0,) * a.ndim)
    resident = lambda a: pl.BlockSpec(a.shape, lambda i: (0,) * a.ndim, pipeline_mode=pl.Buffered(1))
    ln = ln.reshape(1, D)
    mu = mu.reshape(1, SHIFT_COLS)
    return pl.pallas_call(
        functools.partial(_inproj_kernel, tiles_per_seq=seq // tm),
        out_shape=(jax.ShapeDtypeStruct((T, ATT_WIDTH), BF16),
                   jax.ShapeDtypeStruct((T, 2 * KV_WIDTH), BF16),
                   jax.ShapeDtypeStruct((T, 2 * KV_WIDTH), BF16),
                   jax.ShapeDtypeStruct((T, SHIFT_COLS), F32),
                   jax.ShapeDtypeStruct((T, GATE_COLS), BF16)),
        grid=(T // tm,),
        in_specs=[row(D), full(ln), resident(w_in), full(mu), row(LANES), row(LANES)],
        out_specs=(row(ATT_WIDTH), row(2 * KV_WIDTH), row(2 * KV_WIDTH), row(SHIFT_COLS), row(GATE_COLS)),
        scratch_shapes=[pltpu.VMEM((8, SHIFT_COLS), F32)],
        compiler_params=_params("arbitrary"),
    )(x2, ln, w_in, mu, cos, sin)


def _attn_kernel(sink_ref, q_ref, kp_ref, kc_ref, vp_ref, vc_ref, o_ref):
    n = pl.program_id(1)
    bq = WINDOW
    grp = ATT_HEADS // ATT_KV_HEADS
    gw = grp * HEAD_DIM
    kall = jnp.concatenate([kp_ref[...], kc_ref[...]], axis=0)
    vall = jnp.concatenate([vp_ref[...], vc_ref[...]], axis=0)
    qi = lax.broadcasted_iota(jnp.int32, (grp * bq, 2 * bq), 0) % bq
    si = lax.broadcasted_iota(jnp.int32, (grp * bq, 2 * bq), 1)
    diff = qi + bq - si
    band = (diff >= 0) & (diff < WINDOW)
    head_row = lax.broadcasted_iota(jnp.int32, (grp * bq, 1), 0) // bq
    lane_head = lax.broadcasted_iota(jnp.int32, (bq, gw), 1) // HEAD_DIM
    zero = jnp.zeros((), BF16)
    for blk in range(q_ref.shape[0] // bq):
        valid = band & ((si >= bq) | (n > 0)) if blk == 0 else band
        outs = []
        for j in range(ATT_KV_HEADS):
            qg = q_ref[blk * bq:(blk + 1) * bq, j * gw:(j + 1) * gw]
            lhs = jnp.concatenate([jnp.where(lane_head == g, qg, zero) for g in range(grp)], axis=0)
            k_half = kall[blk * bq:(blk + 2) * bq, j * 2 * HEAD_DIM:(j + 1) * 2 * HEAD_DIM]
            v_half = vall[blk * bq:(blk + 2) * bq, j * 2 * HEAD_DIM:(j + 1) * 2 * HEAD_DIM]
            k_rep = jnp.concatenate([k_half, k_half], axis=1)
            v_rep = jnp.concatenate([v_half, v_half], axis=1)
            s = lax.dot_general(lhs, k_rep, (((1,), (1,)), ((), ())), preferred_element_type=F32)
            s = jnp.where(valid, s, NEG_INF)
            sink = jnp.zeros((grp * bq, 1), F32)
            for g in range(grp):
                sink = jnp.where(head_row == g, sink_ref[j * grp + g], sink)
            m = jnp.maximum(jnp.max(s, axis=-1, keepdims=True), sink)
            e = jnp.exp(s - m)
            denom = jnp.sum(e, axis=-1, keepdims=True) + jnp.exp(sink - m)
            pv = jnp.dot(e.astype(BF16), v_rep, preferred_element_type=F32) / denom
            out = pv[:bq]
            for g in range(1, grp):
                out = jnp.where(lane_head == g, pv[g * bq:(g + 1) * bq], out)
            outs.append(out)
        o_ref[blk * bq:(blk + 1) * bq, :] = jnp.concatenate(outs, axis=1).astype(o_ref.dtype)


def _attention(q, k, v, sinks, batch, seq):
    T = q.shape[0]
    bq = WINDOW
    per_step = ATT_BLOCKS_PER_STEP
    ns = seq // (bq * per_step)
    cur = lambda b, n: (b * ns + n, 0)
    prev = lambda b, n: (jnp.maximum((b * ns + n) * per_step - 1, 0), 0)
    return pl.pallas_call(
        _attn_kernel,
        out_shape=jax.ShapeDtypeStruct((T, ATT_WIDTH), BF16),
        grid=(batch, ns),
        in_specs=[pl.BlockSpec(memory_space=pltpu.SMEM),
                  pl.BlockSpec((bq * per_step, ATT_WIDTH), cur),
                  pl.BlockSpec((bq, 2 * KV_WIDTH), prev), pl.BlockSpec((bq * per_step, 2 * KV_WIDTH), cur),
                  pl.BlockSpec((bq, 2 * KV_WIDTH), prev), pl.BlockSpec((bq * per_step, 2 * KV_WIDTH), cur)],
        out_specs=pl.BlockSpec((bq * per_step, ATT_WIDTH), cur),
        compiler_params=_params("arbitrary", "arbitrary"),
    )(sinks, q, k, k, v, v)


def _rwkv_kernel(z_ref, w0_ref, wd_ref, a0_ref, wa_ref, wg_ref, kk_ref, ka_ref, rk_ref, lnw_ref, lnb_ref,
                 y_ref, s_ref):
    c = pl.program_id(1)

    @pl.when(c == 0)
    def _():
        s_ref[...] = jnp.zeros_like(s_ref)

    states = _rwkv_rows(z_ref, w0_ref, wd_ref, a0_ref, wa_ref, wg_ref, kk_ref, ka_ref, rk_ref, lnw_ref, lnb_ref,
                        y_ref, [s_ref[pj] for pj in range(s_ref.shape[0])])
    for pj, state in enumerate(states):
        s_ref[pj] = state


def _rwkv_rows(z_ref, w0_ref, wd_ref, a0_ref, wa_ref, wg_ref, kk_ref, ka_ref, rk_ref, lnw_ref, lnb_ref, y_ref, states):
    L = RWKV_CHUNK
    rows = z_ref.shape[0]
    n_chunks = rows // L
    C = RWKV_WIDTH
    N = RWKV_HEAD
    r = z_ref[:, 0:C]
    k = z_ref[:, C:2 * C]
    v = z_ref[:, 2 * C:3 * C]
    xw = z_ref[:, 3 * C:3 * C + DECAY_LORA]
    xa = z_ref[:, 3 * C + DECAY_LORA:3 * C + DECAY_LORA + AAA_LORA]
    xg = z_ref[:, 3 * C + DECAY_LORA + AAA_LORA:]

    wlin = w0_ref[...] + _bdot(jnp.tanh(xw), wd_ref[...])
    logdecay = -math.exp(-0.5) * _sigmoid(wlin)
    a = _sigmoid(a0_ref[...] + _bdot(xa, wa_ref[...]))
    g = _bdot(_sigmoid(xg), wg_ref[...])
    kk = k * kk_ref[...]
    k2 = k * (1.0 + (a - 1.0) * ka_ref[...])

    slab = 2 * LANES
    hr = lax.broadcasted_iota(jnp.int32, (slab, slab), 0) // N
    hc = lax.broadcasted_iota(jnp.int32, (slab, slab), 1) // N
    head_ones = jnp.where(hr == hc, 1.0, 0.0).astype(BF16)

    def head_sum(t):
        t = t.astype(BF16)
        return jnp.concatenate([jnp.dot(t[:, j * slab:(j + 1) * slab], head_ones, preferred_element_type=F32)
                                for j in range(C // slab)], axis=1)

    kkn = kk / jnp.maximum(jnp.sqrt(head_sum(kk * kk)), 1e-12)

    assert L == N and (RWKV_PACK * N) % LANES == 0, "the packed block-diagonal products need chunk == head size"
    row = lax.broadcasted_iota(jnp.int32, (L, L), 0)
    col = lax.broadcasted_iota(jnp.int32, (L, L), 1)
    tri = jnp.where(row >= col, 1.0, 0.0).astype(BF16)
    ld_1 = logdecay.astype(BF16)
    ld_2 = (logdecay - ld_1.astype(F32)).astype(BF16)
    cums = []
    for ci in range(n_chunks):
        rs = slice(ci * L, (ci + 1) * L)
        cums.append(jnp.dot(tri, ld_1[rs], preferred_element_type=F32)
                    + jnp.dot(tri, ld_2[rs], preferred_element_type=F32))
    cum = jnp.concatenate(cums, axis=0) if n_chunks > 1 else cums[0]
    last = [cums[ci][L - 1:L, :] for ci in range(n_chunks)]
    cum_last = jnp.concatenate([jnp.broadcast_to(t, (L, C)) for t in last], axis=0) if n_chunks > 1 \
        else jnp.broadcast_to(last[0], (L, C))
    p_in = jnp.exp(cum)
    p_inv = jnp.exp(-cum)
    p_rem = jnp.exp(cum_last - cum)
    b = kkn * a
    at_f = (-kkn * jnp.exp(cum - logdecay)).astype(BF16)
    rt_f = r * p_in
    rt_b = rt_f.astype(BF16)
    bt_b = (b * p_inv).astype(BF16)
    kt_b = (k2 * p_inv).astype(BF16)
    br_b = (b * p_rem).astype(BF16)
    kr_b = (k2 * p_rem).astype(BF16)
    v_b = v.astype(BF16)

    dot = lambda x, y: jnp.dot(x.astype(BF16), y.astype(BF16), preferred_element_type=F32)
    dot_nt = lambda x, y: lax.dot_general(x.astype(BF16), y.astype(BF16), (((1,), (1,)), ((), ())),
                                          preferred_element_type=F32)
    dot_tn = lambda x, y: lax.dot_general(x.astype(BF16), y.astype(BF16), (((0,), (0,)), ((), ())),
                                          preferred_element_type=F32)

    PW = RWKV_PACK * N
    n_groups = C // PW
    lane_head = lax.broadcasted_iota(jnp.int32, (L, PW), 1) // N
    pcol = lax.broadcasted_iota(jnp.int32, (L, PW), 1) - lane_head * N
    prow = lax.broadcasted_iota(jnp.int32, (L, PW), 0)
    p_lower = prow > pcol
    p_lower_eq = prow >= pcol
    p_eye = jnp.where(prow == pcol, 1.0, 0.0)
    zero = jnp.zeros((), BF16)

    def bdiag(t):
        t = t.astype(BF16)
        return jnp.concatenate([jnp.where(lane_head == h, t, zero) for h in range(RWKV_PACK)], axis=0)

    def pack_rows(t):
        out = t[:N]
        for h in range(1, RWKV_PACK):
            out = jnp.where(lane_head == h, t[h * N:(h + 1) * N], out)
        return out

    cut = lambda t, ci, pj: t[ci * L:(ci + 1) * L, pj * PW:(pj + 1) * PW]
    same_head = (lax.broadcasted_iota(jnp.int32, (PW, PW), 0) // N) == \
                (lax.broadcasted_iota(jnp.int32, (PW, PW), 1) // N)
    rq, y0, gg, hh = {}, {}, {}, {}
    for g0 in range(0, n_chunks, RWKV_STAGE_GROUP):
        pieces = [(ci, pj) for ci in range(g0, min(g0 + RWKV_STAGE_GROUP, n_chunks)) for pj in range(n_groups)]
        _rwkv_chunk_algebra(pieces, cut, bdiag, dot, dot_nt, dot_tn, L, PW, at_f, rt_f, rt_b, bt_b, kt_b, br_b, kr_b,
                            v_b, p_lower, p_lower_eq, p_eye, same_head, pack_rows, rq, y0, gg, hh)

    ys = {}
    new_states = []
    for pj in range(n_groups):
        state = states[pj]
        for ci in range(n_chunks):
            p = (ci, pj)
            sb = state.astype(BF16)
            ys[p] = y0[p] + dot_nt(rq[p], bdiag(sb))
            p_tot = p_in[(ci + 1) * L - 1:(ci + 1) * L, pj * PW:(pj + 1) * PW]
            state = state * p_tot + dot(sb, gg[p]) + hh[p]
        new_states.append(state)
    y = jnp.concatenate([jnp.concatenate([ys[(ci, pj)] for pj in range(n_groups)], axis=1)
                         for ci in range(n_chunks)], axis=0)

    mu = head_sum(y) * (1.0 / N)
    dev = y - mu
    var = head_sum(dev * dev) * (1.0 / N)
    yn = dev * lax.rsqrt(var + RWKV_GN_EPS)
    bonus = head_sum(r * k2 * rk_ref[...]) * v
    y_ref[...] = ((yn * lnw_ref[...] + lnb_ref[...] + bonus) * g).astype(y_ref.dtype)
    return new_states


def _rwkv_chunk_algebra(pieces, cut, bdiag, dot, dot_nt, dot_tn, L, PW, at_f, rt_f, rt_b, bt_b, kt_b, br_b, kr_b, v_b,
                        p_lower, p_lower_eq, p_eye, same_head, pack_rows, rq_out, y0_out, gg_out, hh_out):
    at = {p: cut(at_f, *p) for p in pieces}
    vv = {p: cut(v_b, *p) for p in pieces}
    m = {p: dot_nt(jnp.concatenate([at[p], cut(rt_b, *p)], axis=0),
                   jnp.concatenate([bdiag(cut(bt_b, *p)), bdiag(cut(kt_b, *p))], axis=0)) for p in pieces}
    m_ab = {p: jnp.where(p_lower, m[p][:L, :PW], 0.0) for p in pieces}
    m_ak = {p: jnp.where(p_lower, m[p][:L, PW:], 0.0).astype(BF16) for p in pieces}
    m_rb = {p: jnp.where(p_lower_eq, m[p][L:, :PW], 0.0).astype(BF16) for p in pieces}
    m_rk = {p: jnp.where(p_lower_eq, m[p][L:, PW:], 0.0).astype(BF16) for p in pieces}
    inv = {p: p_eye + m_ab[p] for p in pieces}
    pw = {p: m_ab[p].astype(BF16) for p in pieces}
    pw = {p: dot(pw[p], bdiag(pw[p])).astype(BF16) for p in pieces}
    for _ in range(max(2, (L - 1).bit_length()) - 2):
        both = {p: dot(jnp.concatenate([pw[p], inv[p].astype(BF16)], axis=0), bdiag(pw[p])) for p in pieces}
        pw = {p: both[p][:L].astype(BF16) for p in pieces}
        inv = {p: inv[p] + both[p][L:] for p in pieces}
    inv = {p: (inv[p] + dot(inv[p], bdiag(pw[p]))).astype(BF16) for p in pieces}
    mv = {p: dot(jnp.concatenate([m_ak[p], m_rk[p]], axis=0), bdiag(vv[p])) for p in pieces}
    wu = {p: dot(inv[p], jnp.concatenate([bdiag(at[p]), bdiag(mv[p][:L])], axis=1)) for p in pieces}
    w = {p: wu[p][:, :PW].astype(BF16) for p in pieces}
    u0 = {p: wu[p][:, PW:].astype(BF16) for p in pieces}
    ry = {p: dot(m_rb[p], jnp.concatenate([bdiag(w[p]), bdiag(u0[p])], axis=1)) for p in pieces}
    rq_out.update({p: (cut(rt_f, *p) + ry[p][:, :PW]).astype(BF16) for p in pieces})
    y0_out.update({p: ry[p][:, PW:] + mv[p][L:] for p in pieces})
    gg_out.update({p: jnp.where(same_head, dot_tn(w[p], cut(br_b, *p)), 0.0).astype(BF16) for p in pieces})
    hh_full = {p: dot_tn(jnp.concatenate([u0[p], vv[p]], axis=0),
                         jnp.concatenate([cut(br_b, *p), cut(kr_b, *p)], axis=0)) for p in pieces}
    hh_out.update({p: pack_rows(hh_full[p]) for p in pieces})


def _rwkv(zr, w0, wd, a0, wa, wg, k_k, k_a, r_k, ln_w, ln_b, batch, seq):
    T = zr.shape[0]
    L = RWKV_CHUNK * RWKV_CHUNKS_PER_STEP
    nc = seq // L
    vec = lambda a: a.reshape(1, RWKV_WIDTH)
    full = lambda a: pl.BlockSpec(a.shape, lambda b, c: (0,) * a.ndim)
    args = (vec(w0), wd, vec(a0), wa, wg, vec(k_k), vec(k_a), vec(r_k), vec(ln_w), vec(ln_b))
    return pl.pallas_call(
        _rwkv_kernel,
        out_shape=jax.ShapeDtypeStruct((T, RWKV_WIDTH), BF16),
        grid=(batch, nc),
        in_specs=[pl.BlockSpec((L, SHIFT_COLS), lambda b, c: (b * nc + c, 0))] + [full(a) for a in args],
        out_specs=pl.BlockSpec((L, RWKV_WIDTH), lambda b, c: (b * nc + c, 0)),
        scratch_shapes=[pltpu.VMEM((RWKV_HEADS // RWKV_PACK, RWKV_HEAD, RWKV_PACK * RWKV_HEAD), F32)],
        compiler_params=_params("arbitrary", "arbitrary"),
    )(zr, *args)


def _merge_kernel(x_ref, ya_ref, yr_ref, gate_ref, wba_ref, wbr_ref, wo_ref, lnm_ref, wr_ref, br_ref,
                  x1_ref, h2_ref, route_ref, cnt_ref, cnt_scr):
    i = pl.program_id(0)

    @pl.when(i == 0)
    def _():
        cnt_scr[...] = jnp.zeros_like(cnt_scr)

    part = x_ref.shape[0] // MERGE_ROW_PARTS
    chains = [_merge_rows(x_ref, ya_ref, yr_ref, gate_ref, wba_ref, wbr_ref, wo_ref, lnm_ref, wr_ref, br_ref,
                          x1_ref, h2_ref, route_ref, cnt_ref, cnt_scr, j * part, part)
              for j in range(MERGE_ROW_PARTS)]
    for _ in zip(*chains):
        pass


def _merge_rows(x_ref, ya_ref, yr_ref, gate_ref, wba_ref, wbr_ref, wo_ref, lnm_ref, wr_ref, br_ref,
                x1_ref, h2_ref, route_ref, cnt_ref, cnt_scr, r0, tm):
    D = D_MODEL
    rows = slice(r0, r0 + tm)
    gates = gate_ref[rows, :].astype(F32)
    ya = jnp.dot(ya_ref[rows, :], wba_ref[...], preferred_element_type=F32)
    yr = jnp.dot(yr_ref[rows, :], wbr_ref[...], preferred_element_type=F32)
    yield
    merged = gates[:, :D] * ya + gates[:, D:] * yr
    x1 = x_ref[rows, :] + jnp.dot(merged.astype(BF16), wo_ref[...], preferred_element_type=F32)
    x1_ref[rows, :] = x1
    yield
    h2 = _rms(x1, lnm_ref[...])
    _store_slabs(h2_ref, h2, r0)

    h_hi = h2.astype(BF16)
    h_lo = (h2 - h_hi.astype(F32)).astype(BF16)
    parts = (jnp.dot(h_hi, wr_ref[...], preferred_element_type=F32)
             + jnp.dot(h_lo, wr_ref[...], preferred_element_type=F32))
    yield
    logits = parts[:, :LANES] + parts[:, LANES:] + br_ref[...]
    lane = lax.broadcasted_iota(jnp.int32, logits.shape, 1)
    big = jnp.int32(1 << 20)
    gl = jnp.where(lane < N_GROUPS, logits, NEG_INF)
    gmax = jnp.max(gl, axis=-1, keepdims=True)
    yield
    gidx = jnp.min(jnp.where(gl == gmax, lane, big), axis=-1, keepdims=True)
    gsum = jnp.sum(jnp.where(lane < N_GROUPS, jnp.exp(logits - gmax), 0.0), axis=-1, keepdims=True)
    yield
    g_w = 1.0 / gsum
    lo = N_GROUPS + EXPERTS_PER_GROUP * gidx
    el = jnp.where((lane >= lo) & (lane < lo + EXPERTS_PER_GROUP), logits, NEG_INF)
    m1 = jnp.max(el, axis=-1, keepdims=True)
    yield
    i1 = jnp.min(jnp.where(el == m1, lane, big), axis=-1, keepdims=True)
    yield
    el2 = jnp.where(lane == i1, NEG_INF, el)
    m2 = jnp.max(el2, axis=-1, keepdims=True)
    yield
    i2 = jnp.min(jnp.where(el2 == m2, lane, big), axis=-1, keepdims=True)
    yield
    d = jnp.exp(m2 - m1)
    w1 = g_w / (1.0 + d)
    w2 = g_w * d / (1.0 + d)
    e1 = i1 - N_GROUPS
    e2 = i2 - N_GROUPS

    hit1 = lane == e1
    hit2 = lane == e2
    onehot = jnp.where(hit1 | hit2, 1.0, 0.0).astype(BF16)
    r_i = lax.broadcasted_iota(jnp.int32, (tm, tm), 0)
    c_i = lax.broadcasted_iota(jnp.int32, (tm, tm), 1)
    before = jnp.dot(jnp.where(r_i > c_i, 1.0, 0.0).astype(BF16), onehot, preferred_element_type=F32)
    yield
    before = before + cnt_scr[...]
    rank1 = jnp.sum(jnp.where(hit1, before, 0.0), axis=-1, keepdims=True)
    rank2 = jnp.sum(jnp.where(hit2, before, 0.0), axis=-1, keepdims=True)
    cnt_scr[...] = cnt_scr[...] + jnp.sum(onehot.astype(F32), axis=0, keepdims=True)
    cnt_ref[...] = cnt_scr[...]

    route = jnp.where(lane == 0, e1.astype(F32), 0.0)
    route = jnp.where(lane == 1, e2.astype(F32), route)
    route = jnp.where(lane == 2, rank1, route)
    route = jnp.where(lane == 3, rank2, route)
    route = jnp.where(lane == 4, w1, route)
    route = jnp.where(lane == 5, w2, route)
    route_ref[rows, :] = route[:, :ROUTE_COLS]
    yield


def _merge(x2, y_att, y_rwkv, gates, wba, wbr, wo, ln_moe, w_router, b_router):
    T, D = x2.shape
    tm = MERGE_TILE
    row = lambda n: pl.BlockSpec((tm, n), lambda i: (i, 0))
    full = lambda a: pl.BlockSpec(a.shape, lambda i: (0,) * a.ndim)
    ln_moe = ln_moe.reshape(1, D)
    return pl.pallas_call(
        _merge_kernel,
        out_shape=(jax.ShapeDtypeStruct((T, D), F32), jax.ShapeDtypeStruct((T * SLAB_ROWS, LANES), F32),
                   jax.ShapeDtypeStruct((T, ROUTE_COLS), F32), jax.ShapeDtypeStruct((1, LANES), F32)),
        grid=(T // tm,),
        in_specs=[row(D), row(ATT_WIDTH), row(RWKV_WIDTH), row(GATE_COLS),
                  full(wba), full(wbr), full(wo), full(ln_moe), full(w_router), full(b_router)],
        out_specs=(row(D), pl.BlockSpec((tm * SLAB_ROWS, LANES), lambda i: (i, 0)), row(ROUTE_COLS),
                   pl.BlockSpec((1, LANES), lambda i: (0, 0))),
        scratch_shapes=[pltpu.VMEM((1, LANES), F32)],
        compiler_params=_params("arbitrary"),
    )(x2, y_att, y_rwkv, gates, wba, wbr, wo, ln_moe, w_router, b_router)


def _dest_kernel(starts_ref, route_t_ref, dest_ref):
    expert = route_t_ref[0:2, :].astype(jnp.int32)
    first_row = jnp.zeros_like(expert)
    for e in range(N_EXPERTS):
        first_row = jnp.where(expert == e, starts_ref[e], first_row)
    dest_ref[...] = first_row + route_t_ref[2:4, :].astype(jnp.int32)


def _dest_rows(route_t, starts):
    T = route_t.shape[1]
    tl = min(T, 8192)
    assert T % tl == 0
    return pl.pallas_call(
        _dest_kernel,
        out_shape=jax.ShapeDtypeStruct((2, T), jnp.int32),
        grid_spec=pltpu.PrefetchScalarGridSpec(
            num_scalar_prefetch=1,
            grid=(T // tl,),
            in_specs=[pl.BlockSpec((ROUTE_COLS, tl), lambda i, s: (0, i))],
            out_specs=pl.BlockSpec((2, tl), lambda i, s: (0, i)),
        ),
        compiler_params=_params("arbitrary"),
    )(starts, route_t)


def _dispatch_kernel(dest_ref, h_ref, xrows_hbm, row_sem, *, tile, tokens):
    base = pl.program_id(0) * tile

    def row_copy(j, slot):
        return pltpu.make_async_copy(_slab(h_ref, j), _slab(xrows_hbm, dest_ref[slot * tokens + base + j]), row_sem)

    def issue(j, carry):
        row_copy(j, 0).start(priority=0)
        row_copy(j, 1).start(priority=1)
        return carry

    lax.fori_loop(0, tile, issue, 0, unroll=8)

    def drain(j, carry):
        row_copy(j, 0).wait()
        row_copy(j, 1).wait()
        return carry

    lax.fori_loop(0, tile, drain, 0, unroll=8)


def _dispatch(h2_slabs, dest):
    T = h2_slabs.shape[0] // SLAB_ROWS
    tile = min(DISPATCH_TILE, T)
    assert T % tile == 0
    return pl.pallas_call(
        functools.partial(_dispatch_kernel, tile=tile, tokens=T),
        out_shape=jax.ShapeDtypeStruct((2 * T * SLAB_ROWS, LANES), F32),
        grid_spec=pltpu.PrefetchScalarGridSpec(
            num_scalar_prefetch=1,
            grid=(T // tile,),
            in_specs=[pl.BlockSpec((tile * SLAB_ROWS, LANES), lambda i, d: (i, 0))],
            out_specs=pl.BlockSpec(memory_space=pl.ANY),
            scratch_shapes=[pltpu.SemaphoreType.DMA],
        ),
        compiler_params=_params("arbitrary"),
    )(dest, h2_slabs)


def _expert_kernel(item_e, item_b, item_lo, item_hi, x_ref, wg_ref, wu_ref, wd_ref, y_ref, wgu_bf, wd_bf):
    w = pl.program_id(0)
    lo = item_lo[w]
    hi = item_hi[w]

    @pl.when((w == 0) | (item_e[w] != item_e[jnp.maximum(w - 1, 0)]))
    def _():
        wgu_bf[:, :D_EXPERT] = wg_ref[0].astype(BF16)
        wgu_bf[:, D_EXPERT:] = wu_ref[0].astype(BF16)
        wd_bf[...] = wd_ref[0].astype(BF16)

    def run(keep_other_rows):
        bm = x_ref.shape[0] // SLAB_ROWS
        xb = _load_slabs(x_ref, bm).astype(BF16)
        hgu = jnp.dot(xb, wgu_bf[...], preferred_element_type=F32)
        hg = hgu[:, :D_EXPERT]
        hid = hg * _sigmoid(hg) * hgu[:, D_EXPERT:]
        y = jnp.dot(hid.astype(BF16), wd_bf[...], preferred_element_type=F32)
        if keep_other_rows:
            rows = lax.broadcasted_iota(jnp.int32, y.shape, 0)
            y = jnp.where((rows >= lo) & (rows < hi), y, _load_slabs(y_ref, bm))
        _store_slabs(y_ref, y)

    pl.when((hi > lo) & (lo == 0))(functools.partial(run, False))
    pl.when((hi > lo) & (lo > 0))(functools.partial(run, True))


def _experts(xrows, items, wg, wu, wd):
    bm = EXPERT_BLOCK
    D = D_MODEL
    n_items = items[0].shape[0]
    block = pl.BlockSpec((bm * SLAB_ROWS, LANES), lambda w, ie, ib, lo, hi: (ib[w], 0))
    return pl.pallas_call(
        _expert_kernel,
        out_shape=jax.ShapeDtypeStruct(xrows.shape, F32),
        grid_spec=pltpu.PrefetchScalarGridSpec(
            num_scalar_prefetch=4,
            grid=(n_items,),
            in_specs=[block,
                      pl.BlockSpec((1, D, D_EXPERT), lambda w, ie, ib, lo, hi: (ie[w], 0, 0)),
                      pl.BlockSpec((1, D, D_EXPERT), lambda w, ie, ib, lo, hi: (ie[w], 0, 0)),
                      pl.BlockSpec((1, D_EXPERT, D), lambda w, ie, ib, lo, hi: (ie[w], 0, 0))],
            out_specs=block,
            scratch_shapes=[pltpu.VMEM((D, 2 * D_EXPERT), BF16), pltpu.VMEM((D_EXPERT, D), BF16)],
        ),
        compiler_params=_params("arbitrary"),
    )(*items, xrows, wg, wu, wd)


def _work_items(counts, n_rows):
    bm = EXPERT_BLOCK
    nb = n_rows // bm
    n_items = nb + N_EXPERTS
    ends = jnp.cumsum(counts)
    starts = ends - counts
    first_blk = starts // bm
    last_blk = (ends - 1) // bm
    per_e = jnp.where(counts > 0, last_blk - first_blk + 1, 0)
    item_end = jnp.cumsum(per_e)
    total = item_end[-1]
    w = jnp.arange(n_items, dtype=jnp.int32)
    wc = jnp.minimum(w, total - 1)
    e = jnp.minimum(jnp.sum((item_end[None, :] <= wc[:, None]).astype(jnp.int32), axis=1), N_EXPERTS - 1)
    b = first_blk[e] + (wc - (item_end[e] - per_e[e]))
    lo = jnp.maximum(starts[e], b * bm) - b * bm
    hi = jnp.minimum(ends[e], (b + 1) * bm) - b * bm
    live = w < total
    lo = jnp.where(live, lo, 0)
    hi = jnp.where(live, hi, 0)
    return (e.astype(jnp.int32), b.astype(jnp.int32), lo.astype(jnp.int32), hi.astype(jnp.int32)), starts


def _final_kernel(dest_ref, yrows_hbm, x1_ref, route_ref, p_ref, lnp_ref, wpg_ref, wpp_ref, lnf_ref,
                  o_ref, rows_ref, row_sems):
    i = pl.program_id(0)
    n = pl.num_programs(0)
    tm = x1_ref.shape[0]
    tokens = n * tm

    def row_copy(tile, j, slot, par):
        return pltpu.make_async_copy(_slab(yrows_hbm, dest_ref[slot * tokens + tile * tm + j]),
                                     _slab(rows_ref.at[par, slot], j), row_sems.at[par])

    def wait_tile(par):
        pltpu.make_async_copy(rows_ref.at[par], rows_ref.at[par], row_sems.at[par]).wait()

    @pl.when(i == 0)
    def _():
        def body(j, carry):
            row_copy(0, j, 0, 0).start(priority=0)
            row_copy(0, j, 1, 0).start(priority=1)
            return carry
        lax.fori_loop(0, tm, body, 0, unroll=8)

    def step(par):
        wait_tile(par)
        nxt = jnp.minimum(i + 1, n - 1)
        for j in range(tm):
            row_copy(nxt, j, 0, 1 - par).start(priority=0)
            row_copy(nxt, j, 1, 1 - par).start(priority=1)
        pp = jnp.dot(p_ref[...].astype(BF16), wpp_ref[...], preferred_element_type=F32)
        route = route_ref[...]
        x2 = (x1_ref[...] + route[:, 4:5] * _load_slabs(rows_ref.at[par, 0], tm)
              + route[:, 5:6] * _load_slabs(rows_ref.at[par, 1], tm))
        gate = _sigmoid(jnp.dot(_rms(x2, lnp_ref[...]).astype(BF16), wpg_ref[...], preferred_element_type=F32))
        x3 = x2 + gate * pp
        o_ref[...] = _rms(x3, lnf_ref[...])

        @pl.when(i == n - 1)
        def _():
            wait_tile(1 - par)

    for par in range(2):
        pl.when(i % 2 == par)(functools.partial(step, par))


def _final(dest, yrows, x1, route, p2, ln_ple, wpg, wpp, ln_final):
    T, D = x1.shape
    tm = GATHER_TILE
    row = lambda n: pl.BlockSpec((tm, n), lambda i, d: (i, 0))
    full = lambda a: pl.BlockSpec(a.shape, lambda i, d: (0,) * a.ndim)
    ln_ple = ln_ple.reshape(1, D)
    ln_final = ln_final.reshape(1, D)
    return pl.pallas_call(
        _final_kernel,
        out_shape=jax.ShapeDtypeStruct((T, D), F32),
        grid_spec=pltpu.PrefetchScalarGridSpec(
            num_scalar_prefetch=1,
            grid=(T // tm,),
            in_specs=[pl.BlockSpec(memory_space=pl.ANY),
                      row(D), row(ROUTE_COLS), row(PLE_DIM), full(ln_ple), full(wpg), full(wpp), full(ln_final)],
            out_specs=row(D),
            scratch_shapes=[pltpu.VMEM((2, 2, tm * SLAB_ROWS, LANES), F32), pltpu.SemaphoreType.DMA((2,))],
        ),
        compiler_params=_params("arbitrary"),
    )(dest, yrows, x1, route, p2, ln_ple, wpg, wpp, ln_final)


def kernel(x, p, positions, ln_mix, w_in, mu_shift, w0, w_decay_up, a0, w_aaa_up, w_gate_up, k_k, k_a, r_k, ln_x_w, ln_x_b, sinks, w_branch_att, w_branch_rwkv, w_out, ln_moe, w_group, b_group, w_expert, b_expert, w_gate_e, w_up_e, w_down_e, ln_ple, w_ple_gate, w_ple_proj, ln_final):
    B, S, D = x.shape
    T = B * S
    depth = w_in.shape[0]
    assert D == D_MODEL and S % ROW_TILE == 0 and S % (WINDOW * ATT_BLOCKS_PER_STEP) == 0 and S % (RWKV_CHUNK * RWKV_CHUNKS_PER_STEP) == 0
    assert T % GATHER_TILE == 0 and (2 * T) % EXPERT_BLOCK == 0
    cos, sin = _rope_tables(positions)
    x2 = x.reshape(T, D)
    out = None
    for i in range(depth):
        q, k, v, zr, gates = _inproj(x2, ln_mix[i], w_in[i].astype(BF16), mu_shift[i], cos, sin, S)
        y_att = _attention(q, k, v, sinks[i], B, S)
        y_rwkv = _rwkv(zr, w0[i], w_decay_up[i], a0[i], w_aaa_up[i], w_gate_up[i], k_k[i], k_a[i], r_k[i],
                       ln_x_w[i], ln_x_b[i], B, S)
        pad = LANES - N_GROUPS - N_EXPERTS
        w_router = jnp.concatenate([w_group[i], w_expert[i], jnp.zeros((D, pad), F32)], axis=1)
        w_router_hi = w_router.astype(BF16)
        w_router = jnp.concatenate([w_router_hi, (w_router - w_router_hi.astype(F32)).astype(BF16)], axis=1)
        b_router = jnp.concatenate([b_group[i], b_expert[i], jnp.zeros((pad,), F32)]).reshape(1, LANES)
        x1, h2, route, cnt = _merge(x2, y_att, y_rwkv, gates, w_branch_att[i].astype(BF16),
                                    w_branch_rwkv[i].astype(BF16), w_out[i].astype(BF16), ln_moe[i],
                                    w_router, b_router)
        counts = cnt[0, :N_EXPERTS].astype(jnp.int32)
        items, starts = _work_items(counts, 2 * T)
        dest = _dest_rows(route.T, starts.astype(jnp.int32)).reshape(-1)
        xrows = _dispatch(h2, dest)
        yrows = _experts(xrows, items, w_gate_e[i], w_up_e[i], w_down_e[i])
        last = i == depth - 1
        assert last, "the final-norm kernel closes the only layer"
        out = _final(dest, yrows, x1, route, p[i].reshape(T, PLE_DIM), ln_ple[i], w_ple_gate[i].astype(BF16),
                     w_ple_proj[i].astype(BF16), ln_final)
    return out.reshape(B, S, D)
```

```python
import functools
import math

import jax
import jax.numpy as jnp
from jax import lax
from jax.experimental import pallas as pl
from jax.experimental.pallas import tpu as pltpu

F32 = jnp.float32
BF16 = jnp.bfloat16

D_MODEL = 1024
PLE_DIM = 256
ATT_HEADS = 8
ATT_KV_HEADS = 2
HEAD_DIM = 64
ATT_WIDTH = ATT_HEADS * HEAD_DIM
KV_WIDTH = ATT_KV_HEADS * HEAD_DIM
WINDOW = 128
ROPE_THETA = 10000.0
RWKV_HEADS = 8
RWKV_HEAD = 64
RWKV_WIDTH = RWKV_HEADS * RWKV_HEAD
DECAY_LORA = 64
AAA_LORA = 64
GATE_LORA = 128
RWKV_GN_EPS = 64e-5
ATT_COLS = ATT_WIDTH + 2 * KV_WIDTH
SHIFT_COLS = 3 * RWKV_WIDTH + DECAY_LORA + AAA_LORA + GATE_LORA
GATE_COLS = 2 * D_MODEL
N_GROUPS = 4
EXPERTS_PER_GROUP = 8
N_EXPERTS = N_GROUPS * EXPERTS_PER_GROUP
D_EXPERT = 512
NORM_EPS = 1e-6
NEG_INF = -1e30

LANES = 128
ROUTE_COLS = 8
SLAB_ROWS = D_MODEL // LANES
VMEM_LIMIT = 56 * 1024 * 1024

ROW_TILE = 512
MERGE_TILE = 1024
MERGE_ROW_PARTS = 4
ATT_BLOCKS_PER_STEP = 8
RWKV_CHUNK = 64
RWKV_CHUNKS_PER_STEP = 8
RWKV_PACK = 2
RWKV_STAGE_GROUP = 4
EXPERT_BLOCK = 1024
GATHER_TILE = 256
DISPATCH_TILE = 512


def _params(*sem):
    return pltpu.CompilerParams(dimension_semantics=sem, vmem_limit_bytes=VMEM_LIMIT)


def _bdot(a, b):
    return jnp.dot(a.astype(BF16), b.astype(BF16), preferred_element_type=F32)


def _rms(x, g):
    return x * lax.rsqrt(jnp.mean(x * x, axis=-1, keepdims=True) + NORM_EPS) * g


def _sigmoid(x):
    return 1.0 / (1.0 + jnp.exp(-x))


def _store_slabs(ref, val, first=0):
    m = val.shape[0]
    for j in range(SLAB_ROWS):
        ref[pl.ds(first * SLAB_ROWS + j, m, stride=SLAB_ROWS), :] = val[:, j * LANES:(j + 1) * LANES]


def _load_slabs(ref, m, first=0):
    return jnp.concatenate([ref[pl.ds(first * SLAB_ROWS + j, m, stride=SLAB_ROWS), :] for j in range(SLAB_ROWS)],
                           axis=1)


def _slab(ref, index):
    return ref.at[pl.ds(pl.multiple_of(index * SLAB_ROWS, SLAB_ROWS), SLAB_ROWS)]


def _rope_table_kernel(pos_ref, invf_ref, cos_ref, sin_ref):
    half = HEAD_DIM // 2
    per_row = LANES // half
    rows = pos_ref.shape[0]
    group = lax.broadcasted_iota(jnp.int32, (rows, LANES), 1) // half
    pos = pos_ref[...].astype(F32)
    pos_lanes = jnp.broadcast_to(pos[:, per_row - 1:per_row], (rows, LANES))
    for m in range(per_row - 1):
        pos_lanes = jnp.where(group == m, pos[:, m:m + 1], pos_lanes)
    ang = pos_lanes * invf_ref[...]
    sign = jnp.where(group % 2 == 0, -1.0, 1.0)
    for table, out_ref, scale in ((jnp.cos(ang), cos_ref, None), (jnp.sin(ang), sin_ref, sign)):
        rolled = [table] + [pltpu.roll(table, half * j, 1) for j in range(1, per_row)]
        for m in range(per_row):
            out = rolled[(per_row - 1 - m) % per_row]
            for g in range(per_row - 1):
                out = jnp.where(group == g, rolled[(g - m) % per_row], out)
            out_ref[pl.ds(m, rows, stride=per_row), :] = out if scale is None else out * scale


def _rope_tables(positions):
    T = positions.size
    half = HEAD_DIM // 2
    per_row = LANES // half
    inv_freq = ROPE_THETA ** (-jnp.arange(half, dtype=F32) / half)
    invf = jnp.tile(inv_freq, per_row).reshape(1, LANES)
    pos = positions.reshape(T // per_row, per_row)
    rows = T // per_row
    tr = min(rows, 1024)
    return pl.pallas_call(
        _rope_table_kernel,
        out_shape=(jax.ShapeDtypeStruct((T, LANES), F32),) * 2,
        grid=(rows // tr,),
        in_specs=[pl.BlockSpec((tr, per_row), lambda i: (i, 0)),
                  pl.BlockSpec((1, LANES), lambda i: (0, 0))],
        out_specs=(pl.BlockSpec((tr * per_row, LANES), lambda i: (i, 0)),) * 2,
        compiler_params=_params("arbitrary"),
    )(pos, invf)


def _rope(t, cos, sin):
    n = t.shape[1]
    reps = n // LANES
    c = jnp.tile(cos, (1, reps)) if reps > 1 else cos
    s = jnp.tile(sin, (1, reps)) if reps > 1 else sin
    lane = lax.broadcasted_iota(jnp.int32, t.shape, 1)
    first_half = (lane % HEAD_DIM) < (HEAD_DIM // 2)
    partner = jnp.where(first_half, pltpu.roll(t, n - HEAD_DIM // 2, 1), pltpu.roll(t, HEAD_DIM // 2, 1))
    return t * c + partner * s


def _inproj_kernel(x_ref, g_ref, w_ref, mu_ref, cos_ref, sin_ref,
                   q_ref, k_ref, v_ref, zr_ref, gate_ref, carry_ref, *, tiles_per_seq):
    i = pl.program_id(0)
    tm = x_ref.shape[0]
    h = _rms(x_ref[...], g_ref[...]).astype(BF16)
    cos = cos_ref[...]
    sin = sin_ref[...]

    za = jnp.dot(h, w_ref[:, :ATT_COLS], preferred_element_type=F32)
    q_ref[...] = (_rope(za[:, :ATT_WIDTH], cos, sin) * (HEAD_DIM ** -0.5)).astype(BF16)

    def per_head_doubled(t):
        lane = lax.broadcasted_iota(jnp.int32, t.shape, 1)
        swapped = pltpu.roll(t, HEAD_DIM, 1)
        low = lane < HEAD_DIM
        return jnp.concatenate([jnp.where(low, t, swapped), jnp.where(low, swapped, t)], axis=1).astype(BF16)

    k_ref[...] = per_head_doubled(_rope(za[:, ATT_WIDTH:ATT_WIDTH + KV_WIDTH], cos, sin))
    v_ref[...] = per_head_doubled(za[:, ATT_WIDTH + KV_WIDTH:])

    zs = jnp.dot(h, w_ref[:, ATT_COLS:ATT_COLS + SHIFT_COLS], preferred_element_type=F32)
    row = lax.broadcasted_iota(jnp.int32, zs.shape, 0)
    seq_start = (i % tiles_per_seq) == 0
    before = jnp.where(seq_start, 0.0, carry_ref[0:1, :])
    prev = jnp.where(row == 0, before, pltpu.roll(zs, 1, 0))
    carry_ref[0:1, :] = zs[tm - 1:tm, :]
    zr_ref[...] = zs + (prev - zs) * mu_ref[...]

    zg = jnp.dot(h, w_ref[:, ATT_COLS + SHIFT_COLS:], preferred_element_type=F32)
    gate_ref[...] = _sigmoid(zg).astype(BF16)


def _inproj(x2, ln, w_in, mu, cos, sin, seq):
    T, D = x2.shape
    tm = ROW_TILE
    row = lambda n: pl.BlockSpec((tm, n), lambda i: (i, 0))
    full = lambda a: pl.BlockSpec(a.shape, lambda i: (0,) * a.ndim)
    resident = lambda a: pl.BlockSpec(a.shape, lambda i: (0,) * a.ndim, pipeline_mode=pl.Buffered(1))
    ln = ln.reshape(1, D)
    mu = mu.reshape(1, SHIFT_COLS)
    return pl.pallas_call(
        functools.partial(_inproj_kernel, tiles_per_seq=seq // tm),
        out_shape=(jax.ShapeDtypeStruct((T, ATT_WIDTH), BF16),
                   jax.ShapeDtypeStruct((T, 2 * KV_WIDTH), BF16),
                   jax.ShapeDtypeStruct((T, 2 * KV_WIDTH), BF16),
                   jax.ShapeDtypeStruct((T, SHIFT_COLS), F32),
                   jax.ShapeDtypeStruct((T, GATE_COLS), BF16)),
        grid=(T // tm,),
        in_specs=[row(D), full(ln), resident(w_in), full(mu), row(LANES), row(LANES)],
        out_specs=(row(ATT_WIDTH), row(2 * KV_WIDTH), row(2 * KV_WIDTH), row(SHIFT_COLS), row(GATE_COLS)),
        scratch_shapes=[pltpu.VMEM((8, SHIFT_COLS), F32)],
        compiler_params=_params("arbitrary"),
    )(x2, ln, w_in, mu, cos, sin)


def _attn_kernel(sink_ref, q_ref, kp_ref, kc_ref, vp_ref, vc_ref, o_ref):
    n = pl.program_id(1)
    bq = WINDOW
    grp = ATT_HEADS // ATT_KV_HEADS
    gw = grp * HEAD_DIM
    kall = jnp.concatenate([kp_ref[...], kc_ref[...]], axis=0)
    vall = jnp.concatenate([vp_ref[...], vc_ref[...]], axis=0)
    qi = lax.broadcasted_iota(jnp.int32, (grp * bq, 2 * bq), 0) % bq
    si = lax.broadcasted_iota(jnp.int32, (grp * bq, 2 * bq), 1)
    diff = qi + bq - si
    band = (diff >= 0) & (diff < WINDOW)
    head_row = lax.broadcasted_iota(jnp.int32, (grp * bq, 1), 0) // bq
    lane_head = lax.broadcasted_iota(jnp.int32, (bq, gw), 1) // HEAD_DIM
    zero = jnp.zeros((), BF16)
    for blk in range(q_ref.shape[0] // bq):
        valid = band & ((si >= bq) | (n > 0)) if blk == 0 else band
        outs = []
        for j in range(ATT_KV_HEADS):
            qg = q_ref[blk * bq:(blk + 1) * bq, j * gw:(j + 1) * gw]
            lhs = jnp.concatenate([jnp.where(lane_head == g, qg, zero) for g in range(grp)], axis=0)
            k_half = kall[blk * bq:(blk + 2) * bq, j * 2 * HEAD_DIM:(j + 1) * 2 * HEAD_DIM]
            v_half = vall[blk * bq:(blk + 2) * bq, j * 2 * HEAD_DIM:(j + 1) * 2 * HEAD_DIM]
            k_rep = jnp.concatenate([k_half, k_half], axis=1)
            v_rep = jnp.concatenate([v_half, v_half], axis=1)
            s = lax.dot_general(lhs, k_rep, (((1,), (1,)), ((), ())), preferred_element_type=F32)
            s = jnp.where(valid, s, NEG_INF)
            sink = jnp.zeros((grp * bq, 1), F32)
            for g in range(grp):
                sink = jnp.where(head_row == g, sink_ref[j * grp + g], sink)
            m = jnp.maximum(jnp.max(s, axis=-1, keepdims=True), sink)
            e = jnp.exp(s - m)
            denom = jnp.sum(e, axis=-1, keepdims=True) + jnp.exp(sink - m)
            pv = jnp.dot(e.astype(BF16), v_rep, preferred_element_type=F32) / denom
            out = pv[:bq]
            for g in range(1, grp):
                out = jnp.where(lane_head == g, pv[g * bq:(g + 1) * bq], out)
            outs.append(out)
        o_ref[blk * bq:(blk + 1) * bq, :] = jnp.concatenate(outs, axis=1).astype(o_ref.dtype)


def _attention(q, k, v, sinks, batch, seq):
    T = q.shape[0]
    bq = WINDOW
    per_step = ATT_BLOCKS_PER_STEP
    ns = seq // (bq * per_step)
    cur = lambda b, n: (b * ns + n, 0)
    prev = lambda b, n: (jnp.maximum((b * ns + n) * per_step - 1, 0), 0)
    return pl.pallas_call(
        _attn_kernel,
        out_shape=jax.ShapeDtypeStruct((T, ATT_WIDTH), BF16),
        grid=(batch, ns),
        in_specs=[pl.BlockSpec(memory_space=pltpu.SMEM),
                  pl.BlockSpec((bq * per_step, ATT_WIDTH), cur),
                  pl.BlockSpec((bq, 2 * KV_WIDTH), prev), pl.BlockSpec((bq * per_step, 2 * KV_WIDTH), cur),
                  pl.BlockSpec((bq, 2 * KV_WIDTH), prev), pl.BlockSpec((bq * per_step, 2 * KV_WIDTH), cur)],
        out_specs=pl.BlockSpec((bq * per_step, ATT_WIDTH), cur),
        compiler_params=_params("arbitrary", "arbitrary"),
    )(sinks, q, k, k, v, v)


def _rwkv_kernel(z_ref, w0_ref, wd_ref, a0_ref, wa_ref, wg_ref, kk_ref, ka_ref, rk_ref, lnw_ref, lnb_ref,
                 y_ref, s_ref):
    c = pl.program_id(1)

    @pl.when(c == 0)
    def _():
        s_ref[...] = jnp.zeros_like(s_ref)

    states = _rwkv_rows(z_ref, w0_ref, wd_ref, a0_ref, wa_ref, wg_ref, kk_ref, ka_ref, rk_ref, lnw_ref, lnb_ref,
                        y_ref, [s_ref[pj] for pj in range(s_ref.shape[0])])
    for pj, state in enumerate(states):
        s_ref[pj] = state


def _rwkv_rows(z_ref, w0_ref, wd_ref, a0_ref, wa_ref, wg_ref, kk_ref, ka_ref, rk_ref, lnw_ref, lnb_ref, y_ref, states):
    L = RWKV_CHUNK
    rows = z_ref.shape[0]
    n_chunks = rows // L
    C = RWKV_WIDTH
    N = RWKV_HEAD
    r = z_ref[:, 0:C]
    k = z_ref[:, C:2 * C]
    v = z_ref[:, 2 * C:3 * C]
    xw = z_ref[:, 3 * C:3 * C + DECAY_LORA]
    xa = z_ref[:, 3 * C + DECAY_LORA:3 * C + DECAY_LORA + AAA_LORA]
    xg = z_ref[:, 3 * C + DECAY_LORA + AAA_LORA:]

    wlin = w0_ref[...] + _bdot(jnp.tanh(xw), wd_ref[...])
    logdecay = -math.exp(-0.5) * _sigmoid(wlin)
    a = _sigmoid(a0_ref[...] + _bdot(xa, wa_ref[...]))
    g = _bdot(_sigmoid(xg), wg_ref[...])
    kk = k * kk_ref[...]
    k2 = k * (1.0 + (a - 1.0) * ka_ref[...])

    slab = 2 * LANES
    hr = lax.broadcasted_iota(jnp.int32, (slab, slab), 0) // N
    hc = lax.broadcasted_iota(jnp.int32, (slab, slab), 1) // N
    head_ones = jnp.where(hr == hc, 1.0, 0.0).astype(BF16)

    def head_sum(t):
        t = t.astype(BF16)
        return jnp.concatenate([jnp.dot(t[:, j * slab:(j + 1) * slab], head_ones, preferred_element_type=F32)
                                for j in range(C // slab)], axis=1)

    kkn = kk / jnp.maximum(jnp.sqrt(head_sum(kk * kk)), 1e-12)

    assert L == N and (RWKV_PACK * N) % LANES == 0, "the packed block-diagonal products need chunk == head size"
    row = lax.broadcasted_iota(jnp.int32, (L, L), 0)
    col = lax.broadcasted_iota(jnp.int32, (L, L), 1)
    tri = jnp.where(row >= col, 1.0, 0.0).astype(BF16)
    ld_1 = logdecay.astype(BF16)
    ld_2 = (logdecay - ld_1.astype(F32)).astype(BF16)
    cums = []
    for ci in range(n_chunks):
        rs = slice(ci * L, (ci + 1) * L)
        cums.append(jnp.dot(tri, ld_1[rs], preferred_element_type=F32)
                    + jnp.dot(tri, ld_2[rs], preferred_element_type=F32))
    cum = jnp.concatenate(cums, axis=0) if n_chunks > 1 else cums[0]
    last = [cums[ci][L - 1:L, :] for ci in range(n_chunks)]
    cum_last = jnp.concatenate([jnp.broadcast_to(t, (L, C)) for t in last], axis=0) if n_chunks > 1 \
        else jnp.broadcast_to(last[0], (L, C))
    p_in = jnp.exp(cum)
    p_inv = jnp.exp(-cum)
    p_rem = jnp.exp(cum_last - cum)
    b = kkn * a
    at_f = (-kkn * jnp.exp(cum - logdecay)).astype(BF16)
    rt_f = r * p_in
    rt_b = rt_f.astype(BF16)
    bt_b = (b * p_inv).astype(BF16)
    kt_b = (k2 * p_inv).astype(BF16)
    br_b = (b * p_rem).astype(BF16)
    kr_b = (k2 * p_rem).astype(BF16)
    v_b = v.astype(BF16)

    dot = lambda x, y: jnp.dot(x.astype(BF16), y.astype(BF16), preferred_element_type=F32)
    dot_nt = lambda x, y: lax.dot_general(x.astype(BF16), y.astype(BF16), (((1,), (1,)), ((), ())),
                                          preferred_element_type=F32)
    dot_tn = lambda x, y: lax.dot_general(x.astype(BF16), y.astype(BF16), (((0,), (0,)), ((), ())),
                                          preferred_element_type=F32)

    PW = RWKV_PACK * N
    n_groups = C // PW
    lane_head = lax.broadcasted_iota(jnp.int32, (L, PW), 1) // N
    pcol = lax.broadcasted_iota(jnp.int32, (L, PW), 1) - lane_head * N
    prow = lax.broadcasted_iota(jnp.int32, (L, PW), 0)
    p_lower = prow > pcol
    p_lower_eq = prow >= pcol
    p_eye = jnp.where(prow == pcol, 1.0, 0.0)
    zero = jnp.zeros((), BF16)

    def bdiag(t):
        t = t.astype(BF16)
        return jnp.concatenate([jnp.where(lane_head == h, t, zero) for h in range(RWKV_PACK)], axis=0)

    def pack_rows(t):
        out = t[:N]
        for h in range(1, RWKV_PACK):
            out = jnp.where(lane_head == h, t[h * N:(h + 1) * N], out)
        return out

    cut = lambda t, ci, pj: t[ci * L:(ci + 1) * L, pj * PW:(pj + 1) * PW]
    same_head = (lax.broadcasted_iota(jnp.int32, (PW, PW), 0) // N) == \
                (lax.broadcasted_iota(jnp.int32, (PW, PW), 1) // N)
    rq, y0, gg, hh = {}, {}, {}, {}
    for g0 in range(0, n_chunks, RWKV_STAGE_GROUP):
        pieces = [(ci, pj) for ci in range(g0, min(g0 + RWKV_STAGE_GROUP, n_chunks)) for pj in range(n_groups)]
        _rwkv_chunk_algebra(pieces, cut, bdiag, dot, dot_nt, dot_tn, L, PW, at_f, rt_f, rt_b, bt_b, kt_b, br_b, kr_b,
                            v_b, p_lower, p_lower_eq, p_eye, same_head, pack_rows, rq, y0, gg, hh)

    ys = {}
    new_states = []
    for pj in range(n_groups):
        state = states[pj]
        for ci in range(n_chunks):
            p = (ci, pj)
            sb = state.astype(BF16)
            ys[p] = y0[p] + dot_nt(rq[p], bdiag(sb))
            p_tot = p_in[(ci + 1) * L - 1:(ci + 1) * L, pj * PW:(pj + 1) * PW]
            state = state * p_tot + dot(sb, gg[p]) + hh[p]
        new_states.append(state)
    y = jnp.concatenate([jnp.concatenate([ys[(ci, pj)] for pj in range(n_groups)], axis=1)
                         for ci in range(n_chunks)], axis=0)

    mu = head_sum(y) * (1.0 / N)
    dev = y - mu
    var = head_sum(dev * dev) * (1.0 / N)
    yn = dev * lax.rsqrt(var + RWKV_GN_EPS)
    bonus = head_sum(r * k2 * rk_ref[...]) * v
    y_ref[...] = ((yn * lnw_ref[...] + lnb_ref[...] + bonus) * g).astype(y_ref.dtype)
    return new_states


def _rwkv_chunk_algebra(pieces, cut, bdiag, dot, dot_nt, dot_tn, L, PW, at_f, rt_f, rt_b, bt_b, kt_b, br_b, kr_b, v_b,
                        p_lower, p_lower_eq, p_eye, same_head, pack_rows, rq_out, y0_out, gg_out, hh_out):
    at = {p: cut(at_f, *p) for p in pieces}
    vv = {p: cut(v_b, *p) for p in pieces}
    m = {p: dot_nt(jnp.concatenate([at[p], cut(rt_b, *p)], axis=0),
                   jnp.concatenate([bdiag(cut(bt_b, *p)), bdiag(cut(kt_b, *p))], axis=0)) for p in pieces}
    m_ab = {p: jnp.where(p_lower, m[p][:L, :PW], 0.0) for p in pieces}
    m_ak = {p: jnp.where(p_lower, m[p][:L, PW:], 0.0).astype(BF16) for p in pieces}
    m_rb = {p: jnp.where(p_lower_eq, m[p][L:, :PW], 0.0).astype(BF16) for p in pieces}
    m_rk = {p: jnp.where(p_lower_eq, m[p][L:, PW:], 0.0).astype(BF16) for p in pieces}
    inv = {p: p_eye + m_ab[p] for p in pieces}
    pw = {p: m_ab[p].astype(BF16) for p in pieces}
    pw = {p: dot(pw[p], bdiag(pw[p])).astype(BF16) for p in pieces}
    for _ in range(max(2, (L - 1).bit_length()) - 2):
        both = {p: dot(jnp.concatenate([pw[p], inv[p].astype(BF16)], axis=0), bdiag(pw[p])) for p in pieces}
        pw = {p: both[p][:L].astype(BF16) for p in pieces}
        inv = {p: inv[p] + both[p][L:] for p in pieces}
    inv = {p: (inv[p] + dot(inv[p], bdiag(pw[p]))).astype(BF16) for p in pieces}
    mv = {p: dot(jnp.concatenate([m_ak[p], m_rk[p]], axis=0), bdiag(vv[p])) for p in pieces}
    wu = {p: dot(inv[p], jnp.concatenate([bdiag(at[p]), bdiag(mv[p][:L])], axis=1)) for p in pieces}
    w = {p: wu[p][:, :PW].astype(BF16) for p in pieces}
    u0 = {p: wu[p][:, PW:].astype(BF16) for p in pieces}
    ry = {p: dot(m_rb[p], jnp.concatenate([bdiag(w[p]), bdiag(u0[p])], axis=1)) for p in pieces}
    rq_out.update({p: (cut(rt_f, *p) + ry[p][:, :PW]).astype(BF16) for p in pieces})
    y0_out.update({p: ry[p][:, PW:] + mv[p][L:] for p in pieces})
    gg_out.update({p: jnp.where(same_head, dot_tn(w[p], cut(br_b, *p)), 0.0).astype(BF16) for p in pieces})
    hh_full = {p: dot_tn(jnp.concatenate([u0[p], vv[p]], axis=0),
                         jnp.concatenate([cut(br_b, *p), cut(kr_b, *p)], axis=0)) for p in pieces}
    hh_out.update({p: pack_rows(hh_full[p]) for p in pieces})


def _rwkv(zr, w0, wd, a0, wa, wg, k_k, k_a, r_k, ln_w, ln_b, batch, seq):
    T = zr.shape[0]
    L = RWKV_CHUNK * RWKV_CHUNKS_PER_STEP
    nc = seq // L
    vec = lambda a: a.reshape(1, RWKV_WIDTH)
    full = lambda a: pl.BlockSpec(a.shape, lambda b, c: (0,) * a.ndim)
    args = (vec(w0), wd, vec(a0), wa, wg, vec(k_k), vec(k_a), vec(r_k), vec(ln_w), vec(ln_b))
    return pl.pallas_call(
        _rwkv_kernel,
        out_shape=jax.ShapeDtypeStruct((T, RWKV_WIDTH), BF16),
        grid=(batch, nc),
        in_specs=[pl.BlockSpec((L, SHIFT_COLS), lambda b, c: (b * nc + c, 0))] + [full(a) for a in args],
        out_specs=pl.BlockSpec((L, RWKV_WIDTH), lambda b, c: (b * nc + c, 0)),
        scratch_shapes=[pltpu.VMEM((RWKV_HEADS // RWKV_PACK, RWKV_HEAD, RWKV_PACK * RWKV_HEAD), F32)],
        compiler_params=_params("arbitrary", "arbitrary"),
    )(zr, *args)


def _merge_kernel(x_ref, ya_ref, yr_ref, gate_ref, wba_ref, wbr_ref, wo_ref, lnm_ref, wr_ref, br_ref,
                  x1_ref, h2_ref, route_ref, cnt_ref, cnt_scr):
    i = pl.program_id(0)

    @pl.when(i == 0)
    def _():
        cnt_scr[...] = jnp.zeros_like(cnt_scr)

    part = x_ref.shape[0] // MERGE_ROW_PARTS
    chains = [_merge_rows(x_ref, ya_ref, yr_ref, gate_ref, wba_ref, wbr_ref, wo_ref, lnm_ref, wr_ref, br_ref,
                          x1_ref, h2_ref, route_ref, cnt_ref, cnt_scr, j * part, part)
              for j in range(MERGE_ROW_PARTS)]
    for _ in zip(*chains):
        pass


def _merge_rows(x_ref, ya_ref, yr_ref, gate_ref, wba_ref, wbr_ref, wo_ref, lnm_ref, wr_ref, br_ref,
                x1_ref, h2_ref, route_ref, cnt_ref, cnt_scr, r0, tm):
    D = D_MODEL
    rows = slice(r0, r0 + tm)
    gates = gate_ref[rows, :].astype(F32)
    ya = jnp.dot(ya_ref[rows, :], wba_ref[...], preferred_element_type=F32)
    yr = jnp.dot(yr_ref[rows, :], wbr_ref[...], preferred_element_type=F32)
    yield
    merged = gates[:, :D] * ya + gates[:, D:] * yr
    x1 = x_ref[rows, :] + jnp.dot(merged.astype(BF16), wo_ref[...], preferred_element_type=F32)
    x1_ref[rows, :] = x1
    yield
    h2 = _rms(x1, lnm_ref[...])
    _store_slabs(h2_ref, h2, r0)

    h_hi = h2.astype(BF16)
    h_lo = (h2 - h_hi.astype(F32)).astype(BF16)
    parts = (jnp.dot(h_hi, wr_ref[...], preferred_element_type=F32)
             + jnp.dot(h_lo, wr_ref[...], preferred_element_type=F32))
    yield
    logits = parts[:, :LANES] + parts[:, LANES:] + br_ref[...]
    lane = lax.broadcasted_iota(jnp.int32, logits.shape, 1)
    big = jnp.int32(1 << 20)
    gl = jnp.where(lane < N_GROUPS, logits, NEG_INF)
    gmax = jnp.max(gl, axis=-1, keepdims=True)
    yield
    gidx = jnp.min(jnp.where(gl == gmax, lane, big), axis=-1, keepdims=True)
    gsum = jnp.sum(jnp.where(lane < N_GROUPS, jnp.exp(logits - gmax), 0.0), axis=-1, keepdims=True)
    yield
    g_w = 1.0 / gsum
    lo = N_GROUPS + EXPERTS_PER_GROUP * gidx
    el = jnp.where((lane >= lo) & (lane < lo + EXPERTS_PER_GROUP), logits, NEG_INF)
    m1 = jnp.max(el, axis=-1, keepdims=True)
    yield
    i1 = jnp.min(jnp.where(el == m1, lane, big), axis=-1, keepdims=True)
    yield
    el2 = jnp.where(lane == i1, NEG_INF, el)
    m2 = jnp.max(el2, axis=-1, keepdims=True)
    yield
    i2 = jnp.min(jnp.where(el2 == m2, lane, big), axis=-1, keepdims=True)
    yield
    d = jnp.exp(m2 - m1)
    w1 = g_w / (1.0 + d)
    w2 = g_w * d / (1.0 + d)
    e1 = i1 - N_GROUPS
    e2 = i2 - N_GROUPS

    hit1 = lane == e1
    hit2 = lane == e2
    onehot = jnp.where(hit1 | hit2, 1.0, 0.0).astype(BF16)
    r_i = lax.broadcasted_iota(jnp.int32, (tm, tm), 0)
    c_i = lax.broadcasted_iota(jnp.int32, (tm, tm), 1)
    before = jnp.dot(jnp.where(r_i > c_i, 1.0, 0.0).astype(BF16), onehot, preferred_element_type=F32)
    yield
    before = before + cnt_scr[...]
    rank1 = jnp.sum(jnp.where(hit1, before, 0.0), axis=-1, keepdims=True)
    rank2 = jnp.sum(jnp.where(hit2, before, 0.0), axis=-1, keepdims=True)
    cnt_scr[...] = cnt_scr[...] + jnp.sum(onehot.astype(F32), axis=0, keepdims=True)
    cnt_ref[...] = cnt_scr[...]

    route = jnp.where(lane == 0, e1.astype(F32), 0.0)
    route = jnp.where(lane == 1, e2.astype(F32), route)
    route = jnp.where(lane == 2, rank1, route)
    route = jnp.where(lane == 3, rank2, route)
    route = jnp.where(lane == 4, w1, route)
    route = jnp.where(lane == 5, w2, route)
    route_ref[rows, :] = route[:, :ROUTE_COLS]
    yield


def _merge(x2, y_att, y_rwkv, gates, wba, wbr, wo, ln_moe, w_router, b_router):
    T, D = x2.shape
    tm = MERGE_TILE
    row = lambda n: pl.BlockSpec((tm, n), lambda i: (i, 0))
    full = lambda a: pl.BlockSpec(a.shape, lambda i: (0,) * a.ndim)
    ln_moe = ln_moe.reshape(1, D)
    return pl.pallas_call(
        _merge_kernel,
        out_shape=(jax.ShapeDtypeStruct((T, D), F32), jax.ShapeDtypeStruct((T * SLAB_ROWS, LANES), F32),
                   jax.ShapeDtypeStruct((T, ROUTE_COLS), F32), jax.ShapeDtypeStruct((1, LANES), F32)),
        grid=(T // tm,),
        in_specs=[row(D), row(ATT_WIDTH), row(RWKV_WIDTH), row(GATE_COLS),
                  full(wba), full(wbr), full(wo), full(ln_moe), full(w_router), full(b_router)],
        out_specs=(row(D), pl.BlockSpec((tm * SLAB_ROWS, LANES), lambda i: (i, 0)), row(ROUTE_COLS),
                   pl.BlockSpec((1, LANES), lambda i: (0, 0))),
        scratch_shapes=[pltpu.VMEM((1, LANES), F32)],
        compiler_params=_params("arbitrary"),
    )(x2, y_att, y_rwkv, gates, wba, wbr, wo, ln_moe, w_router, b_router)


def _dest_kernel(starts_ref, route_t_ref, dest_ref):
    expert = route_t_ref[0:2, :].astype(jnp.int32)
    first_row = jnp.zeros_like(expert)
    for e in range(N_EXPERTS):
        first_row = jnp.where(expert == e, starts_ref[e], first_row)
    dest_ref[...] = first_row + route_t_ref[2:4, :].astype(jnp.int32)


def _dest_rows(route_t, starts):
    T = route_t.shape[1]
    tl = min(T, 8192)
    assert T % tl == 0
    return pl.pallas_call(
        _dest_kernel,
        out_shape=jax.ShapeDtypeStruct((2, T), jnp.int32),
        grid_spec=pltpu.PrefetchScalarGridSpec(
            num_scalar_prefetch=1,
            grid=(T // tl,),
            in_specs=[pl.BlockSpec((ROUTE_COLS, tl), lambda i, s: (0, i))],
            out_specs=pl.BlockSpec((2, tl), lambda i, s: (0, i)),
        ),
        compiler_params=_params("arbitrary"),
    )(starts, route_t)


def _dispatch_kernel(dest_ref, h_ref, xrows_hbm, row_sem, *, tile, tokens):
    base = pl.program_id(0) * tile

    def row_copy(j, slot):
        return pltpu.make_async_copy(_slab(h_ref, j), _slab(xrows_hbm, dest_ref[slot * tokens + base + j]), row_sem)

    def issue(j, carry):
        row_copy(j, 0).start(priority=0)
        row_copy(j, 1).start(priority=1)
        return carry

    lax.fori_loop(0, tile, issue, 0, unroll=8)

    def drain(j, carry):
        row_copy(j, 0).wait()
        row_copy(j, 1).wait()
        return carry

    lax.fori_loop(0, tile, drain, 0, unroll=8)


def _dispatch(h2_slabs, dest):
    T = h2_slabs.shape[0] // SLAB_ROWS
    tile = min(DISPATCH_TILE, T)
    assert T % tile == 0
    return pl.pallas_call(
        functools.partial(_dispatch_kernel, tile=tile, tokens=T),
        out_shape=jax.ShapeDtypeStruct((2 * T * SLAB_ROWS, LANES), F32),
        grid_spec=pltpu.PrefetchScalarGridSpec(
            num_scalar_prefetch=1,
            grid=(T // tile,),
            in_specs=[pl.BlockSpec((tile * SLAB_ROWS, LANES), lambda i, d: (i, 0))],
            out_specs=pl.BlockSpec(memory_space=pl.ANY),
            scratch_shapes=[pltpu.SemaphoreType.DMA],
        ),
        compiler_params=_params("arbitrary"),
    )(dest, h2_slabs)


def _expert_kernel(item_e, item_b, item_lo, item_hi, x_ref, wg_ref, wu_ref, wd_ref, y_ref, wgu_bf, wd_bf):
    w = pl.program_id(0)
    lo = item_lo[w]
    hi = item_hi[w]

    @pl.when((w == 0) | (item_e[w] != item_e[jnp.maximum(w - 1, 0)]))
    def _():
        wgu_bf[:, :D_EXPERT] = wg_ref[0].astype(BF16)
        wgu_bf[:, D_EXPERT:] = wu_ref[0].astype(BF16)
        wd_bf[...] = wd_ref[0].astype(BF16)

    def run(keep_other_rows):
        bm = x_ref.shape[0] // SLAB_ROWS
        xb = _load_slabs(x_ref, bm).astype(BF16)
        hgu = jnp.dot(xb, wgu_bf[...], preferred_element_type=F32)
        hg = hgu[:, :D_EXPERT]
        hid = hg * _sigmoid(hg) * hgu[:, D_EXPERT:]
        y = jnp.dot(hid.astype(BF16), wd_bf[...], preferred_element_type=F32)
        if keep_other_rows:
            rows = lax.broadcasted_iota(jnp.int32, y.shape, 0)
            y = jnp.where((rows >= lo) & (rows < hi), y, _load_slabs(y_ref, bm))
        _store_slabs(y_ref, y)

    pl.when((hi > lo) & (lo == 0))(functools.partial(run, False))
    pl.when((hi > lo) & (lo > 0))(functools.partial(run, True))


def _experts(xrows, items, wg, wu, wd):
    bm = EXPERT_BLOCK
    D = D_MODEL
    n_items = items[0].shape[0]
    block = pl.BlockSpec((bm * SLAB_ROWS, LANES), lambda w, ie, ib, lo, hi: (ib[w], 0))
    return pl.pallas_call(
        _expert_kernel,
        out_shape=jax.ShapeDtypeStruct(xrows.shape, F32),
        grid_spec=pltpu.PrefetchScalarGridSpec(
            num_scalar_prefetch=4,
            grid=(n_items,),
            in_specs=[block,
                      pl.BlockSpec((1, D, D_EXPERT), lambda w, ie, ib, lo, hi: (ie[w], 0, 0)),
                      pl.BlockSpec((1, D, D_EXPERT), lambda w, ie, ib, lo, hi: (ie[w], 0, 0)),
                      pl.BlockSpec((1, D_EXPERT, D), lambda w, ie, ib, lo, hi: (ie[w], 0, 0))],
            out_specs=block,
            scratch_shapes=[pltpu.VMEM((D, 2 * D_EXPERT), BF16), pltpu.VMEM((D_EXPERT, D), BF16)],
        ),
        compiler_params=_params("arbitrary"),
    )(*items, xrows, wg, wu, wd)


def _work_items(counts, n_rows):
    bm = EXPERT_BLOCK
    nb = n_rows // bm
    n_items = nb + N_EXPERTS
    ends = jnp.cumsum(counts)
    starts = ends - counts
    first_blk = starts // bm
    last_blk = (ends - 1) // bm
    per_e = jnp.where(counts > 0, last_blk - first_blk + 1, 0)
    item_end = jnp.cumsum(per_e)
    total = item_end[-1]
    w = jnp.arange(n_items, dtype=jnp.int32)
    wc = jnp.minimum(w, total - 1)
    e = jnp.minimum(jnp.sum((item_end[None, :] <= wc[:, None]).astype(jnp.int32), axis=1), N_EXPERTS - 1)
    b = first_blk[e] + (wc - (item_end[e] - per_e[e]))
    lo = jnp.maximum(starts[e], b * bm) - b * bm
    hi = jnp.minimum(ends[e], (b + 1) * bm) - b * bm
    live = w < total
    lo = jnp.where(live, lo, 0)
    hi = jnp.where(live, hi, 0)
    return (e.astype(jnp.int32), b.astype(jnp.int32), lo.astype(jnp.int32), hi.astype(jnp.int32)), starts


def _final_kernel(dest_ref, yrows_hbm, x1_ref, route_ref, p_ref, lnp_ref, wpg_ref, wpp_ref, lnf_ref,
                  o_ref, rows_ref, row_sems):
    i = pl.program_id(0)
    n = pl.num_programs(0)
    tm = x1_ref.shape[0]
    tokens = n * tm

    def row_copy(tile, j, slot, par):
        return pltpu.make_async_copy(_slab(yrows_hbm, dest_ref[slot * tokens + tile * tm + j]),
                                     _slab(rows_ref.at[par, slot], j), row_sems.at[par])

    def wait_tile(par):
        pltpu.make_async_copy(rows_ref.at[par], rows_ref.at[par], row_sems.at[par]).wait()

    @pl.when(i == 0)
    def _():
        def body(j, carry):
            row_copy(0, j, 0, 0).start(priority=0)
            row_copy(0, j, 1, 0).start(priority=1)
            return carry
        lax.fori_loop(0, tm, body, 0, unroll=8)

    def step(par):
        wait_tile(par)
        nxt = jnp.minimum(i + 1, n - 1)
        for j in range(tm):
            row_copy(nxt, j, 0, 1 - par).start(priority=0)
            row_copy(nxt, j, 1, 1 - par).start(priority=1)
        pp = jnp.dot(p_ref[...].astype(BF16), wpp_ref[...], preferred_element_type=F32)
        route = route_ref[...]
        x2 = (x1_ref[...] + route[:, 4:5] * _load_slabs(rows_ref.at[par, 0], tm)
              + route[:, 5:6] * _load_slabs(rows_ref.at[par, 1], tm))
        gate = _sigmoid(jnp.dot(_rms(x2, lnp_ref[...]).astype(BF16), wpg_ref[...], preferred_element_type=F32))
        x3 = x2 + gate * pp
        o_ref[...] = _rms(x3, lnf_ref[...])

        @pl.when(i == n - 1)
        def _():
            wait_tile(1 - par)

    for par in range(2):
        pl.when(i % 2 == par)(functools.partial(step, par))


def _final(dest, yrows, x1, route, p2, ln_ple, wpg, wpp, ln_final):
    T, D = x1.shape
    tm = GATHER_TILE
    row = lambda n: pl.BlockSpec((tm, n), lambda i, d: (i, 0))
    full = lambda a: pl.BlockSpec(a.shape, lambda i, d: (0,) * a.ndim)
    ln_ple = ln_ple.reshape(1, D)
    ln_final = ln_final.reshape(1, D)
    return pl.pallas_call(
        _final_kernel,
        out_shape=jax.ShapeDtypeStruct((T, D), F32),
        grid_spec=pltpu.PrefetchScalarGridSpec(
            num_scalar_prefetch=1,
            grid=(T // tm,),
            in_specs=[pl.BlockSpec(memory_space=pl.ANY),
                      row(D), row(ROUTE_COLS), row(PLE_DIM), full(ln_ple), full(wpg), full(wpp), full(ln_final)],
            out_specs=row(D),
            scratch_shapes=[pltpu.VMEM((2, 2, tm * SLAB_ROWS, LANES), F32), pltpu.SemaphoreType.DMA((2,))],
        ),
        compiler_params=_params("arbitrary"),
    )(dest, yrows, x1, route, p2, ln_ple, wpg, wpp, ln_final)


def kernel(x, p, positions, ln_mix, w_in, mu_shift, w0, w_decay_up, a0, w_aaa_up, w_gate_up, k_k, k_a, r_k, ln_x_w, ln_x_b, sinks, w_branch_att, w_branch_rwkv, w_out, ln_moe, w_group, b_group, w_expert, b_expert, w_gate_e, w_up_e, w_down_e, ln_ple, w_ple_gate, w_ple_proj, ln_final):
    B, S, D = x.shape
    T = B * S
    depth = w_in.shape[0]
    assert D == D_MODEL and S % ROW_TILE == 0 and S % (WINDOW * ATT_BLOCKS_PER_STEP) == 0 and S % (RWKV_CHUNK * RWKV_CHUNKS_PER_STEP) == 0
    assert T % GATHER_TILE == 0 and (2 * T) % EXPERT_BLOCK == 0
    cos, sin = _rope_tables(positions)
    x2 = x.reshape(T, D)
    out = None
    for i in range(depth):
        q, k, v, zr, gates = _inproj(x2, ln_mix[i], w_in[i].astype(BF16), mu_shift[i], cos, sin, S)
        y_att = _attention(q, k, v, sinks[i], B, S)
        y_rwkv = _rwkv(zr, w0[i], w_decay_up[i], a0[i], w_aaa_up[i], w_gate_up[i], k_k[i], k_a[i], r_k[i],
                       ln_x_w[i], ln_x_b[i], B, S)
        pad = LANES - N_GROUPS - N_EXPERTS
        w_router = jnp.concatenate([w_group[i], w_expert[i], jnp.zeros((D, pad), F32)], axis=1)
        w_router_hi = w_router.astype(BF16)
        w_router = jnp.concatenate([w_router_hi, (w_router - w_router_hi.astype(F32)).astype(BF16)], axis=1)
        b_router = jnp.concatenate([b_group[i], b_expert[i], jnp.zeros((pad,), F32)]).reshape(1, LANES)
        x1, h2, route, cnt = _merge(x2, y_att, y_rwkv, gates, w_branch_att[i].astype(BF16),
                                    w_branch_rwkv[i].astype(BF16), w_out[i].astype(BF16), ln_moe[i],
                                    w_router, b_router)
        counts = cnt[0, :N_EXPERTS].astype(jnp.int32)
        items, starts = _work_items(counts, 2 * T)
        dest = _dest_rows(route.T, starts.astype(jnp.int32)).reshape(-1)
        xrows = _dispatch(h2, dest)
        yrows = _experts(xrows, items, w_gate_e[i], w_up_e[i], w_down_e[i])
        last = i == depth - 1
        assert last, "the final-norm kernel closes the only layer"
        out = _final(dest, yrows, x1, route, p[i].reshape(T, PLE_DIM), ln_ple[i], w_ple_gate[i].astype(BF16),
                     w_ple_proj[i].astype(BF16), ln_final)
    return out.reshape(B, S, D)
```

```python
import functools
import math

import jax
import jax.numpy as jnp
from jax import lax
from jax.experimental import pallas as pl
from jax.experimental.pallas import tpu as pltpu

F32 = jnp.float32
BF16 = jnp.bfloat16

D_MODEL = 1024
PLE_DIM = 256
ATT_HEADS = 8
ATT_KV_HEADS = 2
HEAD_DIM = 64
ATT_WIDTH = ATT_HEADS * HEAD_DIM
KV_WIDTH = ATT_KV_HEADS * HEAD_DIM
WINDOW = 128
ROPE_THETA = 10000.0
RWKV_HEADS = 8
RWKV_HEAD = 64
RWKV_WIDTH = RWKV_HEADS * RWKV_HEAD
DECAY_LORA = 64
AAA_LORA = 64
GATE_LORA = 128
RWKV_GN_EPS = 64e-5
ATT_COLS = ATT_WIDTH + 2 * KV_WIDTH
SHIFT_COLS = 3 * RWKV_WIDTH + DECAY_LORA + AAA_LORA + GATE_LORA
GATE_COLS = 2 * D_MODEL
N_GROUPS = 4
EXPERTS_PER_GROUP = 8
N_EXPERTS = N_GROUPS * EXPERTS_PER_GROUP
D_EXPERT = 512
NORM_EPS = 1e-6
NEG_INF = -1e30

LANES = 128
ROUTE_COLS = 8
SLAB_ROWS = D_MODEL // LANES
VMEM_LIMIT = 56 * 1024 * 1024

ROW_TILE = 512
MERGE_TILE = 1024
MERGE_ROW_PARTS = 4
ATT_BLOCKS_PER_STEP = 8
RWKV_CHUNK = 64
RWKV_CHUNKS_PER_STEP = 8
RWKV_PACK = 2
RWKV_STAGE_GROUP = 4
EXPERT_BLOCK = 1024
GATHER_TILE = 256
DISPATCH_TILE = 512


def _params(*sem):
    return pltpu.CompilerParams(dimension_semantics=sem, vmem_limit_bytes=VMEM_LIMIT)


def _bdot(a, b):
    return jnp.dot(a.astype(BF16), b.astype(BF16), preferred_element_type=F32)


def _rms(x, g):
    return x * lax.rsqrt(jnp.mean(x * x, axis=-1, keepdims=True) + NORM_EPS) * g


def _sigmoid(x):
    return 1.0 / (1.0 + jnp.exp(-x))


def _store_slabs(ref, val, first=0):
    m = val.shape[0]
    for j in range(SLAB_ROWS):
        ref[pl.ds(first * SLAB_ROWS + j, m, stride=SLAB_ROWS), :] = val[:, j * LANES:(j + 1) * LANES]


def _load_slabs(ref, m, first=0):
    return jnp.concatenate([ref[pl.ds(first * SLAB_ROWS + j, m, stride=SLAB_ROWS), :] for j in range(SLAB_ROWS)],
                           axis=1)


def _slab(ref, index):
    return ref.at[pl.ds(pl.multiple_of(index * SLAB_ROWS, SLAB_ROWS), SLAB_ROWS)]


def _rope_table_kernel(pos_ref, invf_ref, cos_ref, sin_ref):
    half = HEAD_DIM // 2
    per_row = LANES // half
    rows = pos_ref.shape[0]
    group = lax.broadcasted_iota(jnp.int32, (rows, LANES), 1) // half
    pos = pos_ref[...].astype(F32)
    pos_lanes = jnp.broadcast_to(pos[:, per_row - 1:per_row], (rows, LANES))
    for m in range(per_row - 1):
        pos_lanes = jnp.where(group == m, pos[:, m:m + 1], pos_lanes)
    ang = pos_lanes * invf_ref[...]
    sign = jnp.where(group % 2 == 0, -1.0, 1.0)
    for table, out_ref, scale in ((jnp.cos(ang), cos_ref, None), (jnp.sin(ang), sin_ref, sign)):
        rolled = [table] + [pltpu.roll(table, half * j, 1) for j in range(1, per_row)]
        for m in range(per_row):
            out = rolled[(per_row - 1 - m) % per_row]
            for g in range(per_row - 1):
                out = jnp.where(group == g, rolled[(g - m) % per_row], out)
            out_ref[pl.ds(m, rows, stride=per_row), :] = out if scale is None else out * scale


def _rope_tables(positions):
    T = positions.size
    half = HEAD_DIM // 2
    per_row = LANES // half
    inv_freq = ROPE_THETA ** (-jnp.arange(half, dtype=F32) / half)
    invf = jnp.tile(inv_freq, per_row).reshape(1, LANES)
    pos = positions.reshape(T // per_row, per_row)
    rows = T // per_row
    tr = min(rows, 1024)
    return pl.pallas_call(
        _rope_table_kernel,
        out_shape=(jax.ShapeDtypeStruct((T, LANES), F32),) * 2,
        grid=(rows // tr,),
        in_specs=[pl.BlockSpec((tr, per_row), lambda i: (i, 0)),
                  pl.BlockSpec((1, LANES), lambda i: (0, 0))],
        out_specs=(pl.BlockSpec((tr * per_row, LANES), lambda i: (i, 0)),) * 2,
        compiler_params=_params("arbitrary"),
    )(pos, invf)


def _rope(t, cos, sin):
    n = t.shape[1]
    reps = n // LANES
    c = jnp.tile(cos, (1, reps)) if reps > 1 else cos
    s = jnp.tile(sin, (1, reps)) if reps > 1 else sin
    lane = lax.broadcasted_iota(jnp.int32, t.shape, 1)
    first_half = (lane % HEAD_DIM) < (HEAD_DIM // 2)
    partner = jnp.where(first_half, pltpu.roll(t, n - HEAD_DIM // 2, 1), pltpu.roll(t, HEAD_DIM // 2, 1))
    return t * c + partner * s


def _inproj_kernel(x_ref, g_ref, w_ref, mu_ref, cos_ref, sin_ref,
                   q_ref, k_ref, v_ref, zr_ref, gate_ref, carry_ref, *, tiles_per_seq):
    i = pl.program_id(0)
    tm = x_ref.shape[0]
    h = _rms(x_ref[...], g_ref[...]).astype(BF16)
    cos = cos_ref[...]
    sin = sin_ref[...]

    za = jnp.dot(h, w_ref[:, :ATT_COLS], preferred_element_type=F32)
    q_ref[...] = (_rope(za[:, :ATT_WIDTH], cos, sin) * (HEAD_DIM ** -0.5)).astype(BF16)

    def per_head_doubled(t):
        lane = lax.broadcasted_iota(jnp.int32, t.shape, 1)
        swapped = pltpu.roll(t, HEAD_DIM, 1)
        low = lane < HEAD_DIM
        return jnp.concatenate([jnp.where(low, t, swapped), jnp.where(low, swapped, t)], axis=1).astype(BF16)

    k_ref[...] = per_head_doubled(_rope(za[:, ATT_WIDTH:ATT_WIDTH + KV_WIDTH], cos, sin))
    v_ref[...] = per_head_doubled(za[:, ATT_WIDTH + KV_WIDTH:])

    zs = jnp.dot(h, w_ref[:, ATT_COLS:ATT_COLS + SHIFT_COLS], preferred_element_type=F32)
    row = lax.broadcasted_iota(jnp.int32, zs.shape, 0)
    seq_start = (i % tiles_per_seq) == 0
    before = jnp.where(seq_start, 0.0, carry_ref[0:1, :])
    prev = jnp.where(row == 0, before, pltpu.roll(zs, 1, 0))
    carry_ref[0:1, :] = zs[tm - 1:tm, :]
    zr_ref[...] = zs + (prev - zs) * mu_ref[...]

    zg = jnp.dot(h, w_ref[:, ATT_COLS + SHIFT_COLS:], preferred_element_type=F32)
    gate_ref[...] = _sigmoid(zg).astype(BF16)


def _inproj(x2, ln, w_in, mu, cos, sin, seq):
    T, D = x2.shape
    tm = ROW_TILE
    row = lambda n: pl.BlockSpec((tm, n), lambda i: (i, 0))
    full = lambda a: pl.BlockSpec(a.shape, lambda i: (0,) * a.ndim)
    resident = lambda a: pl.BlockSpec(a.shape, lambda i: (0,) * a.ndim, pipeline_mode=pl.Buffered(1))
    ln = ln.reshape(1, D)
    mu = mu.reshape(1, SHIFT_COLS)
    return pl.pallas_call(
        functools.partial(_inproj_kernel, tiles_per_seq=seq // tm),
        out_shape=(jax.ShapeDtypeStruct((T, ATT_WIDTH), BF16),
                   jax.ShapeDtypeStruct((T, 2 * KV_WIDTH), BF16),
                   jax.ShapeDtypeStruct((T, 2 * KV_WIDTH), BF16),
                   jax.ShapeDtypeStruct((T, SHIFT_COLS), F32),
                   jax.ShapeDtypeStruct((T, GATE_COLS), BF16)),
        grid=(T // tm,),
        in_specs=[row(D), full(ln), resident(w_in), full(mu), row(LANES), row(LANES)],
        out_specs=(row(ATT_WIDTH), row(2 * KV_WIDTH), row(2 * KV_WIDTH), row(SHIFT_COLS), row(GATE_COLS)),
        scratch_shapes=[pltpu.VMEM((8, SHIFT_COLS), F32)],
        compiler_params=_params("arbitrary"),
    )(x2, ln, w_in, mu, cos, sin)


def _attn_kernel(sink_ref, q_ref, kp_ref, kc_ref, vp_ref, vc_ref, o_ref):
    n = pl.program_id(1)
    bq = WINDOW
    grp = ATT_HEADS // ATT_KV_HEADS
    gw = grp * HEAD_DIM
    kall = jnp.concatenate([kp_ref[...], kc_ref[...]], axis=0)
    vall = jnp.concatenate([vp_ref[...], vc_ref[...]], axis=0)
    qi = lax.broadcasted_iota(jnp.int32, (grp * bq, 2 * bq), 0) % bq
    si = lax.broadcasted_iota(jnp.int32, (grp * bq, 2 * bq), 1)
    diff = qi + bq - si
    band = (diff >= 0) & (diff < WINDOW)
    head_row = lax.broadcasted_iota(jnp.int32, (grp * bq, 1), 0) // bq
    lane_head = lax.broadcasted_iota(jnp.int32, (bq, gw), 1) // HEAD_DIM
    zero = jnp.zeros((), BF16)
    for blk in range(q_ref.shape[0] // bq):
        valid = band & ((si >= bq) | (n > 0)) if blk == 0 else band
        outs = []
        for j in range(ATT_KV_HEADS):
            qg = q_ref[blk * bq:(blk + 1) * bq, j * gw:(j + 1) * gw]
            lhs = jnp.concatenate([jnp.where(lane_head == g, qg, zero) for g in range(grp)], axis=0)
            k_half = kall[blk * bq:(blk + 2) * bq, j * 2 * HEAD_DIM:(j + 1) * 2 * HEAD_DIM]
            v_half = vall[blk * bq:(blk + 2) * bq, j * 2 * HEAD_DIM:(j + 1) * 2 * HEAD_DIM]
            k_rep = jnp.concatenate([k_half, k_half], axis=1)
            v_rep = jnp.concatenate([v_half, v_half], axis=1)
            s = lax.dot_general(lhs, k_rep, (((1,), (1,)), ((), ())), preferred_element_type=F32)
            s = jnp.where(valid, s, NEG_INF)
            sink = jnp.zeros((grp * bq, 1), F32)
            for g in range(grp):
                sink = jnp.where(head_row == g, sink_ref[j * grp + g], sink)
            m = jnp.maximum(jnp.max(s, axis=-1, keepdims=True), sink)
            e = jnp.exp(s - m)
            denom = jnp.sum(e, axis=-1, keepdims=True) + jnp.exp(sink - m)
            pv = jnp.dot(e.astype(BF16), v_rep, preferred_element_type=F32) / denom
            out = pv[:bq]
            for g in range(1, grp):
                out = jnp.where(lane_head == g, pv[g * bq:(g + 1) * bq], out)
            outs.append(out)
        o_ref[blk * bq:(blk + 1) * bq, :] = jnp.concatenate(outs, axis=1).astype(o_ref.dtype)


def _attention(q, k, v, sinks, batch, seq):
    T = q.shape[0]
    bq = WINDOW
    per_step = ATT_BLOCKS_PER_STEP
    ns = seq // (bq * per_step)
    cur = lambda b, n: (b * ns + n, 0)
    prev = lambda b, n: (jnp.maximum((b * ns + n) * per_step - 1, 0), 0)
    return pl.pallas_call(
        _attn_kernel,
        out_shape=jax.ShapeDtypeStruct((T, ATT_WIDTH), BF16),
        grid=(batch, ns),
        in_specs=[pl.BlockSpec(memory_space=pltpu.SMEM),
                  pl.BlockSpec((bq * per_step, ATT_WIDTH), cur),
                  pl.BlockSpec((bq, 2 * KV_WIDTH), prev), pl.BlockSpec((bq * per_step, 2 * KV_WIDTH), cur),
                  pl.BlockSpec((bq, 2 * KV_WIDTH), prev), pl.BlockSpec((bq * per_step, 2 * KV_WIDTH), cur)],
        out_specs=pl.BlockSpec((bq * per_step, ATT_WIDTH), cur),
        compiler_params=_params("arbitrary", "arbitrary"),
    )(sinks, q, k, k, v, v)


def _rwkv_kernel(z_ref, w0_ref, wd_ref, a0_ref, wa_ref, wg_ref, kk_ref, ka_ref, rk_ref, lnw_ref, lnb_ref,
                 y_ref, s_ref):
    c = pl.program_id(1)

    @pl.when(c == 0)
    def _():
        s_ref[...] = jnp.zeros_like(s_ref)

    states = _rwkv_rows(z_ref, w0_ref, wd_ref, a0_ref, wa_ref, wg_ref, kk_ref, ka_ref, rk_ref, lnw_ref, lnb_ref,
                        y_ref, [s_ref[pj] for pj in range(s_ref.shape[0])])
    for pj, state in enumerate(states):
        s_ref[pj] = state


def _rwkv_rows(z_ref, w0_ref, wd_ref, a0_ref, wa_ref, wg_ref, kk_ref, ka_ref, rk_ref, lnw_ref, lnb_ref, y_ref, states):
    L = RWKV_CHUNK
    rows = z_ref.shape[0]
    n_chunks = rows // L
    C = RWKV_WIDTH
    N = RWKV_HEAD
    r = z_ref[:, 0:C]
    k = z_ref[:, C:2 * C]
    v = z_ref[:, 2 * C:3 * C]
    xw = z_ref[:, 3 * C:3 * C + DECAY_LORA]
    xa = z_ref[:, 3 * C + DECAY_LORA:3 * C + DECAY_LORA + AAA_LORA]
    xg = z_ref[:, 3 * C + DECAY_LORA + AAA_LORA:]

    wlin = w0_ref[...] + _bdot(jnp.tanh(xw), wd_ref[...])
    logdecay = -math.exp(-0.5) * _sigmoid(wlin)
    a = _sigmoid(a0_ref[...] + _bdot(xa, wa_ref[...]))
    g = _bdot(_sigmoid(xg), wg_ref[...])
    kk = k * kk_ref[...]
    k2 = k * (1.0 + (a - 1.0) * ka_ref[...])

    slab = 2 * LANES
    hr = lax.broadcasted_iota(jnp.int32, (slab, slab), 0) // N
    hc = lax.broadcasted_iota(jnp.int32, (slab, slab), 1) // N
    head_ones = jnp.where(hr == hc, 1.0, 0.0).astype(BF16)

    def head_sum(t):
        t = t.astype(BF16)
        return jnp.concatenate([jnp.dot(t[:, j * slab:(j + 1) * slab], head_ones, preferred_element_type=F32)
                                for j in range(C // slab)], axis=1)

    kkn = kk / jnp.maximum(jnp.sqrt(head_sum(kk * kk)), 1e-12)

    assert L == N and (RWKV_PACK * N) % LANES == 0, "the packed block-diagonal products need chunk == head size"
    row = lax.broadcasted_iota(jnp.int32, (L, L), 0)
    col = lax.broadcasted_iota(jnp.int32, (L, L), 1)
    tri = jnp.where(row >= col, 1.0, 0.0).astype(BF16)
    ld_1 = logdecay.astype(BF16)
    ld_2 = (logdecay - ld_1.astype(F32)).astype(BF16)
    cums = []
    for ci in range(n_chunks):
        rs = slice(ci * L, (ci + 1) * L)
        cums.append(jnp.dot(tri, ld_1[rs], preferred_element_type=F32)
                    + jnp.dot(tri, ld_2[rs], preferred_element_type=F32))
    cum = jnp.concatenate(cums, axis=0) if n_chunks > 1 else cums[0]
    last = [cums[ci][L - 1:L, :] for ci in range(n_chunks)]
    cum_last = jnp.concatenate([jnp.broadcast_to(t, (L, C)) for t in last], axis=0) if n_chunks > 1 \
        else jnp.broadcast_to(last[0], (L, C))
    p_in = jnp.exp(cum)
    p_inv = jnp.exp(-cum)
    p_rem = jnp.exp(cum_last - cum)
    b = kkn * a
    at_f = (-kkn * jnp.exp(cum - logdecay)).astype(BF16)
    rt_f = r * p_in
    rt_b = rt_f.astype(BF16)
    bt_b = (b * p_inv).astype(BF16)
    kt_b = (k2 * p_inv).astype(BF16)
    br_b = (b * p_rem).astype(BF16)
    kr_b = (k2 * p_rem).astype(BF16)
    v_b = v.astype(BF16)

    dot = lambda x, y: jnp.dot(x.astype(BF16), y.astype(BF16), preferred_element_type=F32)
    dot_nt = lambda x, y: lax.dot_general(x.astype(BF16), y.astype(BF16), (((1,), (1,)), ((), ())),
                                          preferred_element_type=F32)
    dot_tn = lambda x, y: lax.dot_general(x.astype(BF16), y.astype(BF16), (((0,), (0,)), ((), ())),
                                          preferred_element_type=F32)

    PW = RWKV_PACK * N
    n_groups = C // PW
    lane_head = lax.broadcasted_iota(jnp.int32, (L, PW), 1) // N
    pcol = lax.broadcasted_iota(jnp.int32, (L, PW), 1) - lane_head * N
    prow = lax.broadcasted_iota(jnp.int32, (L, PW), 0)
    p_lower = prow > pcol
    p_lower_eq = prow >= pcol
    p_eye = jnp.where(prow == pcol, 1.0, 0.0)
    zero = jnp.zeros((), BF16)

    def bdiag(t):
        t = t.astype(BF16)
        return jnp.concatenate([jnp.where(lane_head == h, t, zero) for h in range(RWKV_PACK)], axis=0)

    def pack_rows(t):
        out = t[:N]
        for h in range(1, RWKV_PACK):
            out = jnp.where(lane_head == h, t[h * N:(h + 1) * N], out)
        return out

    cut = lambda t, ci, pj: t[ci * L:(ci + 1) * L, pj * PW:(pj + 1) * PW]
    same_head = (lax.broadcasted_iota(jnp.int32, (PW, PW), 0) // N) == \
                (lax.broadcasted_iota(jnp.int32, (PW, PW), 1) // N)
    rq, y0, gg, hh = {}, {}, {}, {}
    for g0 in range(0, n_chunks, RWKV_STAGE_GROUP):
        pieces = [(ci, pj) for ci in range(g0, min(g0 + RWKV_STAGE_GROUP, n_chunks)) for pj in range(n_groups)]
        _rwkv_chunk_algebra(pieces, cut, bdiag, dot, dot_nt, dot_tn, L, PW, at_f, rt_f, rt_b, bt_b, kt_b, br_b, kr_b,
                            v_b, p_lower, p_lower_eq, p_eye, same_head, pack_rows, rq, y0, gg, hh)

    ys = {}
    new_states = []
    for pj in range(n_groups):
        state = states[pj]
        for ci in range(n_chunks):
            p = (ci, pj)
            sb = state.astype(BF16)
            ys[p] = y0[p] + dot_nt(rq[p], bdiag(sb))
            p_tot = p_in[(ci + 1) * L - 1:(ci + 1) * L, pj * PW:(pj + 1) * PW]
            state = state * p_tot + dot(sb, gg[p]) + hh[p]
        new_states.append(state)
    y = jnp.concatenate([jnp.concatenate([ys[(ci, pj)] for pj in range(n_groups)], axis=1)
                         for ci in range(n_chunks)], axis=0)

    mu = head_sum(y) * (1.0 / N)
    dev = y - mu
    var = head_sum(dev * dev) * (1.0 / N)
    yn = dev * lax.rsqrt(var + RWKV_GN_EPS)
    bonus = head_sum(r * k2 * rk_ref[...]) * v
    y_ref[...] = ((yn * lnw_ref[...] + lnb_ref[...] + bonus) * g).astype(y_ref.dtype)
    return new_states


def _rwkv_chunk_algebra(pieces, cut, bdiag, dot, dot_nt, dot_tn, L, PW, at_f, rt_f, rt_b, bt_b, kt_b, br_b, kr_b, v_b,
                        p_lower, p_lower_eq, p_eye, same_head, pack_rows, rq_out, y0_out, gg_out, hh_out):
    at = {p: cut(at_f, *p) for p in pieces}
    vv = {p: cut(v_b, *p) for p in pieces}
    m = {p: dot_nt(jnp.concatenate([at[p], cut(rt_b, *p)], axis=0),
                   jnp.concatenate([bdiag(cut(bt_b, *p)), bdiag(cut(kt_b, *p))], axis=0)) for p in pieces}
    m_ab = {p: jnp.where(p_lower, m[p][:L, :PW], 0.0) for p in pieces}
    m_ak = {p: jnp.where(p_lower, m[p][:L, PW:], 0.0).astype(BF16) for p in pieces}
    m_rb = {p: jnp.where(p_lower_eq, m[p][L:, :PW], 0.0).astype(BF16) for p in pieces}
    m_rk = {p: jnp.where(p_lower_eq, m[p][L:, PW:], 0.0).astype(BF16) for p in pieces}
    inv = {p: p_eye + m_ab[p] for p in pieces}
    pw = {p: m_ab[p].astype(BF16) for p in pieces}
    pw = {p: dot(pw[p], bdiag(pw[p])).astype(BF16) for p in pieces}
    for _ in range(max(2, (L - 1).bit_length()) - 2):
        both = {p: dot(jnp.concatenate([pw[p], inv[p].astype(BF16)], axis=0), bdiag(pw[p])) for p in pieces}
        pw = {p: both[p][:L].astype(BF16) for p in pieces}
        inv = {p: inv[p] + both[p][L:] for p in pieces}
    inv = {p: (inv[p] + dot(inv[p], bdiag(pw[p]))).astype(BF16) for p in pieces}
    mv = {p: dot(jnp.concatenate([m_ak[p], m_rk[p]], axis=0), bdiag(vv[p])) for p in pieces}
    wu = {p: dot(inv[p], jnp.concatenate([bdiag(at[p]), bdiag(mv[p][:L])], axis=1)) for p in pieces}
    w = {p: wu[p][:, :PW].astype(BF16) for p in pieces}
    u0 = {p: wu[p][:, PW:].astype(BF16) for p in pieces}
    ry = {p: dot(m_rb[p], jnp.concatenate([bdiag(w[p]), bdiag(u0[p])], axis=1)) for p in pieces}
    rq_out.update({p: (cut(rt_f, *p) + ry[p][:, :PW]).astype(BF16) for p in pieces})
    y0_out.update({p: ry[p][:, PW:] + mv[p][L:] for p in pieces})
    gg_out.update({p: jnp.where(same_head, dot_tn(w[p], cut(br_b, *p)), 0.0).astype(BF16) for p in pieces})
    hh_full = {p: dot_tn(jnp.concatenate([u0[p], vv[p]], axis=0),
                         jnp.concatenate([cut(br_b, *p), cut(kr_b, *p)], axis=0)) for p in pieces}
    hh_out.update({p: pack_rows(hh_full[p]) for p in pieces})


def _rwkv(zr, w0, wd, a0, wa, wg, k_k, k_a, r_k, ln_w, ln_b, batch, seq):
    T = zr.shape[0]
    L = RWKV_CHUNK * RWKV_CHUNKS_PER_STEP
    nc = seq // L
    vec = lambda a: a.reshape(1, RWKV_WIDTH)
    full = lambda a: pl.BlockSpec(a.shape, lambda b, c: (0,) * a.ndim)
    args = (vec(w0), wd, vec(a0), wa, wg, vec(k_k), vec(k_a), vec(r_k), vec(ln_w), vec(ln_b))
    return pl.pallas_call(
        _rwkv_kernel,
        out_shape=jax.ShapeDtypeStruct((T, RWKV_WIDTH), BF16),
        grid=(batch, nc),
        in_specs=[pl.BlockSpec((L, SHIFT_COLS), lambda b, c: (b * nc + c, 0))] + [full(a) for a in args],
        out_specs=pl.BlockSpec((L, RWKV_WIDTH), lambda b, c: (b * nc + c, 0)),
        scratch_shapes=[pltpu.VMEM((RWKV_HEADS // RWKV_PACK, RWKV_HEAD, RWKV_PACK * RWKV_HEAD), F32)],
        compiler_params=_params("arbitrary", "arbitrary"),
    )(zr, *args)


def _merge_kernel(x_ref, ya_ref, yr_ref, gate_ref, wba_ref, wbr_ref, wo_ref, lnm_ref, wr_ref, br_ref,
                  x1_ref, h2_ref, route_ref, cnt_ref, cnt_scr):
    i = pl.program_id(0)

    @pl.when(i == 0)
    def _():
        cnt_scr[...] = jnp.zeros_like(cnt_scr)

    part = x_ref.shape[0] // MERGE_ROW_PARTS
    chains = [_merge_rows(x_ref, ya_ref, yr_ref, gate_ref, wba_ref, wbr_ref, wo_ref, lnm_ref, wr_ref, br_ref,
                          x1_ref, h2_ref, route_ref, cnt_ref, cnt_scr, j * part, part)
              for j in range(MERGE_ROW_PARTS)]
    for _ in zip(*chains):
        pass


def _merge_rows(x_ref, ya_ref, yr_ref, gate_ref, wba_ref, wbr_ref, wo_ref, lnm_ref, wr_ref, br_ref,
                x1_ref, h2_ref, route_ref, cnt_ref, cnt_scr, r0, tm):
    D = D_MODEL
    rows = slice(r0, r0 + tm)
    gates = gate_ref[rows, :].astype(F32)
    ya = jnp.dot(ya_ref[rows, :], wba_ref[...], preferred_element_type=F32)
    yr = jnp.dot(yr_ref[rows, :], wbr_ref[...], preferred_element_type=F32)
    yield
    merged = gates[:, :D] * ya + gates[:, D:] * yr
    x1 = x_ref[rows, :] + jnp.dot(merged.astype(BF16), wo_ref[...], preferred_element_type=F32)
    x1_ref[rows, :] = x1
    yield
    h2 = _rms(x1, lnm_ref[...])
    _store_slabs(h2_ref, h2, r0)

    h_hi = h2.astype(BF16)
    h_lo = (h2 - h_hi.astype(F32)).astype(BF16)
    parts = (jnp.dot(h_hi, wr_ref[...], preferred_element_type=F32)
             + jnp.dot(h_lo, wr_ref[...], preferred_element_type=F32))
    yield
    logits = parts[:, :LANES] + parts[:, LANES:] + br_ref[...]
    lane = lax.broadcasted_iota(jnp.int32, logits.shape, 1)
    big = jnp.int32(1 << 20)
    gl = jnp.where(lane < N_GROUPS, logits, NEG_INF)
    gmax = jnp.max(gl, axis=-1, keepdims=True)
    yield
    gidx = jnp.min(jnp.where(gl == gmax, lane, big), axis=-1, keepdims=True)
    gsum = jnp.sum(jnp.where(lane < N_GROUPS, jnp.exp(logits - gmax), 0.0), axis=-1, keepdims=True)
    yield
    g_w = 1.0 / gsum
    lo = N_GROUPS + EXPERTS_PER_GROUP * gidx
    el = jnp.where((lane >= lo) & (lane < lo + EXPERTS_PER_GROUP), logits, NEG_INF)
    m1 = jnp.max(el, axis=-1, keepdims=True)
    yield
    i1 = jnp.min(jnp.where(el == m1, lane, big), axis=-1, keepdims=True)
    yield
    el2 = jnp.where(lane == i1, NEG_INF, el)
    m2 = jnp.max(el2, axis=-1, keepdims=True)
    yield
    i2 = jnp.min(jnp.where(el2 == m2, lane, big), axis=-1, keepdims=True)
    yield
    d = jnp.exp(m2 - m1)
    w1 = g_w / (1.0 + d)
    w2 = g_w * d / (1.0 + d)
    e1 = i1 - N_GROUPS
    e2 = i2 - N_GROUPS

    hit1 = lane == e1
    hit2 = lane == e2
    onehot = jnp.where(hit1 | hit2, 1.0, 0.0).astype(BF16)
    r_i = lax.broadcasted_iota(jnp.int32, (tm, tm), 0)
    c_i = lax.broadcasted_iota(jnp.int32, (tm, tm), 1)
    before = jnp.dot(jnp.where(r_i > c_i, 1.0, 0.0).astype(BF16), onehot, preferred_element_type=F32)
    yield
    before = before + cnt_scr[...]
    rank1 = jnp.sum(jnp.where(hit1, before, 0.0), axis=-1, keepdims=True)
    rank2 = jnp.sum(jnp.where(hit2, before, 0.0), axis=-1, keepdims=True)
    cnt_scr[...] = cnt_scr[...] + jnp.sum(onehot.astype(F32), axis=0, keepdims=True)
    cnt_ref[...] = cnt_scr[...]

    route = jnp.where(lane == 0, e1.astype(F32), 0.0)
    route = jnp.where(lane == 1, e2.astype(F32), route)
    route = jnp.where(lane == 2, rank1, route)
    route = jnp.where(lane == 3, rank2, route)
    route = jnp.where(lane == 4, w1, route)
    route = jnp.where(lane == 5, w2, route)
    route_ref[rows, :] = route[:, :ROUTE_COLS]
    yield


def _merge(x2, y_att, y_rwkv, gates, wba, wbr, wo, ln_moe, w_router, b_router):
    T, D = x2.shape
    tm = MERGE_TILE
    row = lambda n: pl.BlockSpec((tm, n), lambda i: (i, 0))
    full = lambda a: pl.BlockSpec(a.shape, lambda i: (0,) * a.ndim)
    ln_moe = ln_moe.reshape(1, D)
    return pl.pallas_call(
        _merge_kernel,
        out_shape=(jax.ShapeDtypeStruct((T, D), F32), jax.ShapeDtypeStruct((T * SLAB_ROWS, LANES), F32),
                   jax.ShapeDtypeStruct((T, ROUTE_COLS), F32), jax.ShapeDtypeStruct((1, LANES), F32)),
        grid=(T // tm,),
        in_specs=[row(D), row(ATT_WIDTH), row(RWKV_WIDTH), row(GATE_COLS),
                  full(wba), full(wbr), full(wo), full(ln_moe), full(w_router), full(b_router)],
        out_specs=(row(D), pl.BlockSpec((tm * SLAB_ROWS, LANES), lambda i: (i, 0)), row(ROUTE_COLS),
                   pl.BlockSpec((1, LANES), lambda i: (0, 0))),
        scratch_shapes=[pltpu.VMEM((1, LANES), F32)],
        compiler_params=_params("arbitrary"),
    )(x2, y_att, y_rwkv, gates, wba, wbr, wo, ln_moe, w_router, b_router)


def _dest_kernel(starts_ref, route_t_ref, dest_ref):
    expert = route_t_ref[0:2, :].astype(jnp.int32)
    first_row = jnp.zeros_like(expert)
    for e in range(N_EXPERTS):
        first_row = jnp.where(expert == e, starts_ref[e], first_row)
    dest_ref[...] = first_row + route_t_ref[2:4, :].astype(jnp.int32)


def _dest_rows(route_t, starts):
    T = route_t.shape[1]
    tl = min(T, 8192)
    assert T % tl == 0
    return pl.pallas_call(
        _dest_kernel,
        out_shape=jax.ShapeDtypeStruct((2, T), jnp.int32),
        grid_spec=pltpu.PrefetchScalarGridSpec(
            num_scalar_prefetch=1,
            grid=(T // tl,),
            in_specs=[pl.BlockSpec((ROUTE_COLS, tl), lambda i, s: (0, i))],
            out_specs=pl.BlockSpec((2, tl), lambda i, s: (0, i)),
        ),
        compiler_params=_params("arbitrary"),
    )(starts, route_t)


def _dispatch_kernel(dest_ref, h_ref, xrows_hbm, row_sem, *, tile, tokens):
    base = pl.program_id(0) * tile

    def row_copy(j, slot):
        return pltpu.make_async_copy(_slab(h_ref, j), _slab(xrows_hbm, dest_ref[slot * tokens + base + j]), row_sem)

    def issue(j, carry):
        row_copy(j, 0).start(priority=0)
        row_copy(j, 1).start(priority=1)
        return carry

    lax.fori_loop(0, tile, issue, 0, unroll=8)

    def drain(j, carry):
        row_copy(j, 0).wait()
        row_copy(j, 1).wait()
        return carry

    lax.fori_loop(0, tile, drain, 0, unroll=8)


def _dispatch(h2_slabs, dest):
    T = h2_slabs.shape[0] // SLAB_ROWS
    tile = min(DISPATCH_TILE, T)
    assert T % tile == 0
    return pl.pallas_call(
        functools.partial(_dispatch_kernel, tile=tile, tokens=T),
        out_shape=jax.ShapeDtypeStruct((2 * T * SLAB_ROWS, LANES), F32),
        grid_spec=pltpu.PrefetchScalarGridSpec(
            num_scalar_prefetch=1,
            grid=(T // tile,),
            in_specs=[pl.BlockSpec((tile * SLAB_ROWS, LANES), lambda i, d: (i, 0))],
            out_specs=pl.BlockSpec(memory_space=pl.ANY),
            scratch_shapes=[pltpu.SemaphoreType.DMA],
        ),
        compiler_params=_params("arbitrary"),
    )(dest, h2_slabs)


def _expert_kernel(item_e, item_b, item_lo, item_hi, x_ref, wg_ref, wu_ref, wd_ref, y_ref, wgu_bf, wd_bf):
    w = pl.program_id(0)
    lo = item_lo[w]
    hi = item_hi[w]

    @pl.when((w == 0) | (item_e[w] != item_e[jnp.maximum(w - 1, 0)]))
    def _():
        wgu_bf[:, :D_EXPERT] = wg_ref[0].astype(BF16)
        wgu_bf[:, D_EXPERT:] = wu_ref[0].astype(BF16)
        wd_bf[...] = wd_ref[0].astype(BF16)

    def run(keep_other_rows):
        bm = x_ref.shape[0] // SLAB_ROWS
        xb = _load_slabs(x_ref, bm).astype(BF16)
        hgu = jnp.dot(xb, wgu_bf[...], preferred_element_type=F32)
        hg = hgu[:, :D_EXPERT]
        hid = hg * _sigmoid(hg) * hgu[:, D_EXPERT:]
        y = jnp.dot(hid.astype(BF16), wd_bf[...], preferred_element_type=F32)
        if keep_other_rows:
            rows = lax.broadcasted_iota(jnp.int32, y.shape, 0)
            y = jnp.where((rows >= lo) & (rows < hi), y, _load_slabs(y_ref, bm))
        _store_slabs(y_ref, y)

    pl.when((hi > lo) & (lo == 0))(functools.partial(run, False))
    pl.when((hi > lo) & (lo > 0))(functools.partial(run, True))


def _experts(xrows, items, wg, wu, wd):
    bm = EXPERT_BLOCK
    D = D_MODEL
    n_items = items[0].shape[0]
    block = pl.BlockSpec((bm * SLAB_ROWS, LANES), lambda w, ie, ib, lo, hi: (ib[w], 0))
    return pl.pallas_call(
        _expert_kernel,
        out_shape=jax.ShapeDtypeStruct(xrows.shape, F32),
        grid_spec=pltpu.PrefetchScalarGridSpec(
            num_scalar_prefetch=4,
            grid=(n_items,),
            in_specs=[block,
                      pl.BlockSpec((1, D, D_EXPERT), lambda w, ie, ib, lo, hi: (ie[w], 0, 0)),
                      pl.BlockSpec((1, D, D_EXPERT), lambda w, ie, ib, lo, hi: (ie[w], 0, 0)),
                      pl.BlockSpec((1, D_EXPERT, D), lambda w, ie, ib, lo, hi: (ie[w], 0, 0))],
            out_specs=block,
            scratch_shapes=[pltpu.VMEM((D, 2 * D_EXPERT), BF16), pltpu.VMEM((D_EXPERT, D), BF16)],
        ),
        compiler_params=_params("arbitrary"),
    )(*items, xrows, wg, wu, wd)


def _work_items(counts, n_rows):
    bm = EXPERT_BLOCK
    nb = n_rows // bm
    n_items = nb + N_EXPERTS
    ends = jnp.cumsum(counts)
    starts = ends - counts
    first_blk = starts // bm
    last_blk = (ends - 1) // bm
    per_e = jnp.where(counts > 0, last_blk - first_blk + 1, 0)
    item_end = jnp.cumsum(per_e)
    total = item_end[-1]
    w = jnp.arange(n_items, dtype=jnp.int32)
    wc = jnp.minimum(w, total - 1)
    e = jnp.minimum(jnp.sum((item_end[None, :] <= wc[:, None]).astype(jnp.int32), axis=1), N_EXPERTS - 1)
    is_e = e[:, None] == jnp.arange(N_EXPERTS, dtype=e.dtype)[None, :]
    pick = lambda table: jnp.sum(jnp.where(is_e, table[None, :], 0), axis=1)
    b = pick(first_blk) + (wc - pick(item_end - per_e))
    lo = jnp.maximum(pick(starts), b * bm) - b * bm
    hi = jnp.minimum(pick(ends), (b + 1) * bm) - b * bm
    live = w < total
    lo = jnp.where(live, lo, 0)
    hi = jnp.where(live, hi, 0)
    return (e.astype(jnp.int32), b.astype(jnp.int32), lo.astype(jnp.int32), hi.astype(jnp.int32)), starts


def _final_kernel(dest_ref, yrows_hbm, x1_ref, route_ref, p_ref, lnp_ref, wpg_ref, wpp_ref, lnf_ref,
                  o_ref, rows_ref, row_sems):
    i = pl.program_id(0)
    n = pl.num_programs(0)
    tm = x1_ref.shape[0]
    tokens = n * tm

    def row_copy(tile, j, slot, par):
        return pltpu.make_async_copy(_slab(yrows_hbm, dest_ref[slot * tokens + tile * tm + j]),
                                     _slab(rows_ref.at[par, slot], j), row_sems.at[par])

    def wait_tile(par):
        pltpu.make_async_copy(rows_ref.at[par], rows_ref.at[par], row_sems.at[par]).wait()

    @pl.when(i == 0)
    def _():
        def body(j, carry):
            row_copy(0, j, 0, 0).start(priority=0)
            row_copy(0, j, 1, 0).start(priority=1)
            return carry
        lax.fori_loop(0, tm, body, 0, unroll=8)

    def step(par):
        wait_tile(par)
        nxt = jnp.minimum(i + 1, n - 1)
        for j in range(tm):
            row_copy(nxt, j, 0, 1 - par).start(priority=0)
            row_copy(nxt, j, 1, 1 - par).start(priority=1)
        pp = jnp.dot(p_ref[...].astype(BF16), wpp_ref[...], preferred_element_type=F32)
        route = route_ref[...]
        x2 = (x1_ref[...] + route[:, 4:5] * _load_slabs(rows_ref.at[par, 0], tm)
              + route[:, 5:6] * _load_slabs(rows_ref.at[par, 1], tm))
        gate = _sigmoid(jnp.dot(_rms(x2, lnp_ref[...]).astype(BF16), wpg_ref[...], preferred_element_type=F32))
        x3 = x2 + gate * pp
        o_ref[...] = _rms(x3, lnf_ref[...])

        @pl.when(i == n - 1)
        def _():
            wait_tile(1 - par)

    for par in range(2):
        pl.when(i % 2 == par)(functools.partial(step, par))


def _final(dest, yrows, x1, route, p2, ln_ple, wpg, wpp, ln_final):
    T, D = x1.shape
    tm = GATHER_TILE
    row = lambda n: pl.BlockSpec((tm, n), lambda i, d: (i, 0))
    full = lambda a: pl.BlockSpec(a.shape, lambda i, d: (0,) * a.ndim)
    ln_ple = ln_ple.reshape(1, D)
    ln_final = ln_final.reshape(1, D)
    return pl.pallas_call(
        _final_kernel,
        out_shape=jax.ShapeDtypeStruct((T, D), F32),
        grid_spec=pltpu.PrefetchScalarGridSpec(
            num_scalar_prefetch=1,
            grid=(T // tm,),
            in_specs=[pl.BlockSpec(memory_space=pl.ANY),
                      row(D), row(ROUTE_COLS), row(PLE_DIM), full(ln_ple), full(wpg), full(wpp), full(ln_final)],
            out_specs=row(D),
            scratch_shapes=[pltpu.VMEM((2, 2, tm * SLAB_ROWS, LANES), F32), pltpu.SemaphoreType.DMA((2,))],
        ),
        compiler_params=_params("arbitrary"),
    )(dest, yrows, x1, route, p2, ln_ple, wpg, wpp, ln_final)


def kernel(x, p, positions, ln_mix, w_in, mu_shift, w0, w_decay_up, a0, w_aaa_up, w_gate_up, k_k, k_a, r_k, ln_x_w, ln_x_b, sinks, w_branch_att, w_branch_rwkv, w_out, ln_moe, w_group, b_group, w_expert, b_expert, w_gate_e, w_up_e, w_down_e, ln_ple, w_ple_gate, w_ple_proj, ln_final):
    B, S, D = x.shape
    T = B * S
    depth = w_in.shape[0]
    assert D == D_MODEL and S % ROW_TILE == 0 and S % (WINDOW * ATT_BLOCKS_PER_STEP) == 0 and S % (RWKV_CHUNK * RWKV_CHUNKS_PER_STEP) == 0
    assert T % GATHER_TILE == 0 and (2 * T) % EXPERT_BLOCK == 0
    cos, sin = _rope_tables(positions)
    x2 = x.reshape(T, D)
    out = None
    for i in range(depth):
        q, k, v, zr, gates = _inproj(x2, ln_mix[i], w_in[i].astype(BF16), mu_shift[i], cos, sin, S)
        y_att = _attention(q, k, v, sinks[i], B, S)
        y_rwkv = _rwkv(zr, w0[i], w_decay_up[i], a0[i], w_aaa_up[i], w_gate_up[i], k_k[i], k_a[i], r_k[i],
                       ln_x_w[i], ln_x_b[i], B, S)
        pad = LANES - N_GROUPS - N_EXPERTS
        w_router = jnp.concatenate([w_group[i], w_expert[i], jnp.zeros((D, pad), F32)], axis=1)
        w_router_hi = w_router.astype(BF16)
        w_router = jnp.concatenate([w_router_hi, (w_router - w_router_hi.astype(F32)).astype(BF16)], axis=1)
        b_router = jnp.concatenate([b_group[i], b_expert[i], jnp.zeros((pad,), F32)]).reshape(1, LANES)
        x1, h2, route, cnt = _merge(x2, y_att, y_rwkv, gates, w_branch_att[i].astype(BF16),
                                    w_branch_rwkv[i].astype(BF16), w_out[i].astype(BF16), ln_moe[i],
                                    w_router, b_router)
        counts = cnt[0, :N_EXPERTS].astype(jnp.int32)
        items, starts = _work_items(counts, 2 * T)
        dest = _dest_rows(route.T, starts.astype(jnp.int32)).reshape(-1)
        xrows = _dispatch(h2, dest)
        yrows = _experts(xrows, items, w_gate_e[i], w_up_e[i], w_down_e[i])
        last = i == depth - 1
        assert last, "the final-norm kernel closes the only layer"
        out = _final(dest, yrows, x1, route, p[i].reshape(T, PLE_DIM), ln_ple[i], w_ple_gate[i].astype(BF16),
                     w_ple_proj[i].astype(BF16), ln_final)
    return out.reshape(B, S, D)
```
